```python
import jax
import jax.numpy as jnp
from jax import lax
import numpy as np

D_MODEL = 4096
BATCH = 4
SEQ = 2048
DEPTH = 2

N_MIXERS = 2
N_MEM = 256
CHUNK = 128
EPS = 1e-6
SGU_GROUPS = 8
SGU_WIDTH = D_MODEL
MLSTM_HEADS = 8
QK_DIM = D_MODEL // 2 // MLSTM_HEADS
V_DIM = D_MODEL // MLSTM_HEADS
MLSTM_PROJ = 2 * MLSTM_HEADS * QK_DIM + 2 * MLSTM_HEADS * V_DIM + 2 * MLSTM_HEADS
CONV_WIDTH = 4
GATE_CAP = 15.0
X_HEADS = 4
X_HEAD_DIM = D_MODEL // X_HEADS
D_FF_DENSE = 7 * D_MODEL // 2
N_EXPERTS = 8
TOP_K = 2
D_FF_EXPERT = D_MODEL

kernel_name = 'hybrid_sgu_mlstm_memxattn_moe'


def _rmsnorm(x, g):
    xf = x.astype(jnp.float32)
    y = xf * lax.rsqrt(jnp.mean(xf * xf, axis=-1, keepdims=True) + EPS)
    return (y * g.astype(jnp.float32)).astype(x.dtype)


def _layernorm(x, g, b):
    xf = x.astype(jnp.float32)
    xc = xf - jnp.mean(xf, axis=-1, keepdims=True)
    y = xc * lax.rsqrt(jnp.mean(xc * xc, axis=-1, keepdims=True) + EPS)
    return (y * g.astype(jnp.float32) + b.astype(jnp.float32)).astype(x.dtype)


def _causal_tril(n):
    return jnp.tril(jnp.ones((n, n), dtype=bool))


def spatial_gating_mixer(h, w_in, ln_g, ln_b, sgu_w, sgu_b, w_out):
    B, S, _ = h.shape
    n_chunks = S // CHUNK
    gd = SGU_WIDTH // SGU_GROUPS
    z = jax.nn.gelu(h @ w_in, approximate=False)
    u, v = jnp.split(z, 2, axis=-1)
    v = _layernorm(v, ln_g, ln_b).reshape(B, n_chunks, CHUNK, SGU_GROUPS, gd)
    w_s = jnp.where(_causal_tril(CHUNK), sgu_w, 0).astype(v.dtype)
    mixed = jnp.einsum('gts,bcsgd->bctgd', w_s, v) + sgu_b.T[None, None, :, :, None]
    y = u * mixed.reshape(B, S, SGU_WIDTH).astype(u.dtype)
    return y @ w_out


def _causal_conv(x, w):
    S = x.shape[1]
    xp = jnp.pad(x, ((0, 0), (CONV_WIDTH - 1, 0), (0, 0)))
    return sum(xp[:, j:j + S] * w[j] for j in range(CONV_WIDTH))


def mlstm_chunkwise(q, k, v, i_pre, f_pre):
    B, H, S, dk = q.shape
    dv = v.shape[-1]
    n_chunks = S // CHUNK
    q = q.astype(jnp.float32).reshape(B, H, n_chunks, CHUNK, dk)
    k = k.astype(jnp.float32).reshape(B, H, n_chunks, CHUNK, dk)
    v = v.astype(jnp.float32).reshape(B, H, n_chunks, CHUNK, dv)
    i_pre = i_pre.reshape(B, H, n_chunks, CHUNK)
    logf = jax.nn.log_sigmoid(f_pre).reshape(B, H, n_chunks, CHUNK)
    b = jnp.cumsum(logf, axis=-1)
    g = b[..., -1]
    w_log = g[..., None] - b + i_pre
    a = jnp.max(w_log, axis=-1)
    w = jnp.exp(w_log - a[..., None])
    c_loc = jnp.einsum('bhcsv,bhcsk->bhcvk', w[..., None] * v, k)
    n_loc = jnp.einsum('bhcs,bhcsk->bhck', w, k)

    def step(carry, xs):
        c_st, n_st, m_st = carry
        g_c, a_c, cl, nl = xs
        m_new = jnp.maximum(g_c + m_st, a_c)
        s_prev = jnp.exp(g_c + m_st - m_new)
        s_loc = jnp.exp(a_c - m_new)
        c_new = s_prev[..., None, None] * c_st + s_loc[..., None, None] * cl
        n_new = s_prev[..., None] * n_st + s_loc[..., None] * nl
        return (c_new, n_new, m_new), (c_st, n_st, m_st)

    init = (jnp.zeros((B, H, dv, dk), jnp.float32), jnp.zeros((B, H, dk), jnp.float32),
            jnp.zeros((B, H), jnp.float32))
    xs = (jnp.moveaxis(g, 2, 0), jnp.moveaxis(a, 2, 0), jnp.moveaxis(c_loc, 2, 0), jnp.moveaxis(n_loc, 2, 0))
    _, (c0, n0, m0) = lax.scan(step, init, xs)
    c0 = jnp.moveaxis(c0, 0, 2)
    n0 = jnp.moveaxis(n0, 0, 2)
    m0 = jnp.moveaxis(m0, 0, 2)
    d_log = b[..., :, None] - b[..., None, :] + i_pre[..., None, :]
    d_log = jnp.where(_causal_tril(CHUNK), d_log, -jnp.inf)
    m_inter = b + m0[..., None]
    m_t = jnp.maximum(m_inter, jnp.max(d_log, axis=-1))
    scores = jnp.einsum('bhctd,bhcsd->bhcts', q, k) * jnp.exp(d_log - m_t[..., None])
    inter = jnp.exp(m_inter - m_t)
    num = (jnp.einsum('bhcts,bhcsv->bhctv', scores, v)
           + inter[..., None] * jnp.einsum('bhcvd,bhctd->bhctv', c0, q))
    den = jnp.sum(scores, axis=-1) + inter * jnp.einsum('bhcd,bhctd->bhct', n0, q)
    h = num / jnp.maximum(jnp.abs(den), jnp.exp(-m_t))[..., None]
    return h.reshape(B, H, S, dv)


def mlstm_mixer(h, w_in, gate_b, conv_w, head_g, w_out):
    B, S, _ = h.shape
    H = MLSTM_HEADS
    qk_w = 2 * H * QK_DIM
    proj = h @ w_in
    qk, v, o, gates = jnp.split(proj, [qk_w, qk_w + H * V_DIM, qk_w + 2 * H * V_DIM], axis=-1)
    qk = jax.nn.silu(_causal_conv(qk, conv_w))
    q, k = jnp.split(qk, 2, axis=-1)
    gates = gates.astype(jnp.float32) + gate_b.astype(jnp.float32)
    gates = GATE_CAP * jnp.tanh(gates / GATE_CAP)
    i_pre = jnp.transpose(gates[..., :H], (0, 2, 1))
    f_pre = jnp.transpose(gates[..., H:], (0, 2, 1))
    q = jnp.transpose(q.reshape(B, S, H, QK_DIM), (0, 2, 1, 3)) * QK_DIM ** -0.5
    k = jnp.transpose(k.reshape(B, S, H, QK_DIM), (0, 2, 1, 3))
    v = jnp.transpose(v.reshape(B, S, H, V_DIM), (0, 2, 1, 3))
    ht = jnp.transpose(mlstm_chunkwise(q, k, v, i_pre, f_pre), (0, 2, 1, 3))
    hn = ht * lax.rsqrt(jnp.mean(ht * ht, axis=-1, keepdims=True) + EPS)
    hn = hn * head_g.astype(jnp.float32).reshape(H, V_DIM)
    y = jax.nn.sigmoid(o.astype(jnp.float32)) * hn.reshape(B, S, H * V_DIM)
    return y.astype(h.dtype) @ w_out


def memory_cross_attention(h, mem_k, mem_v, w_q, w_o):
    B, S, _ = h.shape
    q = (h @ w_q).reshape(B, S, X_HEADS, X_HEAD_DIM)
    s = jnp.einsum('bshd,bmhd->bhsm', q, mem_k).astype(jnp.float32) * X_HEAD_DIM ** -0.5
    p = jax.nn.softmax(s, axis=-1).astype(mem_v.dtype)
    o = jnp.einsum('bhsm,bmhd->bshd', p, mem_v).reshape(B, S, X_HEADS * X_HEAD_DIM)
    return o @ w_o


def swiglu(h, w_gate, w_up, w_down):
    return (jax.nn.silu(h @ w_gate) * (h @ w_up)) @ w_down


def moe_swiglu(h, w_router, w_gate, w_up, w_down):
    B, S, D = h.shape
    t = h.reshape(B * S, D)
    logits = t.astype(jnp.float32) @ w_router.astype(jnp.float32)
    top_v, top_i = lax.top_k(logits, TOP_K)
    top_w = jax.nn.softmax(top_v, axis=-1)
    combine = jnp.sum(jax.nn.one_hot(top_i, N_EXPERTS, dtype=jnp.float32) * top_w[..., None], axis=1)
    combine = combine.astype(t.dtype)
    out = jnp.zeros_like(t)
    for e in range(N_EXPERTS):
        y_e = swiglu(t, w_gate[e], w_up[e], w_down[e])
        out = out + combine[:, e:e + 1] * y_e
    return out.reshape(B, S, D)


def setup_inputs(seed: int = 0) -> dict:
    key = jax.random.key(seed)
    ks = iter(jax.random.split(key, 64))
    D = D_MODEL
    H = MLSTM_HEADS

    def nrm(shape, scale):
        return jax.random.normal(next(ks), shape, jnp.float32) * scale

    def gain(n):
        return 1.0 + 0.02 * jax.random.normal(next(ks), (n,), jnp.float32)

    gate_b = jnp.concatenate([nrm((H,), 0.1),
                              jnp.linspace(3.0, 6.0, H, dtype=jnp.float32) + nrm((H,), 0.1)])
    return {
        'x': nrm((BATCH, SEQ, D), 1.0),
        'mem': nrm((BATCH, N_MEM, D), 1.0),
        'mem_norm': gain(D),
        'mem_kv': nrm((D, 2 * D), D ** -0.5),
        'l0_norm_mix_pre': gain(D),
        'l0_mix_in': nrm((D, 2 * SGU_WIDTH), D ** -0.5),
        'l0_sgu_ln_g': gain(SGU_WIDTH),
        'l0_sgu_ln_b': nrm((SGU_WIDTH,), 0.02),
        'l0_sgu_w': nrm((SGU_GROUPS, CHUNK, CHUNK), CHUNK ** -0.5),
        'l0_sgu_b': 1.0 + nrm((SGU_GROUPS, CHUNK), 0.02),
        'l0_mix_out': nrm((SGU_WIDTH, D), SGU_WIDTH ** -0.5),
        'l0_norm_mix_post': gain(D),
        'l0_norm_x_pre': gain(D),
        'l0_xq': nrm((D, X_HEADS * X_HEAD_DIM), D ** -0.5),
        'l0_xo': nrm((X_HEADS * X_HEAD_DIM, D), D ** -0.5),
        'l0_norm_x_post': gain(D),
        'l0_norm_ffn_pre': gain(D),
        'l0_ffn_gate': nrm((D, D_FF_DENSE), D ** -0.5),
        'l0_ffn_up': nrm((D, D_FF_DENSE), D ** -0.5),
        'l0_ffn_down': nrm((D_FF_DENSE, D), D_FF_DENSE ** -0.5),
        'l0_norm_ffn_post': gain(D),
        'l1_norm_mix_pre': gain(D),
        'l1_mix_in': nrm((D, MLSTM_PROJ), D ** -0.5),
        'l1_gate_b': gate_b,
        'l1_conv': nrm((CONV_WIDTH, 2 * H * QK_DIM), CONV_WIDTH ** -0.5),
        'l1_head_norm': gain(H * V_DIM),
        'l1_mix_out': nrm((H * V_DIM, D), (H * V_DIM) ** -0.5),
        'l1_norm_mix_post': gain(D),
        'l1_norm_x_pre': gain(D),
        'l1_xq': nrm((D, X_HEADS * X_HEAD_DIM), D ** -0.5),
        'l1_xo': nrm((X_HEADS * X_HEAD_DIM, D), D ** -0.5),
        'l1_norm_x_post': gain(D),
        'l1_norm_ffn_pre': gain(D),
        'l1_router': nrm((D, N_EXPERTS), D ** -0.5),
        'l1_moe_gate': nrm((N_EXPERTS, D, D_FF_EXPERT), D ** -0.5),
        'l1_moe_up': nrm((N_EXPERTS, D, D_FF_EXPERT), D ** -0.5),
        'l1_moe_down': nrm((N_EXPERTS, D_FF_EXPERT, D), D_FF_EXPERT ** -0.5),
        'l1_norm_ffn_post': gain(D),
    }


def reference(x, mem, mem_norm, mem_kv,
              l0_norm_mix_pre, l0_mix_in, l0_sgu_ln_g, l0_sgu_ln_b, l0_sgu_w, l0_sgu_b, l0_mix_out,
              l0_norm_mix_post, l0_norm_x_pre, l0_xq, l0_xo, l0_norm_x_post,
              l0_norm_ffn_pre, l0_ffn_gate, l0_ffn_up, l0_ffn_down, l0_norm_ffn_post,
              l1_norm_mix_pre, l1_mix_in, l1_gate_b, l1_conv, l1_head_norm, l1_mix_out,
              l1_norm_mix_post, l1_norm_x_pre, l1_xq, l1_xo, l1_norm_x_post,
              l1_norm_ffn_pre, l1_router, l1_moe_gate, l1_moe_up, l1_moe_down, l1_norm_ffn_post):
    Bm = mem.shape[0]
    mem_k, mem_v = jnp.split(_rmsnorm(mem, mem_norm) @ mem_kv, 2, axis=-1)
    mem_k = mem_k.reshape(Bm, N_MEM, X_HEADS, X_HEAD_DIM)
    mem_v = mem_v.reshape(Bm, N_MEM, X_HEADS, X_HEAD_DIM)

    layers = (
        {'norm_mix': (l0_norm_mix_pre, l0_norm_mix_post),
         'mixer': (l0_mix_in, l0_sgu_ln_g, l0_sgu_ln_b, l0_sgu_w, l0_sgu_b, l0_mix_out),
         'norm_x': (l0_norm_x_pre, l0_norm_x_post), 'cross': (l0_xq, l0_xo),
         'norm_ffn': (l0_norm_ffn_pre, l0_norm_ffn_post), 'ffn': (l0_ffn_gate, l0_ffn_up, l0_ffn_down)},
        {'norm_mix': (l1_norm_mix_pre, l1_norm_mix_post),
         'mixer': (l1_mix_in, l1_gate_b, l1_conv, l1_head_norm, l1_mix_out),
         'norm_x': (l1_norm_x_pre, l1_norm_x_post), 'cross': (l1_xq, l1_xo),
         'norm_ffn': (l1_norm_ffn_pre, l1_norm_ffn_post),
         'ffn': (l1_router, l1_moe_gate, l1_moe_up, l1_moe_down)},
    )

    for i in range(DEPTH):
        p = layers[i]
        hn = _rmsnorm(x, p['norm_mix'][0])
        if i % N_MIXERS == 0:
            y = spatial_gating_mixer(hn, *p['mixer'])
        else:
            y = mlstm_mixer(hn, *p['mixer'])
        x = x + _rmsnorm(y, p['norm_mix'][1])
        y = memory_cross_attention(_rmsnorm(x, p['norm_x'][0]), mem_k, mem_v, *p['cross'])
        x = x + _rmsnorm(y, p['norm_x'][1])
        hn = _rmsnorm(x, p['norm_ffn'][0])
        if i % 2 == 0:
            y = swiglu(hn, *p['ffn'])
        else:
            y = moe_swiglu(hn, *p['ffn'])
        x = x + _rmsnorm(y, p['norm_ffn'][1])
    return x
```

```python
import functools

import jax
import jax.numpy as jnp
from jax import lax
from jax.experimental import pallas as pl
from jax.experimental.pallas import tpu as pltpu

F32 = jnp.float32
BF16 = jnp.bfloat16

EPS = 1e-6
CHUNK = 128
SGU_GROUPS = 8
MLSTM_HEADS = 8
CONV_WIDTH = 4
GATE_CAP = 15.0
X_HEADS = 4
N_EXPERTS = 8
TOP_K = 2
LANES = 128
MOE_TM = 512
VMEM_LIMIT = 56 * 1024 * 1024


def _cparams(n_axes, vmem=VMEM_LIMIT):
    return pltpu.CompilerParams(
        dimension_semantics=("arbitrary",) * n_axes, vmem_limit_bytes=vmem)


def _tile(pref, dim):
    t = min(pref, dim)
    assert dim % t == 0, (pref, dim)
    return t


def _rms(x, g):
    return x * lax.rsqrt(jnp.mean(x * x, axis=-1, keepdims=True) + EPS) * g


def _sigmoid(x):
    return 1.0 / (1.0 + jnp.exp(-x))


def _rms_kernel(x_ref, g_ref, o_ref):
    o_ref[...] = _rms(x_ref[...], g_ref[...]).astype(o_ref.dtype)


def rmsnorm_rows(x, g, tm=256):
    M, D = x.shape
    tm = _tile(tm, M)
    return pl.pallas_call(
        _rms_kernel,
        grid=(M // tm,),
        in_specs=[pl.BlockSpec((tm, D), lambda i: (i, 0)),
                  pl.BlockSpec((1, D), lambda i: (0, 0))],
        out_specs=pl.BlockSpec((tm, D), lambda i: (i, 0)),
        out_shape=jax.ShapeDtypeStruct((M, D), BF16),
        compiler_params=_cparams(1),
        name="rmsnorm_rows",
    )(x, g.reshape(1, D))


def _gelu(x):
    return 0.5 * x * (1.0 + lax.erf(x * 0.7071067811865476))


def _mm_kernel(a_ref, w_ref, o_ref, wb_ref, *, act):
    @pl.when(pl.program_id(1) == 0)
    def _():
        wb_ref[...] = w_ref[...].astype(BF16)

    acc = jnp.dot(a_ref[...], wb_ref[...], preferred_element_type=F32)
    if act == "gelu":
        acc = _gelu(acc)
    o_ref[...] = acc.astype(o_ref.dtype)


def matmul_fullk(a, w, n_out, out_dtype, act=None, tm=1024, tn=512):
    M, K = a.shape
    tm = _tile(tm, M)
    tn = _tile(tn, n_out)
    return pl.pallas_call(
        functools.partial(_mm_kernel, act=act),
        grid=(n_out // tn, M // tm),
        in_specs=[pl.BlockSpec((tm, K), lambda j, i: (i, 0)),
                  pl.BlockSpec((K, tn), lambda j, i: (0, j))],
        out_specs=pl.BlockSpec((tm, tn), lambda j, i: (i, j)),
        out_shape=jax.ShapeDtypeStruct((M, n_out), out_dtype),
        scratch_shapes=[pltpu.VMEM((K, tn), BF16)],
        compiler_params=_cparams(2),
        name="matmul_fullk" + ("_" + act if act else ""),
    )(a, w)


def _swiglu_kernel(a_ref, wg_ref, wu_ref, o_ref, wgb_ref, wub_ref):
    @pl.when(pl.program_id(1) == 0)
    def _():
        wgb_ref[...] = wg_ref[...].astype(BF16)
        wub_ref[...] = wu_ref[...].astype(BF16)

    a = a_ref[...]
    g = jnp.dot(a, wgb_ref[...], preferred_element_type=F32)
    u = jnp.dot(a, wub_ref[...], preferred_element_type=F32)
    o_ref[...] = (g * _sigmoid(g) * u).astype(o_ref.dtype)


def swiglu_up(a, wg, wu, tm=1024, tn=256):
    M, K = a.shape
    F = wg.shape[1]
    tm = _tile(tm, M)
    tn = _tile(tn, F)
    wspec = pl.BlockSpec((K, tn), lambda j, i: (0, j))
    return pl.pallas_call(
        _swiglu_kernel,
        grid=(F // tn, M // tm),
        in_specs=[pl.BlockSpec((tm, K), lambda j, i: (i, 0)), wspec, wspec],
        out_specs=pl.BlockSpec((tm, tn), lambda j, i: (i, j)),
        out_shape=jax.ShapeDtypeStruct((M, F), BF16),
        scratch_shapes=[pltpu.VMEM((K, tn), BF16), pltpu.VMEM((K, tn), BF16)],
        compiler_params=_cparams(2),
        name="swiglu_up",
    )(a, wg, wu)


def _mmk_kernel(a_ref, w_ref, o_ref, acc_ref, *, nk):
    k = pl.program_id(2)

    @pl.when(k == 0)
    def _():
        acc_ref[...] = jnp.zeros_like(acc_ref)

    acc_ref[...] += jnp.dot(a_ref[...], w_ref[...].astype(BF16),
                            preferred_element_type=F32)

    @pl.when(k == nk - 1)
    def _():
        o_ref[...] = acc_ref[...].astype(o_ref.dtype)


def matmul_tiledk(a, w, out_dtype, tm=2048, tn=1024, tk=512):
    M, K = a.shape
    N = w.shape[1]
    tm, tn, tk = _tile(tm, M), _tile(tn, N), _tile(tk, K)
    nk = K // tk
    return pl.pallas_call(
        functools.partial(_mmk_kernel, nk=nk),
        grid=(N // tn, M // tm, nk),
        in_specs=[pl.BlockSpec((tm, tk), lambda j, i, k: (i, k)),
                  pl.BlockSpec((tk, tn), lambda j, i, k: (k, j))],
        out_specs=pl.BlockSpec((tm, tn), lambda j, i, k: (i, j)),
        out_shape=jax.ShapeDtypeStruct((M, N), out_dtype),
        scratch_shapes=[pltpu.VMEM((tm, tn), F32)],
        compiler_params=_cparams(3),
        name="matmul_tiledk",
    )(a, w)


def _res_kernel(x_ref, y_ref, gp_ref, gn_ref, xo_ref, hn_ref):
    xn = x_ref[...] + _rms(y_ref[...].astype(F32), gp_ref[...])
    xo_ref[...] = xn
    hn_ref[...] = _rms(xn, gn_ref[...]).astype(hn_ref.dtype)


def _res_last_kernel(x_ref, y_ref, gp_ref, xo_ref):
    xo_ref[...] = x_ref[...] + _rms(y_ref[...].astype(F32), gp_ref[...])


def residual_norm(x, y, g_post, g_next, tm=256):
    M, D = x.shape
    tm = _tile(tm, M)
    row = pl.BlockSpec((tm, D), lambda i: (i, 0))
    vec = pl.BlockSpec((1, D), lambda i: (0, 0))
    if g_next is None:
        return pl.pallas_call(
            _res_last_kernel, grid=(M // tm,),
            in_specs=[row, row, vec], out_specs=row,
            out_shape=jax.ShapeDtypeStruct((M, D), F32),
            compiler_params=_cparams(1), name="residual_last",
        )(x, y, g_post.reshape(1, D))
    return pl.pallas_call(
        _res_kernel, grid=(M // tm,),
        in_specs=[row, row, vec, vec], out_specs=[row, row],
        out_shape=[jax.ShapeDtypeStruct((M, D), F32), jax.ShapeDtypeStruct((M, D), BF16)],
        compiler_params=_cparams(1), name="residual_norm",
    )(x, y, g_post.reshape(1, D), g_next.reshape(1, D))


def _sgu_kernel(u_ref, v_ref, lg_ref, lb_ref, w_ref, bt_ref, o_ref, wm_ref, *, groups):
    @pl.when(pl.program_id(0) == 0)
    def _():
        t = lax.broadcasted_iota(jnp.int32, (CHUNK, CHUNK), 0)
        s = lax.broadcasted_iota(jnp.int32, (CHUNK, CHUNK), 1)
        for g in range(groups):
            wm_ref[g] = jnp.where(t >= s, w_ref[g], 0.0).astype(BF16)

    v = v_ref[...].astype(F32)
    vc = v - jnp.mean(v, axis=-1, keepdims=True)
    vn = vc * lax.rsqrt(jnp.mean(vc * vc, axis=-1, keepdims=True) + EPS)
    vn = (vn * lg_ref[...] + lb_ref[...]).astype(BF16)
    tm, width = vn.shape
    gd = width // groups
    for c in range(tm // CHUNK):
        rows = slice(c * CHUNK, (c + 1) * CHUNK)
        for g in range(groups):
            cols = slice(g * gd, (g + 1) * gd)
            mixed = jnp.dot(wm_ref[g], vn[rows, cols], preferred_element_type=F32)
            mixed = mixed + bt_ref[:, g:g + 1]
            o_ref[rows, cols] = (u_ref[rows, cols].astype(F32) * mixed).astype(o_ref.dtype)


def sgu_mix(z, ln_g, ln_b, sgu_w, sgu_b, tm=256):
    M, W2 = z.shape
    W = W2 // 2
    G = sgu_w.shape[0]
    tm = _tile(tm, M)
    return pl.pallas_call(
        functools.partial(_sgu_kernel, groups=G),
        grid=(M // tm,),
        in_specs=[pl.BlockSpec((tm, W), lambda i: (i, 0)),
                  pl.BlockSpec((tm, W), lambda i: (i, 1)),
                  pl.BlockSpec((1, W), lambda i: (0, 0)),
                  pl.BlockSpec((1, W), lambda i: (0, 0)),
                  pl.BlockSpec((G, CHUNK, CHUNK), lambda i: (0, 0, 0)),
                  pl.BlockSpec((CHUNK, G), lambda i: (0, 0))],
        out_specs=pl.BlockSpec((tm, W), lambda i: (i, 0)),
        out_shape=jax.ShapeDtypeStruct((M, W), BF16),
        scratch_shapes=[pltpu.VMEM((G, CHUNK, CHUNK), BF16)],
        compiler_params=_cparams(1),
        name="sgu_mix",
    )(z, z, ln_g.reshape(1, W), ln_b.reshape(1, W), sgu_w, sgu_b.T)


def _xattn_kernel(hn_ref, wq_ref, k_ref, v_ref, o_ref, wqb_ref, *, scale):
    @pl.when((pl.program_id(1) == 0) & (pl.program_id(2) == 0))
    def _():
        wqb_ref[...] = wq_ref[...].astype(BF16)

    q = jnp.dot(hn_ref[...], wqb_ref[...], preferred_element_type=F32)
    s = lax.dot_general(q.astype(BF16), k_ref[0], (((1,), (1,)), ((), ())),
                        preferred_element_type=F32) * scale
    p = jnp.exp(s - jnp.max(s, axis=-1, keepdims=True))
    p = p / jnp.sum(p, axis=-1, keepdims=True)
    o = jnp.dot(p.astype(BF16), v_ref[0], preferred_element_type=F32)
    o_ref[...] = o.astype(o_ref.dtype)


def cross_attention(hn, wq, kv, batch, tm=512):
    M, D = hn.shape
    S = M // batch
    n_mem = kv.shape[1]
    hd = D // X_HEADS
    tm = _tile(tm, S)
    spt = S // tm
    return pl.pallas_call(
        functools.partial(_xattn_kernel, scale=hd ** -0.5),
        grid=(X_HEADS, batch, spt),
        in_specs=[pl.BlockSpec((tm, D), lambda h, b, m: (b * spt + m, 0)),
                  pl.BlockSpec((D, hd), lambda h, b, m: (0, h),
                               pipeline_mode=pl.Buffered(1)),
                  pl.BlockSpec((1, n_mem, hd), lambda h, b, m: (b, 0, h)),
                  pl.BlockSpec((1, n_mem, hd), lambda h, b, m: (b, 0, X_HEADS + h))],
        out_specs=pl.BlockSpec((tm, hd), lambda h, b, m: (b * spt + m, h)),
        out_shape=jax.ShapeDtypeStruct((M, D), BF16),
        scratch_shapes=[pltpu.VMEM((D, hd), BF16)],
        compiler_params=_cparams(3),
        name="cross_attention",
    )(hn, wq, kv, kv)


def _gates_kernel(hn_ref, w_ref, b_ref, o_ref, *, n_gates):
    acc = jnp.dot(hn_ref[...], w_ref[...].astype(BF16), preferred_element_type=F32)
    g = GATE_CAP * jnp.tanh((acc + b_ref[...]) / GATE_CAP)
    o_ref[...] = g.T[:n_gates, :]


def mlstm_gates(hn, w_gates, gate_b, tm=512):
    M, D = hn.shape
    n_gates = w_gates.shape[1]
    tm = _tile(tm, M)
    w_pad = jnp.pad(w_gates, ((0, 0), (0, LANES - n_gates)))
    b_pad = jnp.pad(gate_b, (0, LANES - n_gates)).reshape(1, LANES)
    return pl.pallas_call(
        functools.partial(_gates_kernel, n_gates=n_gates),
        grid=(M // tm,),
        in_specs=[pl.BlockSpec((tm, D), lambda i: (i, 0)),
                  pl.BlockSpec((D, LANES), lambda i: (0, 0)),
                  pl.BlockSpec((1, LANES), lambda i: (0, 0))],
        out_specs=pl.BlockSpec((n_gates, tm), lambda i: (0, i)),
        out_shape=jax.ShapeDtypeStruct((n_gates, M), F32),
        compiler_params=_cparams(1),
        name="mlstm_gates",
    )(hn, w_pad, b_pad)


def _conv_silu(x_ref, prev_ref, w_ref):
    x = x_ref[...].astype(F32)
    prev = prev_ref[...]
    rid = lax.broadcasted_iota(jnp.int32, x.shape, 0)
    acc = x * w_ref[CONV_WIDTH - 1:CONV_WIDTH, :]
    for r in range(1, CONV_WIDTH):
        shifted = jnp.where(rid < r, pltpu.roll(prev, r, 0), pltpu.roll(x, r, 0))
        acc = acc + shifted * w_ref[CONV_WIDTH - 1 - r:CONV_WIDTH - r, :]
    prev_ref[...] = x
    return acc * _sigmoid(acc)


def _mlstm_kernel(q_ref, k_ref, v_ref, o_ref, ig_ref, fg_ref, cq_ref, ck_ref, hg_ref,
                  y_ref, s_ref, n_ref, m_ref, qp_ref, kp_ref):
    @pl.when(pl.program_id(2) == 0)
    def _():
        s_ref[...] = jnp.zeros_like(s_ref)
        n_ref[...] = jnp.zeros_like(n_ref)
        m_ref[...] = jnp.zeros_like(m_ref)
        qp_ref[...] = jnp.zeros_like(qp_ref)
        kp_ref[...] = jnp.zeros_like(kp_ref)

    L = CHUNK
    dk = q_ref.shape[-1]
    q = _conv_silu(q_ref, qp_ref, cq_ref) * dk ** -0.5
    k = _conv_silu(k_ref, kp_ref, ck_ref)
    qb = q.astype(BF16)
    kb = k.astype(BF16)
    vb = v_ref[...]
    v = vb.astype(F32)

    t_id = lax.broadcasted_iota(jnp.int32, (L, L), 0)
    s_id = lax.broadcasted_iota(jnp.int32, (L, L), 1)
    eye = t_id == s_id
    causal = t_id >= s_id

    i_row = ig_ref[0]
    f_row = fg_ref[0]
    logf_row = -(jnp.maximum(-f_row, 0.0) + jnp.log1p(jnp.exp(-jnp.abs(f_row))))
    logf_b = jnp.broadcast_to(logf_row, (L, L))
    i_b = jnp.broadcast_to(i_row, (L, L))
    logf_col = jnp.sum(jnp.where(eye, logf_b, 0.0), axis=1, keepdims=True)
    i_col = jnp.sum(jnp.where(eye, i_b, 0.0), axis=1, keepdims=True)
    b_col = jnp.sum(jnp.where(causal, logf_b, 0.0), axis=1, keepdims=True)
    b_row = jnp.sum(jnp.where(t_id <= s_id, jnp.broadcast_to(logf_col, (L, L)), 0.0),
                    axis=0, keepdims=True)
    g_tot = jnp.sum(logf_row, axis=1, keepdims=True)
    m0 = m_ref[:, 0:1]

    d_log = jnp.where(causal, b_col - b_row + i_row, -jnp.inf)
    m_inter = b_col + m0
    m_t = jnp.maximum(m_inter, jnp.max(d_log, axis=1, keepdims=True))
    qk = lax.dot_general(qb, kb, (((1,), (1,)), ((), ())), preferred_element_type=F32)
    scores = qk * jnp.exp(d_log - m_t)
    inter = jnp.exp(m_inter - m_t)
    num = (jnp.dot(scores.astype(BF16), vb, preferred_element_type=F32)
           + inter * jnp.dot(qb, s_ref[...].astype(BF16), preferred_element_type=F32))
    den = (jnp.sum(scores, axis=1, keepdims=True)
           + inter * jnp.sum(q * n_ref[...], axis=1, keepdims=True))
    h = num / jnp.maximum(jnp.abs(den), jnp.exp(-m_t))
    hn = _rms(h, hg_ref[...])
    y_ref[...] = (_sigmoid(o_ref[...].astype(F32)) * hn).astype(y_ref.dtype)

    w_log = g_tot - b_col + i_col
    a = jnp.max(w_log, axis=0, keepdims=True)
    w = jnp.exp(w_log - a)
    s_loc = lax.dot_general(kb, (w * v).astype(BF16), (((0,), (0,)), ((), ())),
                            preferred_element_type=F32)
    n_loc = jnp.sum(w * k, axis=0, keepdims=True)
    m_new = jnp.maximum(g_tot + m0, a)
    sc_prev = jnp.exp(g_tot + m0 - m_new)
    sc_loc = jnp.exp(a - m_new)
    s_ref[...] = sc_prev * s_ref[...] + sc_loc * s_loc
    n_ref[...] = sc_prev * n_ref[...] + sc_loc * n_loc
    m_ref[...] = jnp.broadcast_to(m_new, m_ref.shape)


def mlstm_core(proj, gates_t, conv_w, head_g, batch):
    M = proj.shape[0]
    H = MLSTM_HEADS
    qkw = conv_w.shape[1]
    dk = qkw // (2 * H)
    dv = head_g.shape[0] // H
    nc = M // batch // CHUNK
    assert qkw % dv == 0
    v0 = qkw // dv
    gates3 = gates_t.reshape(2 * H, 1, M)
    row = lambda b, h, c: b * nc + c
    return pl.pallas_call(
        _mlstm_kernel,
        grid=(batch, H, nc),
        in_specs=[pl.BlockSpec((CHUNK, dk), lambda b, h, c: (row(b, h, c), h)),
                  pl.BlockSpec((CHUNK, dk), lambda b, h, c: (row(b, h, c), H + h)),
                  pl.BlockSpec((CHUNK, dv), lambda b, h, c: (row(b, h, c), v0 + h)),
                  pl.BlockSpec((CHUNK, dv), lambda b, h, c: (row(b, h, c), v0 + H + h)),
                  pl.BlockSpec((1, 1, CHUNK), lambda b, h, c: (h, 0, row(b, h, c))),
                  pl.BlockSpec((1, 1, CHUNK), lambda b, h, c: (H + h, 0, row(b, h, c))),
                  pl.BlockSpec((CONV_WIDTH, dk), lambda b, h, c: (0, h)),
                  pl.BlockSpec((CONV_WIDTH, dk), lambda b, h, c: (0, H + h)),
                  pl.BlockSpec((1, dv), lambda b, h, c: (0, h))],
        out_specs=pl.BlockSpec((CHUNK, dv), lambda b, h, c: (row(b, h, c), h)),
        out_shape=jax.ShapeDtypeStruct((M, H * dv), BF16),
        scratch_shapes=[pltpu.VMEM((dk, dv), F32), pltpu.VMEM((1, dk), F32),
                        pltpu.VMEM((1, LANES), F32),
                        pltpu.VMEM((CHUNK, dk), F32), pltpu.VMEM((CHUNK, dk), F32)],
        compiler_params=_cparams(3),
        name="mlstm_core",
    )(proj, proj, proj, proj, gates3, gates3, conv_w, conv_w, head_g.reshape(1, H * dv))


def _router_kernel(x_ref, g_ref, w_ref, idx_ref, wt_ref):
    hn = _rms(x_ref[...], g_ref[...])
    logits = jnp.dot(hn, w_ref[...], preferred_element_type=F32,
                     precision=lax.Precision.HIGHEST)
    lane = lax.broadcasted_iota(jnp.int32, logits.shape, 1)
    neg = -jnp.inf
    l1 = jnp.where(lane < N_EXPERTS, logits, neg)
    m1 = jnp.max(l1, axis=1, keepdims=True)
    i1 = jnp.min(jnp.where(l1 == m1, lane, LANES), axis=1, keepdims=True)
    l2 = jnp.where(lane == i1, neg, l1)
    m2 = jnp.max(l2, axis=1, keepdims=True)
    i2 = jnp.min(jnp.where(l2 == m2, lane, LANES), axis=1, keepdims=True)
    r = jnp.exp(m2 - m1)
    w1 = 1.0 / (1.0 + r)
    w2 = r / (1.0 + r)
    idx_ref[...] = jnp.where(lane == 0, i1, jnp.where(lane == 1, i2, 0))
    wt_ref[...] = jnp.where(lane == 0, w1, jnp.where(lane == 1, w2, 0.0))


def moe_router(x, g, w_router, tm=256):
    M, D = x.shape
    tm = _tile(tm, M)
    w_pad = jnp.pad(w_router, ((0, 0), (0, LANES - w_router.shape[1])))
    row = pl.BlockSpec((tm, D), lambda i: (i, 0))
    out = pl.BlockSpec((tm, LANES), lambda i: (i, 0))
    return pl.pallas_call(
        _router_kernel, grid=(M // tm,),
        in_specs=[row, pl.BlockSpec((1, D), lambda i: (0, 0)),
                  pl.BlockSpec((D, LANES), lambda i: (0, 0))],
        out_specs=[out, out],
        out_shape=[jax.ShapeDtypeStruct((M, LANES), jnp.int32),
                   jax.ShapeDtypeStruct((M, LANES), F32)],
        compiler_params=_cparams(1), name="moe_router",
    )(x, g.reshape(1, D), w_pad)


def _weights_changed(te_ref, i):
    return (i == 0) | (te_ref[i] != te_ref[jnp.maximum(i - 1, 0)])


def _moe_up_kernel(te_ref, nt_ref, a_ref, wg_ref, wu_ref, o_ref, wgb_ref, wub_ref):
    i = pl.program_id(1)

    @pl.when(_weights_changed(te_ref, i))
    def _():
        wgb_ref[...] = wg_ref[...].astype(BF16)
        wub_ref[...] = wu_ref[...].astype(BF16)

    @pl.when(i < nt_ref[0])
    def _():
        a = a_ref[...]
        g = jnp.dot(a, wgb_ref[...], preferred_element_type=F32)
        u = jnp.dot(a, wub_ref[...], preferred_element_type=F32)
        o_ref[...] = (g * _sigmoid(g) * u).astype(o_ref.dtype)

    @pl.when(i >= nt_ref[0])
    def _():
        o_ref[...] = jnp.zeros_like(o_ref)


def moe_up(xs, wg, wu, tile_expert, n_tiles, tn=256):
    P, D = xs.shape
    F = wg.shape[2]
    tm = MOE_TM
    tn = _tile(tn, F)
    wspec = pl.BlockSpec((None, D, tn), lambda j, i, te, nt: (te[i], 0, j))
    return pl.pallas_call(
        _moe_up_kernel,
        grid_spec=pltpu.PrefetchScalarGridSpec(
            num_scalar_prefetch=2,
            grid=(F // tn, P // tm),
            in_specs=[pl.BlockSpec((tm, D), lambda j, i, te, nt: (i, 0)), wspec, wspec],
            out_specs=pl.BlockSpec((tm, tn), lambda j, i, te, nt: (i, j)),
            scratch_shapes=[pltpu.VMEM((D, tn), BF16), pltpu.VMEM((D, tn), BF16)]),
        out_shape=jax.ShapeDtypeStruct((P, F), BF16),
        compiler_params=_cparams(2),
        name="moe_up",
    )(tile_expert, n_tiles, xs, wg, wu)


def _moe_down_kernel(te_ref, nt_ref, a_ref, w_ref, o_ref, wb_ref):
    i = pl.program_id(1)

    @pl.when(_weights_changed(te_ref, i))
    def _():
        wb_ref[...] = w_ref[...].astype(BF16)

    @pl.when(i < nt_ref[0])
    def _():
        o_ref[...] = jnp.dot(a_ref[...], wb_ref[...],
                             preferred_element_type=F32).astype(o_ref.dtype)

    @pl.when(i >= nt_ref[0])
    def _():
        o_ref[...] = jnp.zeros_like(o_ref)


def moe_down(hs, wd, tile_expert, n_tiles, tn=512):
    P, F = hs.shape
    D = wd.shape[2]
    tm = MOE_TM
    tn = _tile(tn, D)
    return pl.pallas_call(
        _moe_down_kernel,
        grid_spec=pltpu.PrefetchScalarGridSpec(
            num_scalar_prefetch=2,
            grid=(D // tn, P // tm),
            in_specs=[pl.BlockSpec((tm, F), lambda j, i, te, nt: (i, 0)),
                      pl.BlockSpec((None, F, tn), lambda j, i, te, nt: (te[i], 0, j))],
            out_specs=pl.BlockSpec((tm, tn), lambda j, i, te, nt: (i, j)),
            scratch_shapes=[pltpu.VMEM((F, tn), BF16)]),
        out_shape=jax.ShapeDtypeStruct((P, D), F32),
        compiler_params=_cparams(2),
        name="moe_down",
    )(tile_expert, n_tiles, hs, wd)


def _combine_kernel(x_ref, y0_ref, y1_ref, wt_ref, gp_ref, xo_ref):
    y = wt_ref[:, 0:1] * y0_ref[...] + wt_ref[:, 1:2] * y1_ref[...]
    xo_ref[...] = x_ref[...] + _rms(y, gp_ref[...])


def moe_combine_residual(x, y0, y1, wts, g_post, tm=256):
    M, D = x.shape
    tm = _tile(tm, M)
    row = pl.BlockSpec((tm, D), lambda i: (i, 0))
    return pl.pallas_call(
        _combine_kernel, grid=(M // tm,),
        in_specs=[row, row, row, pl.BlockSpec((tm, LANES), lambda i: (i, 0)),
                  pl.BlockSpec((1, D), lambda i: (0, 0))],
        out_specs=row,
        out_shape=jax.ShapeDtypeStruct((M, D), F32),
        compiler_params=_cparams(1), name="moe_combine_residual",
    )(x, y0, y1, wts, g_post.reshape(1, D))


def _moe_plan(idx, n_tokens, tm):
    e_flat = idx.reshape(-1)
    onehot = (e_flat[:, None] == jnp.arange(N_EXPERTS)[None, :]).astype(jnp.int32)
    counts = jnp.sum(onehot, axis=0)
    rank = jnp.sum((jnp.cumsum(onehot, axis=0) - onehot) * onehot, axis=1)
    tiles_per = (counts + tm - 1) // tm
    tile_end = jnp.cumsum(tiles_per)
    tile_start = tile_end - tiles_per
    slot = tile_start[e_flat] * tm + rank
    n_slots_tiles = (n_tokens * TOP_K) // tm + N_EXPERTS
    n_tiles = tile_end[-1]
    tile_ids = jnp.arange(n_slots_tiles)
    te = jnp.searchsorted(tile_end, jnp.minimum(tile_ids, n_tiles - 1), side="right")
    te = jnp.minimum(te, N_EXPERTS - 1).astype(jnp.int32)
    token_of_pair = jnp.arange(n_tokens * TOP_K, dtype=jnp.int32) // TOP_K
    token_of_slot = jnp.zeros((n_slots_tiles * tm,), jnp.int32).at[slot].set(token_of_pair)
    return slot, token_of_slot, te, n_tiles.reshape(1).astype(jnp.int32)


def kernel(x, mem, mem_norm, mem_kv, l0_norm_mix_pre, l0_mix_in, l0_sgu_ln_g, l0_sgu_ln_b, l0_sgu_w, l0_sgu_b, l0_mix_out, l0_norm_mix_post, l0_norm_x_pre, l0_xq, l0_xo, l0_norm_x_post, l0_norm_ffn_pre, l0_ffn_gate, l0_ffn_up, l0_ffn_down, l0_norm_ffn_post, l1_norm_mix_pre, l1_mix_in, l1_gate_b, l1_conv, l1_head_norm, l1_mix_out, l1_norm_mix_post, l1_norm_x_pre, l1_xq, l1_xo, l1_norm_x_post, l1_norm_ffn_pre, l1_router, l1_moe_gate, l1_moe_up, l1_moe_down, l1_norm_ffn_post):
    B, S, D = x.shape
    T = B * S
    n_mem = mem.shape[1]
    xf = x.reshape(T, D)

    memn = rmsnorm_rows(mem.reshape(B * n_mem, D), mem_norm)
    kv = matmul_fullk(memn, mem_kv, mem_kv.shape[1], BF16).reshape(B, n_mem, 2 * D)

    hn = rmsnorm_rows(xf, l0_norm_mix_pre)
    z = matmul_fullk(hn, l0_mix_in, l0_mix_in.shape[1], BF16, act="gelu")
    y = sgu_mix(z, l0_sgu_ln_g, l0_sgu_ln_b, l0_sgu_w, l0_sgu_b)
    y = matmul_fullk(y, l0_mix_out, D, F32)
    xf, hn = residual_norm(xf, y, l0_norm_mix_post, l0_norm_x_pre)
    o = cross_attention(hn, l0_xq, kv, B)
    y = matmul_fullk(o, l0_xo, D, F32)
    xf, hn = residual_norm(xf, y, l0_norm_x_post, l0_norm_ffn_pre)
    hmid = swiglu_up(hn, l0_ffn_gate, l0_ffn_up)
    y = matmul_tiledk(hmid, l0_ffn_down, F32)
    xf, hn = residual_norm(xf, y, l0_norm_ffn_post, l1_norm_mix_pre)

    n_gates = 2 * MLSTM_HEADS
    n_main = l1_mix_in.shape[1] - n_gates
    proj = matmul_fullk(hn, l1_mix_in, n_main, BF16)
    gates_t = mlstm_gates(hn, l1_mix_in[:, n_main:], l1_gate_b)
    y = mlstm_core(proj, gates_t, l1_conv, l1_head_norm, B)
    y = matmul_fullk(y, l1_mix_out, D, F32)
    xf, hn = residual_norm(xf, y, l1_norm_mix_post, l1_norm_x_pre)
    o = cross_attention(hn, l1_xq, kv, B)
    y = matmul_fullk(o, l1_xo, D, F32)
    xf, hn = residual_norm(xf, y, l1_norm_x_post, l1_norm_ffn_pre)
    idx, wts = moe_router(xf, l1_norm_ffn_pre, l1_router)
    slot, token_of_slot, te, n_tiles = _moe_plan(idx[:, :TOP_K], T, MOE_TM)
    xs = jnp.take(hn, token_of_slot, axis=0)
    hs = moe_up(xs, l1_moe_gate, l1_moe_up, te, n_tiles)
    ys = moe_down(hs, l1_moe_down, te, n_tiles)
    yg = jnp.take(ys, slot, axis=0).reshape(T, TOP_K, D)
    xf = moe_combine_residual(xf, yg[:, 0], yg[:, 1], wts, l1_norm_ffn_post)
    return xf.reshape(B, S, D)
```

```python
import functools

import jax
import jax.numpy as jnp
from jax import lax
from jax.experimental import pallas as pl
from jax.experimental.pallas import tpu as pltpu

F32 = jnp.float32
BF16 = jnp.bfloat16

EPS = 1e-6
CHUNK = 128
SGU_GROUPS = 8
MLSTM_HEADS = 8
CONV_WIDTH = 4
GATE_CAP = 15.0
X_HEADS = 4
N_EXPERTS = 8
TOP_K = 2
LANES = 128
MOE_TM = 512
VMEM_LIMIT = 56 * 1024 * 1024


def _cparams(n_axes, vmem=VMEM_LIMIT):
    return pltpu.CompilerParams(
        dimension_semantics=("arbitrary",) * n_axes, vmem_limit_bytes=vmem)


def _tile(pref, dim):
    t = min(pref, dim)
    assert dim % t == 0, (pref, dim)
    return t


def _rms(x, g):
    return x * lax.rsqrt(jnp.mean(x * x, axis=-1, keepdims=True) + EPS) * g


def _sigmoid(x):
    return 1.0 / (1.0 + jnp.exp(-x))


def _rms_kernel(x_ref, g_ref, o_ref):
    o_ref[...] = _rms(x_ref[...], g_ref[...]).astype(o_ref.dtype)


def rmsnorm_rows(x, g, tm=256):
    M, D = x.shape
    tm = _tile(tm, M)
    return pl.pallas_call(
        _rms_kernel,
        grid=(M // tm,),
        in_specs=[pl.BlockSpec((tm, D), lambda i: (i, 0)),
                  pl.BlockSpec((1, D), lambda i: (0, 0))],
        out_specs=pl.BlockSpec((tm, D), lambda i: (i, 0)),
        out_shape=jax.ShapeDtypeStruct((M, D), BF16),
        compiler_params=_cparams(1),
        name="rmsnorm_rows",
    )(x, g.reshape(1, D))


def _gelu(x):
    return 0.5 * x * (1.0 + lax.erf(x * 0.7071067811865476))


def _mm_kernel(a_ref, w_ref, o_ref, wb_ref, *, act):
    @pl.when(pl.program_id(1) == 0)
    def _():
        wb_ref[...] = w_ref[...].astype(BF16)

    acc = jnp.dot(a_ref[...], wb_ref[...], preferred_element_type=F32)
    if act == "gelu":
        acc = _gelu(acc)
    o_ref[...] = acc.astype(o_ref.dtype)


def matmul_fullk(a, w, n_out, out_dtype, act=None, tm=1024, tn=512):
    M, K = a.shape
    tm = _tile(tm, M)
    tn = _tile(tn, n_out)
    return pl.pallas_call(
        functools.partial(_mm_kernel, act=act),
        grid=(n_out // tn, M // tm),
        in_specs=[pl.BlockSpec((tm, K), lambda j, i: (i, 0)),
                  pl.BlockSpec((K, tn), lambda j, i: (0, j))],
        out_specs=pl.BlockSpec((tm, tn), lambda j, i: (i, j)),
        out_shape=jax.ShapeDtypeStruct((M, n_out), out_dtype),
        scratch_shapes=[pltpu.VMEM((K, tn), BF16)],
        compiler_params=_cparams(2),
        name="matmul_fullk" + ("_" + act if act else ""),
    )(a, w)


def _swiglu_kernel(a_ref, wg_ref, wu_ref, o_ref, wgb_ref, wub_ref):
    @pl.when(pl.program_id(1) == 0)
    def _():
        wgb_ref[...] = wg_ref[...].astype(BF16)
        wub_ref[...] = wu_ref[...].astype(BF16)

    a = a_ref[...]
    g = jnp.dot(a, wgb_ref[...], preferred_element_type=F32)
    u = jnp.dot(a, wub_ref[...], preferred_element_type=F32)
    o_ref[...] = (g * _sigmoid(g) * u).astype(o_ref.dtype)


def swiglu_up(a, wg, wu, tm=1024, tn=256):
    M, K = a.shape
    F = wg.shape[1]
    tm = _tile(tm, M)
    tn = _tile(tn, F)
    wspec = pl.BlockSpec((K, tn), lambda j, i: (0, j))
    return pl.pallas_call(
        _swiglu_kernel,
        grid=(F // tn, M // tm),
        in_specs=[pl.BlockSpec((tm, K), lambda j, i: (i, 0)), wspec, wspec],
        out_specs=pl.BlockSpec((tm, tn), lambda j, i: (i, j)),
        out_shape=jax.ShapeDtypeStruct((M, F), BF16),
        scratch_shapes=[pltpu.VMEM((K, tn), BF16), pltpu.VMEM((K, tn), BF16)],
        compiler_params=_cparams(2),
        name="swiglu_up",
    )(a, wg, wu)


def _mmk_kernel(a_ref, w_ref, o_ref, acc_ref, *, nk):
    k = pl.program_id(2)

    @pl.when(k == 0)
    def _():
        acc_ref[...] = jnp.zeros_like(acc_ref)

    acc_ref[...] += jnp.dot(a_ref[...], w_ref[...].astype(BF16),
                            preferred_element_type=F32)

    @pl.when(k == nk - 1)
    def _():
        o_ref[...] = acc_ref[...].astype(o_ref.dtype)


def matmul_tiledk(a, w, out_dtype, tm=2048, tn=1024, tk=1024):
    M, K = a.shape
    N = w.shape[1]
    tm, tn, tk = _tile(tm, M), _tile(tn, N), _tile(tk, K)
    nk = K // tk
    return pl.pallas_call(
        functools.partial(_mmk_kernel, nk=nk),
        grid=(N // tn, M // tm, nk),
        in_specs=[pl.BlockSpec((tm, tk), lambda j, i, k: (i, k)),
                  pl.BlockSpec((tk, tn), lambda j, i, k: (k, j))],
        out_specs=pl.BlockSpec((tm, tn), lambda j, i, k: (i, j)),
        out_shape=jax.ShapeDtypeStruct((M, N), out_dtype),
        scratch_shapes=[pltpu.VMEM((tm, tn), F32)],
        compiler_params=_cparams(3),
        name="matmul_tiledk",
    )(a, w)


def _res_kernel(x_ref, y_ref, gp_ref, gn_ref, xo_ref, hn_ref):
    xn = x_ref[...] + _rms(y_ref[...].astype(F32), gp_ref[...])
    xo_ref[...] = xn
    hn_ref[...] = _rms(xn, gn_ref[...]).astype(hn_ref.dtype)


def _res_last_kernel(x_ref, y_ref, gp_ref, xo_ref):
    xo_ref[...] = x_ref[...] + _rms(y_ref[...].astype(F32), gp_ref[...])


def residual_norm(x, y, g_post, g_next, tm=256):
    M, D = x.shape
    tm = _tile(tm, M)
    row = pl.BlockSpec((tm, D), lambda i: (i, 0))
    vec = pl.BlockSpec((1, D), lambda i: (0, 0))
    if g_next is None:
        return pl.pallas_call(
            _res_last_kernel, grid=(M // tm,),
            in_specs=[row, row, vec], out_specs=row,
            out_shape=jax.ShapeDtypeStruct((M, D), F32),
            compiler_params=_cparams(1), name="residual_last",
        )(x, y, g_post.reshape(1, D))
    return pl.pallas_call(
        _res_kernel, grid=(M // tm,),
        in_specs=[row, row, vec, vec], out_specs=[row, row],
        out_shape=[jax.ShapeDtypeStruct((M, D), F32), jax.ShapeDtypeStruct((M, D), BF16)],
        compiler_params=_cparams(1), name="residual_norm",
    )(x, y, g_post.reshape(1, D), g_next.reshape(1, D))


def _sgu_kernel(u_ref, v_ref, lg_ref, lb_ref, w_ref, bt_ref, o_ref, wm_ref, *, groups):
    @pl.when(pl.program_id(0) == 0)
    def _():
        t = lax.broadcasted_iota(jnp.int32, (CHUNK, CHUNK), 0)
        s = lax.broadcasted_iota(jnp.int32, (CHUNK, CHUNK), 1)
        for g in range(groups):
            wm_ref[g] = jnp.where(t >= s, w_ref[g], 0.0).astype(BF16)

    v = v_ref[...].astype(F32)
    vc = v - jnp.mean(v, axis=-1, keepdims=True)
    vn = vc * lax.rsqrt(jnp.mean(vc * vc, axis=-1, keepdims=True) + EPS)
    vn = (vn * lg_ref[...] + lb_ref[...]).astype(BF16)
    tm, width = vn.shape
    gd = width // groups
    for c in range(tm // CHUNK):
        rows = slice(c * CHUNK, (c + 1) * CHUNK)
        for g in range(groups):
            cols = slice(g * gd, (g + 1) * gd)
            mixed = jnp.dot(wm_ref[g], vn[rows, cols], preferred_element_type=F32)
            mixed = mixed + bt_ref[:, g:g + 1]
            o_ref[rows, cols] = (u_ref[rows, cols].astype(F32) * mixed).astype(o_ref.dtype)


def sgu_mix(z, ln_g, ln_b, sgu_w, sgu_b, tm=256):
    M, W2 = z.shape
    W = W2 // 2
    G = sgu_w.shape[0]
    tm = _tile(tm, M)
    return pl.pallas_call(
        functools.partial(_sgu_kernel, groups=G),
        grid=(M // tm,),
        in_specs=[pl.BlockSpec((tm, W), lambda i: (i, 0)),
                  pl.BlockSpec((tm, W), lambda i: (i, 1)),
                  pl.BlockSpec((1, W), lambda i: (0, 0)),
                  pl.BlockSpec((1, W), lambda i: (0, 0)),
                  pl.BlockSpec((G, CHUNK, CHUNK), lambda i: (0, 0, 0)),
                  pl.BlockSpec((CHUNK, G), lambda i: (0, 0))],
        out_specs=pl.BlockSpec((tm, W), lambda i: (i, 0)),
        out_shape=jax.ShapeDtypeStruct((M, W), BF16),
        scratch_shapes=[pltpu.VMEM((G, CHUNK, CHUNK), BF16)],
        compiler_params=_cparams(1),
        name="sgu_mix",
    )(z, z, ln_g.reshape(1, W), ln_b.reshape(1, W), sgu_w, sgu_b.T)


def _xattn_kernel(hn_ref, wq_ref, k_ref, v_ref, o_ref, wqb_ref, *, scale):
    @pl.when((pl.program_id(1) == 0) & (pl.program_id(2) == 0))
    def _():
        wqb_ref[...] = wq_ref[...].astype(BF16)

    q = jnp.dot(hn_ref[...], wqb_ref[...], preferred_element_type=F32)
    s = lax.dot_general(q.astype(BF16), k_ref[0], (((1,), (1,)), ((), ())),
                        preferred_element_type=F32) * scale
    p = jnp.exp(s - jnp.max(s, axis=-1, keepdims=True))
    p = p / jnp.sum(p, axis=-1, keepdims=True)
    o = jnp.dot(p.astype(BF16), v_ref[0], preferred_element_type=F32)
    o_ref[...] = o.astype(o_ref.dtype)


def cross_attention(hn, wq, kv, batch, tm=512):
    M, D = hn.shape
    S = M // batch
    n_mem = kv.shape[1]
    hd = D // X_HEADS
    tm = _tile(tm, S)
    spt = S // tm
    return pl.pallas_call(
        functools.partial(_xattn_kernel, scale=hd ** -0.5),
        grid=(X_HEADS, batch, spt),
        in_specs=[pl.BlockSpec((tm, D), lambda h, b, m: (b * spt + m, 0)),
                  pl.BlockSpec((D, hd), lambda h, b, m: (0, h),
                               pipeline_mode=pl.Buffered(1)),
                  pl.BlockSpec((1, n_mem, hd), lambda h, b, m: (b, 0, h)),
                  pl.BlockSpec((1, n_mem, hd), lambda h, b, m: (b, 0, X_HEADS + h))],
        out_specs=pl.BlockSpec((tm, hd), lambda h, b, m: (b * spt + m, h)),
        out_shape=jax.ShapeDtypeStruct((M, D), BF16),
        scratch_shapes=[pltpu.VMEM((D, hd), BF16)],
        compiler_params=_cparams(3),
        name="cross_attention",
    )(hn, wq, kv, kv)


def _gates_kernel(hn_ref, w_ref, b_ref, o_ref, *, n_gates):
    acc = jnp.dot(hn_ref[...], w_ref[...].astype(BF16), preferred_element_type=F32)
    g = GATE_CAP * jnp.tanh((acc + b_ref[...]) / GATE_CAP)
    o_ref[...] = g.T[:n_gates, :]


def mlstm_gates(hn, w_gates, gate_b, tm=512):
    M, D = hn.shape
    n_gates = w_gates.shape[1]
    tm = _tile(tm, M)
    w_pad = jnp.pad(w_gates, ((0, 0), (0, LANES - n_gates)))
    b_pad = jnp.pad(gate_b, (0, LANES - n_gates)).reshape(1, LANES)
    return pl.pallas_call(
        functools.partial(_gates_kernel, n_gates=n_gates),
        grid=(M // tm,),
        in_specs=[pl.BlockSpec((tm, D), lambda i: (i, 0)),
                  pl.BlockSpec((D, LANES), lambda i: (0, 0)),
                  pl.BlockSpec((1, LANES), lambda i: (0, 0))],
        out_specs=pl.BlockSpec((n_gates, tm), lambda i: (0, i)),
        out_shape=jax.ShapeDtypeStruct((n_gates, M), F32),
        compiler_params=_cparams(1),
        name="mlstm_gates",
    )(hn, w_pad, b_pad)


def _conv_silu(x_ref, prev_ref, w_ref):
    x = x_ref[...].astype(F32)
    prev = prev_ref[...]
    rid = lax.broadcasted_iota(jnp.int32, x.shape, 0)
    acc = x * w_ref[CONV_WIDTH - 1:CONV_WIDTH, :]
    for r in range(1, CONV_WIDTH):
        shifted = jnp.where(rid < r, pltpu.roll(prev, r, 0), pltpu.roll(x, r, 0))
        acc = acc + shifted * w_ref[CONV_WIDTH - 1 - r:CONV_WIDTH - r, :]
    prev_ref[...] = x
    return acc * _sigmoid(acc)


def _mlstm_kernel(q_ref, k_ref, v_ref, o_ref, ig_ref, fg_ref, cq_ref, ck_ref, hg_ref,
                  y_ref, s_ref, n_ref, m_ref, qp_ref, kp_ref):
    @pl.when(pl.program_id(2) == 0)
    def _():
        s_ref[...] = jnp.zeros_like(s_ref)
        n_ref[...] = jnp.zeros_like(n_ref)
        m_ref[...] = jnp.zeros_like(m_ref)
        qp_ref[...] = jnp.zeros_like(qp_ref)
        kp_ref[...] = jnp.zeros_like(kp_ref)

    L = CHUNK
    dk = q_ref.shape[-1]
    q = _conv_silu(q_ref, qp_ref, cq_ref) * dk ** -0.5
    k = _conv_silu(k_ref, kp_ref, ck_ref)
    qb = q.astype(BF16)
    kb = k.astype(BF16)
    vb = v_ref[...]
    v = vb.astype(F32)

    t_id = lax.broadcasted_iota(jnp.int32, (L, L), 0)
    s_id = lax.broadcasted_iota(jnp.int32, (L, L), 1)
    eye = t_id == s_id
    causal = t_id >= s_id

    i_row = ig_ref[0]
    f_row = fg_ref[0]
    logf_row = -(jnp.maximum(-f_row, 0.0) + jnp.log1p(jnp.exp(-jnp.abs(f_row))))
    logf_b = jnp.broadcast_to(logf_row, (L, L))
    i_b = jnp.broadcast_to(i_row, (L, L))
    logf_col = jnp.sum(jnp.where(eye, logf_b, 0.0), axis=1, keepdims=True)
    i_col = jnp.sum(jnp.where(eye, i_b, 0.0), axis=1, keepdims=True)
    b_col = jnp.sum(jnp.where(causal, logf_b, 0.0), axis=1, keepdims=True)
    b_row = jnp.sum(jnp.where(t_id <= s_id, jnp.broadcast_to(logf_col, (L, L)), 0.0),
                    axis=0, keepdims=True)
    g_tot = jnp.sum(logf_row, axis=1, keepdims=True)
    m0 = m_ref[:, 0:1]

    d_log = jnp.where(causal, b_col - b_row + i_row, -jnp.inf)
    m_inter = b_col + m0
    m_t = jnp.maximum(m_inter, jnp.max(d_log, axis=1, keepdims=True))
    qk = lax.dot_general(qb, kb, (((1,), (1,)), ((), ())), preferred_element_type=F32)
    scores = qk * jnp.exp(d_log - m_t)
    inter = jnp.exp(m_inter - m_t)
    num = (jnp.dot(scores.astype(BF16), vb, preferred_element_type=F32)
           + inter * jnp.dot(qb, s_ref[...].astype(BF16), preferred_element_type=F32))
    den = (jnp.sum(scores, axis=1, keepdims=True)
           + inter * jnp.sum(q * n_ref[...], axis=1, keepdims=True))
    h = num / jnp.maximum(jnp.abs(den), jnp.exp(-m_t))
    hn = _rms(h, hg_ref[...])
    y_ref[...] = (_sigmoid(o_ref[...].astype(F32)) * hn).astype(y_ref.dtype)

    w_log = g_tot - b_col + i_col
    a = jnp.max(w_log, axis=0, keepdims=True)
    w = jnp.exp(w_log - a)
    s_loc = lax.dot_general(kb, (w * v).astype(BF16), (((0,), (0,)), ((), ())),
                            preferred_element_type=F32)
    n_loc = jnp.sum(w * k, axis=0, keepdims=True)
    m_new = jnp.maximum(g_tot + m0, a)
    sc_prev = jnp.exp(g_tot + m0 - m_new)
    sc_loc = jnp.exp(a - m_new)
    s_ref[...] = sc_prev * s_ref[...] + sc_loc * s_loc
    n_ref[...] = sc_prev * n_ref[...] + sc_loc * n_loc
    m_ref[...] = jnp.broadcast_to(m_new, m_ref.shape)


def mlstm_core(proj, gates_t, conv_w, head_g, batch):
    M = proj.shape[0]
    H = MLSTM_HEADS
    qkw = conv_w.shape[1]
    dk = qkw // (2 * H)
    dv = head_g.shape[0] // H
    nc = M // batch // CHUNK
    assert qkw % dv == 0
    v0 = qkw // dv
    gates3 = gates_t.reshape(2 * H, 1, M)
    row = lambda b, h, c: b * nc + c
    return pl.pallas_call(
        _mlstm_kernel,
        grid=(batch, H, nc),
        in_specs=[pl.BlockSpec((CHUNK, dk), lambda b, h, c: (row(b, h, c), h)),
                  pl.BlockSpec((CHUNK, dk), lambda b, h, c: (row(b, h, c), H + h)),
                  pl.BlockSpec((CHUNK, dv), lambda b, h, c: (row(b, h, c), v0 + h)),
                  pl.BlockSpec((CHUNK, dv), lambda b, h, c: (row(b, h, c), v0 + H + h)),
                  pl.BlockSpec((1, 1, CHUNK), lambda b, h, c: (h, 0, row(b, h, c))),
                  pl.BlockSpec((1, 1, CHUNK), lambda b, h, c: (H + h, 0, row(b, h, c))),
                  pl.BlockSpec((CONV_WIDTH, dk), lambda b, h, c: (0, h)),
                  pl.BlockSpec((CONV_WIDTH, dk), lambda b, h, c: (0, H + h)),
                  pl.BlockSpec((1, dv), lambda b, h, c: (0, h))],
        out_specs=pl.BlockSpec((CHUNK, dv), lambda b, h, c: (row(b, h, c), h)),
        out_shape=jax.ShapeDtypeStruct((M, H * dv), BF16),
        scratch_shapes=[pltpu.VMEM((dk, dv), F32), pltpu.VMEM((1, dk), F32),
                        pltpu.VMEM((1, LANES), F32),
                        pltpu.VMEM((CHUNK, dk), F32), pltpu.VMEM((CHUNK, dk), F32)],
        compiler_params=_cparams(3),
        name="mlstm_core",
    )(proj, proj, proj, proj, gates3, gates3, conv_w, conv_w, head_g.reshape(1, H * dv))


def _router_kernel(x_ref, g_ref, w_ref, idx_ref, wt_ref):
    hn = _rms(x_ref[...], g_ref[...])
    logits = jnp.dot(hn, w_ref[...], preferred_element_type=F32,
                     precision=lax.Precision.HIGHEST)
    lane = lax.broadcasted_iota(jnp.int32, logits.shape, 1)
    neg = -jnp.inf
    l1 = jnp.where(lane < N_EXPERTS, logits, neg)
    m1 = jnp.max(l1, axis=1, keepdims=True)
    i1 = jnp.min(jnp.where(l1 == m1, lane, LANES), axis=1, keepdims=True)
    l2 = jnp.where(lane == i1, neg, l1)
    m2 = jnp.max(l2, axis=1, keepdims=True)
    i2 = jnp.min(jnp.where(l2 == m2, lane, LANES), axis=1, keepdims=True)
    r = jnp.exp(m2 - m1)
    w1 = 1.0 / (1.0 + r)
    w2 = r / (1.0 + r)
    idx_ref[...] = jnp.where(lane == 0, i1, jnp.where(lane == 1, i2, 0))
    wt_ref[...] = jnp.where(lane == 0, w1, jnp.where(lane == 1, w2, 0.0))


def moe_router(x, g, w_router, tm=256):
    M, D = x.shape
    tm = _tile(tm, M)
    w_pad = jnp.pad(w_router, ((0, 0), (0, LANES - w_router.shape[1])))
    row = pl.BlockSpec((tm, D), lambda i: (i, 0))
    out = pl.BlockSpec((tm, LANES), lambda i: (i, 0))
    return pl.pallas_call(
        _router_kernel, grid=(M // tm,),
        in_specs=[row, pl.BlockSpec((1, D), lambda i: (0, 0)),
                  pl.BlockSpec((D, LANES), lambda i: (0, 0))],
        out_specs=[out, out],
        out_shape=[jax.ShapeDtypeStruct((M, LANES), jnp.int32),
                   jax.ShapeDtypeStruct((M, LANES), F32)],
        compiler_params=_cparams(1), name="moe_router",
    )(x, g.reshape(1, D), w_pad)


def _row_copy(src_hbm, src_row, dst_ref, dst_row, sem):
    return pltpu.make_async_copy(src_hbm.at[pl.ds(src_row, 1)],
                                 dst_ref.at[pl.ds(dst_row, 1)], sem)


def _dispatch_kernel(tok_ref, nt_ref, x_hbm, g_ref, o_ref, buf_ref, sem_ref, *, tm):
    i = pl.program_id(0)
    nt = nt_ref[0]

    def start_tile(tile, slot):
        def body(r, carry):
            _row_copy(x_hbm, tok_ref[tile * tm + r], buf_ref.at[slot], r,
                      sem_ref.at[slot]).start()
            return carry
        lax.fori_loop(0, tm, body, 0)

    def wait_tile(slot):
        def body(r, carry):
            _row_copy(x_hbm, 0, buf_ref.at[slot], r, sem_ref.at[slot]).wait()
            return carry
        lax.fori_loop(0, tm, body, 0)

    @pl.when(i == 0)
    def _():
        start_tile(0, 0)

    @pl.when(i + 1 < nt)
    def _():
        start_tile(i + 1, (i + 1) % 2)

    @pl.when(i < nt)
    def _():
        slot = i % 2
        wait_tile(slot)
        o_ref[...] = _rms(buf_ref[slot], g_ref[...]).astype(o_ref.dtype)

    @pl.when(i >= nt)
    def _():
        o_ref[...] = jnp.zeros_like(o_ref)


def moe_dispatch(x, g, token_of_slot, n_tiles, tm):
    T, D = x.shape
    P = token_of_slot.shape[0]
    return pl.pallas_call(
        functools.partial(_dispatch_kernel, tm=tm),
        grid_spec=pltpu.PrefetchScalarGridSpec(
            num_scalar_prefetch=2,
            grid=(P // tm,),
            in_specs=[pl.BlockSpec(memory_space=pl.ANY),
                      pl.BlockSpec((1, D), lambda i, tok, nt: (0, 0))],
            out_specs=pl.BlockSpec((tm, D), lambda i, tok, nt: (i, 0)),
            scratch_shapes=[pltpu.VMEM((2, tm, D), F32), pltpu.SemaphoreType.DMA((2,))]),
        out_shape=jax.ShapeDtypeStruct((P, D), BF16),
        compiler_params=_cparams(1),
        name="moe_dispatch",
    )(token_of_slot, n_tiles, x, g.reshape(1, D))


def _weights_changed(te_ref, i):
    return (i == 0) | (te_ref[i] != te_ref[jnp.maximum(i - 1, 0)])


def _stage_copies(w_hbms, e, j, tn, stage_refs, sem_ref):
    col = pl.multiple_of(j * tn, tn)
    return [pltpu.make_async_copy(w.at[e, :, pl.ds(col, tn)], s, sem_ref.at[n])
            for n, (w, s) in enumerate(zip(w_hbms, stage_refs))]


def _restage_weights(te_ref, nx_ref, w_hbms, stage_refs, wb_refs, sem_ref, *, tn, nj):
    j = pl.program_id(0)
    i = pl.program_id(1)

    @pl.when((j == 0) & (i == 0))
    def _():
        for c in _stage_copies(w_hbms, te_ref[0], 0, tn, stage_refs, sem_ref):
            c.start()

    @pl.when(_weights_changed(te_ref, i))
    def _():
        for c in _stage_copies(w_hbms, te_ref[i], j, tn, stage_refs, sem_ref):
            c.wait()
        for s, wb in zip(stage_refs, wb_refs):
            wb[...] = s[...].astype(BF16)
        nxt = nx_ref[i]
        last = nxt < 0
        e_next = jnp.where(last, te_ref[0], nxt)
        j_next = jnp.where(last, j + 1, j)

        @pl.when(j_next < nj)
        def _():
            for c in _stage_copies(w_hbms, e_next, j_next, tn, stage_refs, sem_ref):
                c.start()


def _moe_up_kernel(te_ref, nt_ref, nx_ref, a_ref, wg_hbm, wu_hbm, o_ref,
                   sg_ref, su_ref, wgb_ref, wub_ref, sem_ref, *, tn, nj):
    _restage_weights(te_ref, nx_ref, (wg_hbm, wu_hbm), (sg_ref, su_ref),
                     (wgb_ref, wub_ref), sem_ref, tn=tn, nj=nj)

    @pl.when(pl.program_id(1) < nt_ref[0])
    def _():
        a = a_ref[...]
        g = jnp.dot(a, wgb_ref[...], preferred_element_type=F32)
        u = jnp.dot(a, wub_ref[...], preferred_element_type=F32)
        o_ref[...] = (g * _sigmoid(g) * u).astype(o_ref.dtype)

    @pl.when(pl.program_id(1) >= nt_ref[0])
    def _():
        o_ref[...] = jnp.zeros_like(o_ref)


def _moe_down_kernel(te_ref, nt_ref, nx_ref, a_ref, w_hbm, o_ref,
                     s_ref, wb_ref, sem_ref, *, tn, nj):
    _restage_weights(te_ref, nx_ref, (w_hbm,), (s_ref,), (wb_ref,), sem_ref, tn=tn, nj=nj)

    @pl.when(pl.program_id(1) < nt_ref[0])
    def _():
        o_ref[...] = jnp.dot(a_ref[...], wb_ref[...],
                             preferred_element_type=F32).astype(o_ref.dtype)

    @pl.when(pl.program_id(1) >= nt_ref[0])
    def _():
        o_ref[...] = jnp.zeros_like(o_ref)


def _moe_grouped(kernel_fn, name, a, weights, plan, tm, tn, out_dtype):
    te, n_tiles, nxt = plan
    P, K = a.shape
    N = weights[0].shape[2]
    tn = _tile(tn, N)
    nj = N // tn
    used = lambda i, nt: jnp.minimum(i, nt[0] - 1)
    n_w = len(weights)
    return pl.pallas_call(
        functools.partial(kernel_fn, tn=tn, nj=nj),
        grid_spec=pltpu.PrefetchScalarGridSpec(
            num_scalar_prefetch=3,
            grid=(nj, P // tm),
            in_specs=[pl.BlockSpec((tm, K), lambda j, i, te, nt, nx: (used(i, nt), 0))]
                     + [pl.BlockSpec(memory_space=pl.ANY)] * n_w,
            out_specs=pl.BlockSpec((tm, tn), lambda j, i, te, nt, nx: (i, j)),
            scratch_shapes=[pltpu.VMEM((K, tn), F32)] * n_w + [pltpu.VMEM((K, tn), BF16)] * n_w
                           + [pltpu.SemaphoreType.DMA((n_w,))]),
        out_shape=jax.ShapeDtypeStruct((P, N), out_dtype),
        compiler_params=_cparams(2),
        name=name,
    )(te, n_tiles, nxt, a, *weights)


def moe_up(xs, wg, wu, plan, tm, tn=512):
    return _moe_grouped(_moe_up_kernel, "moe_up", xs, (wg, wu), plan, tm, tn, BF16)


def moe_down(hs, wd, plan, tm, tn=1024):
    return _moe_grouped(_moe_down_kernel, "moe_down", hs, (wd,), plan, tm, tn, F32)


def _combine_kernel(slot_ref, x_ref, wt_ref, gp_ref, ys_hbm, xo_ref, buf_ref, sem_ref, *, tm):
    i = pl.program_id(0)

    def start_tile(tile, s):
        def body(r, carry):
            for k in range(TOP_K):
                _row_copy(ys_hbm, slot_ref[(tile * tm + r) * TOP_K + k], buf_ref.at[s, k], r,
                          sem_ref.at[s]).start()
            return carry
        lax.fori_loop(0, tm, body, 0)

    def wait_tile(s):
        def body(r, carry):
            for k in range(TOP_K):
                _row_copy(ys_hbm, 0, buf_ref.at[s, k], r, sem_ref.at[s]).wait()
            return carry
        lax.fori_loop(0, tm, body, 0)

    @pl.when(i == 0)
    def _():
        start_tile(0, 0)

    @pl.when(i + 1 < pl.num_programs(0))
    def _():
        start_tile(i + 1, (i + 1) % 2)

    s = i % 2
    wait_tile(s)
    y = wt_ref[:, 0:1] * buf_ref[s, 0] + wt_ref[:, 1:2] * buf_ref[s, 1]
    xo_ref[...] = x_ref[...] + _rms(y, gp_ref[...])


def moe_combine_residual(x, ys, slot, wts, g_post, tm=256):
    M, D = x.shape
    tm = _tile(tm, M)
    row = pl.BlockSpec((tm, D), lambda i, sl: (i, 0))
    return pl.pallas_call(
        functools.partial(_combine_kernel, tm=tm),
        grid_spec=pltpu.PrefetchScalarGridSpec(
            num_scalar_prefetch=1,
            grid=(M // tm,),
            in_specs=[row, pl.BlockSpec((tm, LANES), lambda i, sl: (i, 0)),
                      pl.BlockSpec((1, D), lambda i, sl: (0, 0)),
                      pl.BlockSpec(memory_space=pl.ANY)],
            out_specs=row,
            scratch_shapes=[pltpu.VMEM((2, TOP_K, tm, D), F32), pltpu.SemaphoreType.DMA((2,))]),
        out_shape=jax.ShapeDtypeStruct((M, D), F32),
        compiler_params=_cparams(1), name="moe_combine_residual",
    )(slot, x, wts, g_post.reshape(1, D), ys)


def _moe_plan(idx, n_tokens, tm):
    e_flat = idx.reshape(-1)
    onehot = (e_flat[:, None] == jnp.arange(N_EXPERTS)[None, :]).astype(jnp.int32)
    counts = jnp.sum(onehot, axis=0)
    rank = jnp.sum((jnp.cumsum(onehot, axis=0) - onehot) * onehot, axis=1)
    tiles_per = (counts + tm - 1) // tm
    tile_end = jnp.cumsum(tiles_per)
    tile_start = tile_end - tiles_per
    slot = (tile_start[e_flat] * tm + rank).astype(jnp.int32)
    n_tiles_max = (n_tokens * TOP_K) // tm + N_EXPERTS
    n_tiles = tile_end[-1]
    tile_ids = jnp.minimum(jnp.arange(n_tiles_max), n_tiles - 1)
    expert_of = lambda t: jnp.minimum(
        jnp.sum((tile_end[None, :] <= t[:, None]).astype(jnp.int32), axis=1), N_EXPERTS - 1)
    te = expert_of(tile_ids)
    group_end = tile_end[te]
    nxt = jnp.where(group_end < n_tiles, expert_of(jnp.minimum(group_end, n_tiles - 1)), -1)
    token_of_pair = jnp.arange(n_tokens * TOP_K, dtype=jnp.int32) // TOP_K
    token_of_slot = jnp.zeros((n_tiles_max * tm,), jnp.int32).at[slot].set(token_of_pair)
    plan = (te.astype(jnp.int32), n_tiles.reshape(1).astype(jnp.int32), nxt.astype(jnp.int32))
    return slot, token_of_slot, plan


def kernel(x, mem, mem_norm, mem_kv, l0_norm_mix_pre, l0_mix_in, l0_sgu_ln_g, l0_sgu_ln_b, l0_sgu_w, l0_sgu_b, l0_mix_out, l0_norm_mix_post, l0_norm_x_pre, l0_xq, l0_xo, l0_norm_x_post, l0_norm_ffn_pre, l0_ffn_gate, l0_ffn_up, l0_ffn_down, l0_norm_ffn_post, l1_norm_mix_pre, l1_mix_in, l1_gate_b, l1_conv, l1_head_norm, l1_mix_out, l1_norm_mix_post, l1_norm_x_pre, l1_xq, l1_xo, l1_norm_x_post, l1_norm_ffn_pre, l1_router, l1_moe_gate, l1_moe_up, l1_moe_down, l1_norm_ffn_post):
    B, S, D = x.shape
    T = B * S
    n_mem = mem.shape[1]
    xf = x.reshape(T, D)

    memn = rmsnorm_rows(mem.reshape(B * n_mem, D), mem_norm)
    kv = matmul_fullk(memn, mem_kv, mem_kv.shape[1], BF16).reshape(B, n_mem, 2 * D)

    hn = rmsnorm_rows(xf, l0_norm_mix_pre)
    z = matmul_fullk(hn, l0_mix_in, l0_mix_in.shape[1], BF16, act="gelu")
    y = sgu_mix(z, l0_sgu_ln_g, l0_sgu_ln_b, l0_sgu_w, l0_sgu_b)
    y = matmul_fullk(y, l0_mix_out, D, F32)
    xf, hn = residual_norm(xf, y, l0_norm_mix_post, l0_norm_x_pre)
    o = cross_attention(hn, l0_xq, kv, B)
    y = matmul_fullk(o, l0_xo, D, F32)
    xf, hn = residual_norm(xf, y, l0_norm_x_post, l0_norm_ffn_pre)
    hmid = swiglu_up(hn, l0_ffn_gate, l0_ffn_up)
    y = matmul_tiledk(hmid, l0_ffn_down, F32)
    xf, hn = residual_norm(xf, y, l0_norm_ffn_post, l1_norm_mix_pre)

    n_gates = 2 * MLSTM_HEADS
    n_main = l1_mix_in.shape[1] - n_gates
    proj = matmul_fullk(hn, l1_mix_in, n_main, BF16)
    gates_t = mlstm_gates(hn, l1_mix_in[:, n_main:], l1_gate_b)
    y = mlstm_core(proj, gates_t, l1_conv, l1_head_norm, B)
    y = matmul_fullk(y, l1_mix_out, D, F32)
    xf, hn = residual_norm(xf, y, l1_norm_mix_post, l1_norm_x_pre)
    o = cross_attention(hn, l1_xq, kv, B)
    y = matmul_fullk(o, l1_xo, D, F32)
    xf = residual_norm(xf, y, l1_norm_x_post, None)
    idx, wts = moe_router(xf, l1_norm_ffn_pre, l1_router)
    tm = min(MOE_TM, T)
    slot, token_of_slot, plan = _moe_plan(idx[:, :TOP_K], T, tm)
    xs = moe_dispatch(xf, l1_norm_ffn_pre, token_of_slot, plan[1], tm)
    hs = moe_up(xs, l1_moe_gate, l1_moe_up, plan, tm)
    ys = moe_down(hs, l1_moe_down, plan, tm)
    xf = moe_combine_residual(xf, ys, slot, wts, l1_norm_ffn_post)
    return xf.reshape(B, S, D)
```

```python
import functools

import jax
import jax.numpy as jnp
from jax import lax
from jax.experimental import pallas as pl
from jax.experimental.pallas import tpu as pltpu

F32 = jnp.float32
BF16 = jnp.bfloat16

EPS = 1e-6
CHUNK = 128
SGU_GROUPS = 8
MLSTM_HEADS = 8
CONV_WIDTH = 4
GATE_CAP = 15.0
X_HEADS = 4
N_EXPERTS = 8
TOP_K = 2
LANES = 128
MOE_TM = 512
GATHER_ROWS = 16
GATHER_UNROLL = 2
COMBINE_ROWS = 8
COMBINE_UNROLL = 4
MLSTM_HEADS_PER_STEP = 2
CONV_TAIL = 8
VMEM_LIMIT = 56 * 1024 * 1024


def _cparams(n_axes, vmem=VMEM_LIMIT):
    return pltpu.CompilerParams(
        dimension_semantics=("arbitrary",) * n_axes, vmem_limit_bytes=vmem)


def _tile(pref, dim):
    t = min(pref, dim)
    assert dim % t == 0, (pref, dim)
    return t


def _rms(x, g):
    return x * lax.rsqrt(jnp.mean(x * x, axis=-1, keepdims=True) + EPS) * g


def _sigmoid(x):
    return 1.0 / (1.0 + jnp.exp(-x))


def _rms_kernel(x_ref, g_ref, o_ref):
    o_ref[...] = _rms(x_ref[...], g_ref[...]).astype(o_ref.dtype)


def rmsnorm_rows(x, g, tm=256):
    M, D = x.shape
    tm = _tile(tm, M)
    return pl.pallas_call(
        _rms_kernel,
        grid=(M // tm,),
        in_specs=[pl.BlockSpec((tm, D), lambda i: (i, 0)),
                  pl.BlockSpec((1, D), lambda i: (0, 0))],
        out_specs=pl.BlockSpec((tm, D), lambda i: (i, 0)),
        out_shape=jax.ShapeDtypeStruct((M, D), BF16),
        compiler_params=_cparams(1),
        name="rmsnorm_rows",
    )(x, g.reshape(1, D))


def _gelu(x):
    return 0.5 * x * (1.0 + lax.erf(x * 0.7071067811865476))


def _stream_weights(w_hbms, stage_refs, wb_refs, sem_ref, *, tn, nj, transposed, first=0):
    j = pl.program_id(0)
    i = pl.program_id(1)

    def copies(jj):
        off = pl.multiple_of(first + jj * tn, tn)
        return [pltpu.make_async_copy(
                    w.at[pl.ds(off, tn), :] if transposed else w.at[:, pl.ds(off, tn)],
                    s, sem_ref.at[n])
                for n, (w, s) in enumerate(zip(w_hbms, stage_refs))]

    @pl.when((j == 0) & (i == 0))
    def _():
        for c in copies(0):
            c.start()

    @pl.when(i == 0)
    def _():
        for c in copies(j):
            c.wait()
        for s, wb in zip(stage_refs, wb_refs):
            wb[...] = (s[...].T if transposed else s[...]).astype(BF16)

        @pl.when(j + 1 < nj)
        def _():
            for c in copies(j + 1):
                c.start()


def _mm_kernel(a_ref, w_hbm, o_ref, stage_ref, wb_ref, sem_ref, *, act, tn, nj, transposed,
               first):
    _stream_weights((w_hbm,), (stage_ref,), (wb_ref,), sem_ref,
                    tn=tn, nj=nj, transposed=transposed, first=first)
    acc = jnp.dot(a_ref[...], wb_ref[...], preferred_element_type=F32)
    if act == "gelu":
        acc = _gelu(acc)
    o_ref[...] = acc.astype(o_ref.dtype)


def matmul_fullk(a, w, out_dtype, act=None, w_rows=None, tm=1024, tn=1024):
    M, K = a.shape
    transposed = w_rows is not None
    first, N = w_rows if transposed else (0, w.shape[1])
    tm = _tile(tm, M)
    tn = _tile(tn, N)
    assert first % tn == 0
    nj = N // tn
    stage_block = (tn, K) if transposed else (K, tn)
    return pl.pallas_call(
        functools.partial(_mm_kernel, act=act, tn=tn, nj=nj, transposed=transposed,
                          first=first),
        grid=(nj, M // tm),
        in_specs=[pl.BlockSpec((tm, K), lambda j, i: (i, 0)),
                  pl.BlockSpec(memory_space=pl.ANY)],
        out_specs=pl.BlockSpec((tm, tn), lambda j, i: (i, j)),
        out_shape=jax.ShapeDtypeStruct((M, N), out_dtype),
        scratch_shapes=[pltpu.VMEM(stage_block, F32), pltpu.VMEM((K, tn), BF16),
                        pltpu.SemaphoreType.DMA((1,))],
        compiler_params=_cparams(2),
        name="matmul_fullk" + ("_" + act if act else "") + ("_t" if transposed else ""),
    )(a, w)


def _swiglu_kernel(a_ref, wg_hbm, wu_hbm, o_ref, sg_ref, su_ref, wgb_ref, wub_ref, sem_ref,
                   *, tn, nj):
    _stream_weights((wg_hbm, wu_hbm), (sg_ref, su_ref), (wgb_ref, wub_ref), sem_ref,
                    tn=tn, nj=nj, transposed=False)
    a = a_ref[...]
    g = jnp.dot(a, wgb_ref[...], preferred_element_type=F32)
    u = jnp.dot(a, wub_ref[...], preferred_element_type=F32)
    o_ref[...] = (g * _sigmoid(g) * u).astype(o_ref.dtype)


def swiglu_up(a, wg, wu, tm=512, tn=512):
    M, K = a.shape
    F = wg.shape[1]
    tm = _tile(tm, M)
    tn = _tile(tn, F)
    nj = F // tn
    hbm = pl.BlockSpec(memory_space=pl.ANY)
    return pl.pallas_call(
        functools.partial(_swiglu_kernel, tn=tn, nj=nj),
        grid=(nj, M // tm),
        in_specs=[pl.BlockSpec((tm, K), lambda j, i: (i, 0)), hbm, hbm],
        out_specs=pl.BlockSpec((tm, tn), lambda j, i: (i, j)),
        out_shape=jax.ShapeDtypeStruct((M, F), BF16),
        scratch_shapes=[pltpu.VMEM((K, tn), F32), pltpu.VMEM((K, tn), F32),
                        pltpu.VMEM((K, tn), BF16), pltpu.VMEM((K, tn), BF16),
                        pltpu.SemaphoreType.DMA((2,))],
        compiler_params=_cparams(2),
        name="swiglu_up",
    )(a, wg, wu)


def _mmk_kernel(a_ref, w_ref, o_ref, acc_ref, *, nk):
    k = pl.program_id(2)

    @pl.when(k == 0)
    def _():
        acc_ref[...] = jnp.zeros_like(acc_ref)

    acc_ref[...] += jnp.dot(a_ref[...], w_ref[...].astype(BF16),
                            preferred_element_type=F32)

    @pl.when(k == nk - 1)
    def _():
        o_ref[...] = acc_ref[...].astype(o_ref.dtype)


def matmul_tiledk(a, w, out_dtype, tm=2048, tn=1024, tk=1024):
    M, K = a.shape
    N = w.shape[1]
    tm, tn, tk = _tile(tm, M), _tile(tn, N), _tile(tk, K)
    nk = K // tk
    return pl.pallas_call(
        functools.partial(_mmk_kernel, nk=nk),
        grid=(N // tn, M // tm, nk),
        in_specs=[pl.BlockSpec((tm, tk), lambda j, i, k: (i, k)),
                  pl.BlockSpec((tk, tn), lambda j, i, k: (k, j))],
        out_specs=pl.BlockSpec((tm, tn), lambda j, i, k: (i, j)),
        out_shape=jax.ShapeDtypeStruct((M, N), out_dtype),
        scratch_shapes=[pltpu.VMEM((tm, tn), F32)],
        compiler_params=_cparams(3),
        name="matmul_tiledk",
    )(a, w)


def _res_kernel(x_ref, y_ref, gp_ref, gn_ref, xo_ref, hn_ref):
    xn = x_ref[...] + _rms(y_ref[...].astype(F32), gp_ref[...])
    xo_ref[...] = xn
    hn_ref[...] = _rms(xn, gn_ref[...]).astype(hn_ref.dtype)


def _res_last_kernel(x_ref, y_ref, gp_ref, xo_ref):
    xo_ref[...] = x_ref[...] + _rms(y_ref[...].astype(F32), gp_ref[...])


def residual_norm(x, y, g_post, g_next, tm=256):
    M, D = x.shape
    tm = _tile(tm, M)
    row = pl.BlockSpec((tm, D), lambda i: (i, 0))
    vec = pl.BlockSpec((1, D), lambda i: (0, 0))
    if g_next is None:
        return pl.pallas_call(
            _res_last_kernel, grid=(M // tm,),
            in_specs=[row, row, vec], out_specs=row,
            out_shape=jax.ShapeDtypeStruct((M, D), F32),
            compiler_params=_cparams(1), name="residual_last",
        )(x, y, g_post.reshape(1, D))
    return pl.pallas_call(
        _res_kernel, grid=(M // tm,),
        in_specs=[row, row, vec, vec], out_specs=[row, row],
        out_shape=[jax.ShapeDtypeStruct((M, D), F32), jax.ShapeDtypeStruct((M, D), BF16)],
        compiler_params=_cparams(1), name="residual_norm",
    )(x, y, g_post.reshape(1, D), g_next.reshape(1, D))


def _sgu_kernel(u_ref, v_ref, lg_ref, lb_ref, w_ref, bt_ref, o_ref, wm_ref, *, groups):
    @pl.when(pl.program_id(0) == 0)
    def _():
        t = lax.broadcasted_iota(jnp.int32, (CHUNK, CHUNK), 0)
        s = lax.broadcasted_iota(jnp.int32, (CHUNK, CHUNK), 1)
        for g in range(groups):
            wm_ref[g] = jnp.where(t >= s, w_ref[g], 0.0).astype(BF16)

    v = v_ref[...].astype(F32)
    vc = v - jnp.mean(v, axis=-1, keepdims=True)
    vn = vc * lax.rsqrt(jnp.mean(vc * vc, axis=-1, keepdims=True) + EPS)
    vn = (vn * lg_ref[...] + lb_ref[...]).astype(BF16)
    tm, width = vn.shape
    gd = width // groups
    for c in range(tm // CHUNK):
        rows = slice(c * CHUNK, (c + 1) * CHUNK)
        for g in range(groups):
            cols = slice(g * gd, (g + 1) * gd)
            mixed = jnp.dot(wm_ref[g], vn[rows, cols], preferred_element_type=F32)
            mixed = mixed + bt_ref[:, g:g + 1]
            o_ref[rows, cols] = (u_ref[rows, cols].astype(F32) * mixed).astype(o_ref.dtype)


def sgu_mix(z, ln_g, ln_b, sgu_w, sgu_b, tm=256):
    M, W2 = z.shape
    W = W2 // 2
    G = sgu_w.shape[0]
    tm = _tile(tm, M)
    return pl.pallas_call(
        functools.partial(_sgu_kernel, groups=G),
        grid=(M // tm,),
        in_specs=[pl.BlockSpec((tm, W), lambda i: (i, 0)),
                  pl.BlockSpec((tm, W), lambda i: (i, 1)),
                  pl.BlockSpec((1, W), lambda i: (0, 0)),
                  pl.BlockSpec((1, W), lambda i: (0, 0)),
                  pl.BlockSpec((G, CHUNK, CHUNK), lambda i: (0, 0, 0)),
                  pl.BlockSpec((CHUNK, G), lambda i: (0, 0))],
        out_specs=pl.BlockSpec((tm, W), lambda i: (i, 0)),
        out_shape=jax.ShapeDtypeStruct((M, W), BF16),
        scratch_shapes=[pltpu.VMEM((G, CHUNK, CHUNK), BF16)],
        compiler_params=_cparams(1),
        name="sgu_mix",
    )(z, z, ln_g.reshape(1, W), ln_b.reshape(1, W), sgu_w, sgu_b.T)


def _xattn_kernel(q_ref, k_ref, v_ref, o_ref, *, heads, scale):
    hd = q_ref.shape[-1] // heads
    for h in range(heads):
        cols = slice(h * hd, (h + 1) * hd)
        s = lax.dot_general(q_ref[:, cols], k_ref[0, :, cols], (((1,), (1,)), ((), ())),
                            preferred_element_type=F32) * scale
        p = jnp.exp(s - jnp.max(s, axis=-1, keepdims=True))
        p = p / jnp.sum(p, axis=-1, keepdims=True)
        o = jnp.dot(p.astype(BF16), v_ref[0, :, cols], preferred_element_type=F32)
        o_ref[:, cols] = o.astype(o_ref.dtype)


def cross_attention(q, kv, batch, tm=512):
    M, D = q.shape
    S = M // batch
    n_mem = kv.shape[1]
    tm = _tile(tm, S)
    spt = S // tm
    return pl.pallas_call(
        functools.partial(_xattn_kernel, heads=X_HEADS, scale=(D // X_HEADS) ** -0.5),
        grid=(batch, spt),
        in_specs=[pl.BlockSpec((tm, D), lambda b, m: (b * spt + m, 0)),
                  pl.BlockSpec((1, n_mem, D), lambda b, m: (b, 0, 0)),
                  pl.BlockSpec((1, n_mem, D), lambda b, m: (b, 0, 1))],
        out_specs=pl.BlockSpec((tm, D), lambda b, m: (b * spt + m, 0)),
        out_shape=jax.ShapeDtypeStruct((M, D), BF16),
        compiler_params=_cparams(2),
        name="cross_attention",
    )(q, kv, kv)


def _gates_kernel(hn_ref, w_ref, b_ref, o_ref):
    acc = lax.dot_general(w_ref[...].astype(BF16), hn_ref[...], (((1,), (1,)), ((), ())),
                          preferred_element_type=F32)
    o_ref[...] = GATE_CAP * jnp.tanh((acc + b_ref[...]) / GATE_CAP)


def mlstm_gates(hn, w_gates_t, gate_b, tm=512):
    M, D = hn.shape
    n_gates = w_gates_t.shape[0]
    tm = _tile(tm, M)
    return pl.pallas_call(
        _gates_kernel,
        grid=(M // tm,),
        in_specs=[pl.BlockSpec((tm, D), lambda i: (i, 0)),
                  pl.BlockSpec((n_gates, D), lambda i: (0, 0)),
                  pl.BlockSpec((n_gates, 1), lambda i: (0, 0))],
        out_specs=pl.BlockSpec((n_gates, tm), lambda i: (0, i)),
        out_shape=jax.ShapeDtypeStruct((n_gates, M), F32),
        compiler_params=_cparams(1),
        name="mlstm_gates",
    )(hn, w_gates_t, gate_b.reshape(n_gates, 1))


def _conv_silu(x, tail_ref, w):
    prev = tail_ref[...]
    rid = lax.broadcasted_iota(jnp.int32, prev.shape, 0)
    acc = x * w[CONV_WIDTH - 1:CONV_WIDTH, :]
    for r in range(1, CONV_WIDTH):
        rolled = pltpu.roll(x, r, 0)
        head = jnp.where(rid < r, pltpu.roll(prev, r, 0), rolled[:CONV_TAIL])
        shifted = jnp.concatenate([head, rolled[CONV_TAIL:]], axis=0)
        acc = acc + shifted * w[CONV_WIDTH - 1 - r:CONV_WIDTH - r, :]
    tail_ref[...] = x[x.shape[0] - CONV_TAIL:, :]
    return acc * _sigmoid(acc)


def _mlstm_kernel(q_ref, k_ref, v_ref, o_ref, ig_ref, fg_ref, cq_ref, ck_ref, hg_ref,
                  y_ref, s_ref, n_ref, m_ref, qt_ref, kt_ref, *, heads):
    @pl.when(pl.program_id(2) == 0)
    def _():
        s_ref[...] = jnp.zeros_like(s_ref)
        n_ref[...] = jnp.zeros_like(n_ref)
        m_ref[...] = jnp.zeros_like(m_ref)
        qt_ref[...] = jnp.zeros_like(qt_ref)
        kt_ref[...] = jnp.zeros_like(kt_ref)

    dk = q_ref.shape[-1] // heads
    dv = v_ref.shape[-1] // heads
    for h in range(heads):
        qc = slice(h * dk, (h + 1) * dk)
        vc = slice(h * dv, (h + 1) * dv)
        q = _conv_silu(q_ref[:, qc].astype(F32), qt_ref.at[h], cq_ref[:, qc]) * dk ** -0.5
        k = _conv_silu(k_ref[:, qc].astype(F32), kt_ref.at[h], ck_ref[:, qc])
        _mlstm_head(q, k, v_ref[:, vc], o_ref[:, vc], ig_ref[h], fg_ref[h], hg_ref[:, vc],
                    y_ref.at[:, vc], s_ref.at[h], n_ref.at[h], m_ref.at[h])


def _mlstm_head(q, k, vb, ob, i_row, f_row, hg, y_ref, s_ref, n_ref, m_ref):
    L = CHUNK
    qb = q.astype(BF16)
    kb = k.astype(BF16)
    v = vb.astype(F32)

    t_id = lax.broadcasted_iota(jnp.int32, (L, L), 0)
    s_id = lax.broadcasted_iota(jnp.int32, (L, L), 1)
    eye = t_id == s_id
    causal = t_id >= s_id

    logf_row = -(jnp.maximum(-f_row, 0.0) + jnp.log1p(jnp.exp(-jnp.abs(f_row))))
    logf_b = jnp.broadcast_to(logf_row, (L, L))
    i_b = jnp.broadcast_to(i_row, (L, L))
    logf_col = jnp.sum(jnp.where(eye, logf_b, 0.0), axis=1, keepdims=True)
    i_col = jnp.sum(jnp.where(eye, i_b, 0.0), axis=1, keepdims=True)
    b_col = jnp.sum(jnp.where(causal, logf_b, 0.0), axis=1, keepdims=True)
    b_row = jnp.sum(jnp.where(t_id <= s_id, jnp.broadcast_to(logf_col, (L, L)), 0.0),
                    axis=0, keepdims=True)
    g_tot = jnp.sum(logf_row, axis=1, keepdims=True)
    m0 = m_ref[:, 0:1]

    d_log = jnp.where(causal, b_col - b_row + i_row, -jnp.inf)
    m_inter = b_col + m0
    m_t = jnp.maximum(m_inter, jnp.max(d_log, axis=1, keepdims=True))
    qk = lax.dot_general(qb, kb, (((1,), (1,)), ((), ())), preferred_element_type=F32)
    scores = qk * jnp.exp(d_log - m_t)
    inter = jnp.exp(m_inter - m_t)
    num = (jnp.dot(scores.astype(BF16), vb, preferred_element_type=F32)
           + inter * jnp.dot(qb, s_ref[...].astype(BF16), preferred_element_type=F32))
    den = (jnp.sum(scores, axis=1, keepdims=True)
           + inter * jnp.sum(q * n_ref[...], axis=1, keepdims=True))
    h = num / jnp.maximum(jnp.abs(den), jnp.exp(-m_t))
    y_ref[...] = (_sigmoid(ob.astype(F32)) * _rms(h, hg)).astype(y_ref.dtype)

    w_log = g_tot - b_col + i_col
    a = jnp.max(w_log, axis=0, keepdims=True)
    w = jnp.exp(w_log - a)
    s_loc = lax.dot_general(kb, (w * v).astype(BF16), (((0,), (0,)), ((), ())),
                            preferred_element_type=F32)
    n_loc = jnp.sum(w * k, axis=0, keepdims=True)
    m_new = jnp.maximum(g_tot + m0, a)
    sc_prev = jnp.exp(g_tot + m0 - m_new)
    sc_loc = jnp.exp(a - m_new)
    s_ref[...] = sc_prev * s_ref[...] + sc_loc * s_loc
    n_ref[...] = sc_prev * n_ref[...] + sc_loc * n_loc
    m_ref[...] = jnp.broadcast_to(m_new, m_ref.shape)


def mlstm_core(proj, gates_t, conv_w, head_g, batch):
    M = proj.shape[0]
    H = MLSTM_HEADS
    G = MLSTM_HEADS_PER_STEP
    assert H % G == 0
    ng = H // G
    qkw = conv_w.shape[1]
    dk = qkw // (2 * H)
    dv = head_g.shape[0] // H
    nc = M // batch // CHUNK
    assert qkw % (G * dv) == 0
    v0 = qkw // (G * dv)
    gates4 = gates_t.reshape(2 * ng, G, 1, M)
    row = lambda b, c: b * nc + c
    return pl.pallas_call(
        functools.partial(_mlstm_kernel, heads=G),
        grid=(batch, ng, nc),
        in_specs=[pl.BlockSpec((CHUNK, G * dk), lambda b, g, c: (row(b, c), g)),
                  pl.BlockSpec((CHUNK, G * dk), lambda b, g, c: (row(b, c), ng + g)),
                  pl.BlockSpec((CHUNK, G * dv), lambda b, g, c: (row(b, c), v0 + g)),
                  pl.BlockSpec((CHUNK, G * dv), lambda b, g, c: (row(b, c), v0 + ng + g)),
                  pl.BlockSpec((None, G, 1, CHUNK), lambda b, g, c: (g, 0, 0, row(b, c))),
                  pl.BlockSpec((None, G, 1, CHUNK), lambda b, g, c: (ng + g, 0, 0, row(b, c))),
                  pl.BlockSpec((CONV_WIDTH, G * dk), lambda b, g, c: (0, g)),
                  pl.BlockSpec((CONV_WIDTH, G * dk), lambda b, g, c: (0, ng + g)),
                  pl.BlockSpec((1, G * dv), lambda b, g, c: (0, g))],
        out_specs=pl.BlockSpec((CHUNK, G * dv), lambda b, g, c: (row(b, c), g)),
        out_shape=jax.ShapeDtypeStruct((M, H * dv), BF16),
        scratch_shapes=[pltpu.VMEM((G, dk, dv), F32), pltpu.VMEM((G, 1, dk), F32),
                        pltpu.VMEM((G, 1, LANES), F32),
                        pltpu.VMEM((G, CONV_TAIL, dk), F32), pltpu.VMEM((G, CONV_TAIL, dk), F32)],
        compiler_params=_cparams(3),
        name="mlstm_core",
    )(proj, proj, proj, proj, gates4, gates4, conv_w, conv_w, head_g.reshape(1, H * dv))


def _router_kernel(x_ref, g_ref, w_ref, idx_ref, wt_ref):
    hn = _rms(x_ref[...], g_ref[...])
    logits = jnp.dot(hn, w_ref[...], preferred_element_type=F32,
                     precision=lax.Precision.HIGHEST)
    lane = lax.broadcasted_iota(jnp.int32, logits.shape, 1)
    neg = -jnp.inf
    l1 = jnp.where(lane < N_EXPERTS, logits, neg)
    m1 = jnp.max(l1, axis=1, keepdims=True)
    i1 = jnp.min(jnp.where(l1 == m1, lane, LANES), axis=1, keepdims=True)
    l2 = jnp.where(lane == i1, neg, l1)
    m2 = jnp.max(l2, axis=1, keepdims=True)
    i2 = jnp.min(jnp.where(l2 == m2, lane, LANES), axis=1, keepdims=True)
    r = jnp.exp(m2 - m1)
    w1 = 1.0 / (1.0 + r)
    w2 = r / (1.0 + r)
    idx_ref[...] = jnp.where(lane == 0, i1, jnp.where(lane == 1, i2, 0))
    wt_ref[...] = jnp.where(lane == 0, w1, jnp.where(lane == 1, w2, 0.0))


def moe_router(x, g, w_router, tm=256):
    M, D = x.shape
    tm = _tile(tm, M)
    w_pad = jnp.pad(w_router, ((0, 0), (0, LANES - w_router.shape[1])))
    row = pl.BlockSpec((tm, D), lambda i: (i, 0))
    out = pl.BlockSpec((tm, LANES), lambda i: (i, 0))
    return pl.pallas_call(
        _router_kernel, grid=(M // tm,),
        in_specs=[row, pl.BlockSpec((1, D), lambda i: (0, 0)),
                  pl.BlockSpec((D, LANES), lambda i: (0, 0))],
        out_specs=[out, out],
        out_shape=[jax.ShapeDtypeStruct((M, LANES), jnp.int32),
                   jax.ShapeDtypeStruct((M, LANES), F32)],
        compiler_params=_cparams(1), name="moe_router",
    )(x, g.reshape(1, D), w_pad)


def _row_copy(src_hbm, src_row, dst_ref, dst_row, sem):
    return pltpu.make_async_copy(src_hbm.at[pl.ds(src_row, 1)],
                                 dst_ref.at[pl.ds(dst_row, 1)], sem)


def _dispatch_kernel(tok_ref, nt_ref, x_hbm, g_ref, o_ref, buf_ref, sem_ref, *, tm):
    i = pl.program_id(0)
    nt = nt_ref[0]
    n_blocks = tm // GATHER_ROWS

    def start_rows(tile, slot, blk):
        for q in range(GATHER_ROWS):
            r = blk * GATHER_ROWS + q
            _row_copy(x_hbm, tok_ref[tile * tm + r], buf_ref.at[slot], r,
                      sem_ref.at[slot]).start()

    def wait_rows(slot, blk):
        for q in range(GATHER_ROWS):
            _row_copy(x_hbm, 0, buf_ref.at[slot], blk * GATHER_ROWS + q,
                      sem_ref.at[slot]).wait()

    def norm_rows(slot, blk):
        rows = pl.ds(pl.multiple_of(blk * GATHER_ROWS, GATHER_ROWS), GATHER_ROWS)
        o_ref[rows, :] = _rms(buf_ref[slot, rows, :], g_ref[...]).astype(o_ref.dtype)

    def loop(body):
        lax.fori_loop(0, n_blocks, lambda blk, c: (body(blk), c)[1], 0, unroll=GATHER_UNROLL)

    @pl.when(i == 0)
    def _():
        loop(lambda blk: start_rows(0, 0, blk))

    slot = i % 2

    @pl.when(i < nt)
    def _():
        loop(lambda blk: wait_rows(slot, blk))

    @pl.when(i + 1 < nt)
    def _():
        def body(blk):
            start_rows(i + 1, 1 - slot, blk)
            norm_rows(slot, blk)
        loop(body)

    @pl.when(i + 1 == nt)
    def _():
        loop(lambda blk: norm_rows(slot, blk))

    @pl.when(i >= nt)
    def _():
        o_ref[...] = jnp.zeros_like(o_ref)


def moe_dispatch(x, g, token_of_slot, n_tiles, tm):
    T, D = x.shape
    P = token_of_slot.shape[0]
    return pl.pallas_call(
        functools.partial(_dispatch_kernel, tm=tm),
        grid_spec=pltpu.PrefetchScalarGridSpec(
            num_scalar_prefetch=2,
            grid=(P // tm,),
            in_specs=[pl.BlockSpec(memory_space=pl.ANY),
                      pl.BlockSpec((1, D), lambda i, tok, nt: (0, 0))],
            out_specs=pl.BlockSpec((tm, D), lambda i, tok, nt: (i, 0)),
            scratch_shapes=[pltpu.VMEM((2, tm, D), F32), pltpu.SemaphoreType.DMA((2,))]),
        out_shape=jax.ShapeDtypeStruct((P, D), BF16),
        compiler_params=_cparams(1),
        name="moe_dispatch",
    )(token_of_slot, n_tiles, x, g.reshape(1, D))


def _weights_changed(te_ref, i):
    return (i == 0) | (te_ref[i] != te_ref[jnp.maximum(i - 1, 0)])


def _stage_copies(w_hbms, e, j, tn, stage_refs, sem_ref):
    col = pl.multiple_of(j * tn, tn)
    return [pltpu.make_async_copy(w.at[e, :, pl.ds(col, tn)], s, sem_ref.at[n])
            for n, (w, s) in enumerate(zip(w_hbms, stage_refs))]


def _restage_weights(te_ref, nx_ref, w_hbms, stage_refs, wb_refs, sem_ref, *, tn, nj):
    j = pl.program_id(0)
    i = pl.program_id(1)

    @pl.when((j == 0) & (i == 0))
    def _():
        for c in _stage_copies(w_hbms, te_ref[0], 0, tn, stage_refs, sem_ref):
            c.start()

    @pl.when(_weights_changed(te_ref, i))
    def _():
        for c in _stage_copies(w_hbms, te_ref[i], j, tn, stage_refs, sem_ref):
            c.wait()
        for s, wb in zip(stage_refs, wb_refs):
            wb[...] = s[...].astype(BF16)
        nxt = nx_ref[i]
        last = nxt < 0
        e_next = jnp.where(last, te_ref[0], nxt)
        j_next = jnp.where(last, j + 1, j)

        @pl.when(j_next < nj)
        def _():
            for c in _stage_copies(w_hbms, e_next, j_next, tn, stage_refs, sem_ref):
                c.start()


def _moe_up_kernel(te_ref, nt_ref, nx_ref, a_ref, wg_hbm, wu_hbm, o_ref,
                   sg_ref, su_ref, wgb_ref, wub_ref, sem_ref, *, tn, nj):
    _restage_weights(te_ref, nx_ref, (wg_hbm, wu_hbm), (sg_ref, su_ref),
                     (wgb_ref, wub_ref), sem_ref, tn=tn, nj=nj)

    @pl.when(pl.program_id(1) < nt_ref[0])
    def _():
        a = a_ref[...]
        g = jnp.dot(a, wgb_ref[...], preferred_element_type=F32)
        u = jnp.dot(a, wub_ref[...], preferred_element_type=F32)
        o_ref[...] = (g * _sigmoid(g) * u).astype(o_ref.dtype)

    @pl.when(pl.program_id(1) >= nt_ref[0])
    def _():
        o_ref[...] = jnp.zeros_like(o_ref)


def _moe_down_kernel(te_ref, nt_ref, nx_ref, a_ref, w_hbm, o_ref,
                     s_ref, wb_ref, sem_ref, *, tn, nj):
    _restage_weights(te_ref, nx_ref, (w_hbm,), (s_ref,), (wb_ref,), sem_ref, tn=tn, nj=nj)

    @pl.when(pl.program_id(1) < nt_ref[0])
    def _():
        o_ref[...] = jnp.dot(a_ref[...], wb_ref[...],
                             preferred_element_type=F32).astype(o_ref.dtype)

    @pl.when(pl.program_id(1) >= nt_ref[0])
    def _():
        o_ref[...] = jnp.zeros_like(o_ref)


def _moe_grouped(kernel_fn, name, a, weights, plan, tm, tn, out_dtype):
    te, n_tiles, nxt = plan
    P, K = a.shape
    N = weights[0].shape[2]
    tn = _tile(tn, N)
    nj = N // tn
    used = lambda i, nt: jnp.minimum(i, nt[0] - 1)
    n_w = len(weights)
    return pl.pallas_call(
        functools.partial(kernel_fn, tn=tn, nj=nj),
        grid_spec=pltpu.PrefetchScalarGridSpec(
            num_scalar_prefetch=3,
            grid=(nj, P // tm),
            in_specs=[pl.BlockSpec((tm, K), lambda j, i, te, nt, nx: (used(i, nt), 0))]
                     + [pl.BlockSpec(memory_space=pl.ANY)] * n_w,
            out_specs=pl.BlockSpec((tm, tn), lambda j, i, te, nt, nx: (i, j)),
            scratch_shapes=[pltpu.VMEM((K, tn), F32)] * n_w + [pltpu.VMEM((K, tn), BF16)] * n_w
                           + [pltpu.SemaphoreType.DMA((n_w,))]),
        out_shape=jax.ShapeDtypeStruct((P, N), out_dtype),
        compiler_params=_cparams(2),
        name=name,
    )(te, n_tiles, nxt, a, *weights)


def moe_up(xs, wg, wu, plan, tm, tn=512):
    return _moe_grouped(_moe_up_kernel, "moe_up", xs, (wg, wu), plan, tm, tn, BF16)


def moe_down(hs, wd, plan, tm, tn=1024):
    return _moe_grouped(_moe_down_kernel, "moe_down", hs, (wd,), plan, tm, tn, F32)


def _combine_kernel(slot_ref, x_ref, wt_ref, gp_ref, ys_hbm, xo_ref, buf_ref, sem_ref, *, tm):
    i = pl.program_id(0)
    n_blocks = tm // COMBINE_ROWS

    def start_rows(tile, s, blk):
        for q in range(COMBINE_ROWS):
            r = blk * COMBINE_ROWS + q
            for k in range(TOP_K):
                _row_copy(ys_hbm, slot_ref[(tile * tm + r) * TOP_K + k], buf_ref.at[s, k], r,
                          sem_ref.at[s]).start()

    def wait_rows(s, blk):
        for q in range(COMBINE_ROWS):
            for k in range(TOP_K):
                _row_copy(ys_hbm, 0, buf_ref.at[s, k], blk * COMBINE_ROWS + q,
                          sem_ref.at[s]).wait()

    def combine_rows(s, blk):
        rows = pl.ds(pl.multiple_of(blk * COMBINE_ROWS, COMBINE_ROWS), COMBINE_ROWS)
        y = (wt_ref[rows, 0:1] * buf_ref[s, 0, rows, :]
             + wt_ref[rows, 1:2] * buf_ref[s, 1, rows, :])
        xo_ref[rows, :] = x_ref[rows, :] + _rms(y, gp_ref[...])

    def loop(body):
        lax.fori_loop(0, n_blocks, lambda blk, c: (body(blk), c)[1], 0, unroll=COMBINE_UNROLL)

    @pl.when(i == 0)
    def _():
        loop(lambda blk: start_rows(0, 0, blk))

    s = i % 2
    loop(lambda blk: wait_rows(s, blk))

    @pl.when(i + 1 < pl.num_programs(0))
    def _():
        def body(blk):
            start_rows(i + 1, 1 - s, blk)
            combine_rows(s, blk)
        loop(body)

    @pl.when(i + 1 == pl.num_programs(0))
    def _():
        loop(lambda blk: combine_rows(s, blk))


def moe_combine_residual(x, ys, slot, wts, g_post, tm=256):
    M, D = x.shape
    tm = _tile(tm, M)
    row = pl.BlockSpec((tm, D), lambda i, sl: (i, 0))
    return pl.pallas_call(
        functools.partial(_combine_kernel, tm=tm),
        grid_spec=pltpu.PrefetchScalarGridSpec(
            num_scalar_prefetch=1,
            grid=(M // tm,),
            in_specs=[row, pl.BlockSpec((tm, LANES), lambda i, sl: (i, 0)),
                      pl.BlockSpec((1, D), lambda i, sl: (0, 0)),
                      pl.BlockSpec(memory_space=pl.ANY)],
            out_specs=row,
            scratch_shapes=[pltpu.VMEM((2, TOP_K, tm, D), F32), pltpu.SemaphoreType.DMA((2,))]),
        out_shape=jax.ShapeDtypeStruct((M, D), F32),
        compiler_params=_cparams(1), name="moe_combine_residual",
    )(slot, x, wts, g_post.reshape(1, D), ys)


def _moe_plan(idx, n_tokens, tm):
    e_flat = idx.reshape(-1)
    onehot = (e_flat[:, None] == jnp.arange(N_EXPERTS)[None, :]).astype(jnp.int32)
    counts = jnp.sum(onehot, axis=0)
    rank = jnp.sum((jnp.cumsum(onehot, axis=0) - onehot) * onehot, axis=1)
    tiles_per = (counts + tm - 1) // tm
    tile_end = jnp.cumsum(tiles_per)
    tile_start = tile_end - tiles_per
    slot = (tile_start[e_flat] * tm + rank).astype(jnp.int32)
    n_tiles_max = (n_tokens * TOP_K) // tm + N_EXPERTS
    n_tiles = tile_end[-1]
    tile_ids = jnp.minimum(jnp.arange(n_tiles_max), n_tiles - 1)
    expert_of = lambda t: jnp.minimum(
        jnp.sum((tile_end[None, :] <= t[:, None]).astype(jnp.int32), axis=1), N_EXPERTS - 1)
    te = expert_of(tile_ids)
    group_end = tile_end[te]
    nxt = jnp.where(group_end < n_tiles, expert_of(jnp.minimum(group_end, n_tiles - 1)), -1)
    token_of_pair = jnp.arange(n_tokens * TOP_K, dtype=jnp.int32) // TOP_K
    token_of_slot = jnp.zeros((n_tiles_max * tm,), jnp.int32).at[slot].set(token_of_pair)
    plan = (te.astype(jnp.int32), n_tiles.reshape(1).astype(jnp.int32), nxt.astype(jnp.int32))
    return slot, token_of_slot, plan


def kernel(x, mem, mem_norm, mem_kv, l0_norm_mix_pre, l0_mix_in, l0_sgu_ln_g, l0_sgu_ln_b, l0_sgu_w, l0_sgu_b, l0_mix_out, l0_norm_mix_post, l0_norm_x_pre, l0_xq, l0_xo, l0_norm_x_post, l0_norm_ffn_pre, l0_ffn_gate, l0_ffn_up, l0_ffn_down, l0_norm_ffn_post, l1_norm_mix_pre, l1_mix_in, l1_gate_b, l1_conv, l1_head_norm, l1_mix_out, l1_norm_mix_post, l1_norm_x_pre, l1_xq, l1_xo, l1_norm_x_post, l1_norm_ffn_pre, l1_router, l1_moe_gate, l1_moe_up, l1_moe_down, l1_norm_ffn_post):
    B, S, D = x.shape
    T = B * S
    n_mem = mem.shape[1]
    xf = x.reshape(T, D)

    memn = rmsnorm_rows(mem.reshape(B * n_mem, D), mem_norm)
    kv = matmul_fullk(memn, mem_kv, BF16).reshape(B, n_mem, 2 * D)

    hn = rmsnorm_rows(xf, l0_norm_mix_pre)
    z = matmul_fullk(hn, l0_mix_in, BF16, act="gelu")
    y = sgu_mix(z, l0_sgu_ln_g, l0_sgu_ln_b, l0_sgu_w, l0_sgu_b)
    y = matmul_fullk(y, l0_mix_out, BF16)
    xf, hn = residual_norm(xf, y, l0_norm_mix_post, l0_norm_x_pre)
    o = cross_attention(matmul_fullk(hn, l0_xq, BF16), kv, B)
    y = matmul_fullk(o, l0_xo, BF16)
    xf, hn = residual_norm(xf, y, l0_norm_x_post, l0_norm_ffn_pre)
    hmid = swiglu_up(hn, l0_ffn_gate, l0_ffn_up)
    y = matmul_tiledk(hmid, l0_ffn_down, BF16)
    xf, hn = residual_norm(xf, y, l0_norm_ffn_post, l1_norm_mix_pre)

    n_gates = 2 * MLSTM_HEADS
    n_main = l1_mix_in.shape[1] - n_gates
    w_in_t = l1_mix_in.T
    proj = matmul_fullk(hn, w_in_t, BF16, w_rows=(0, n_main))
    gates_t = mlstm_gates(hn, w_in_t[n_main:], l1_gate_b)
    y = mlstm_core(proj, gates_t, l1_conv, l1_head_norm, B)
    y = matmul_fullk(y, l1_mix_out, BF16)
    xf, hn = residual_norm(xf, y, l1_norm_mix_post, l1_norm_x_pre)
    o = cross_attention(matmul_fullk(hn, l1_xq, BF16), kv, B)
    y = matmul_fullk(o, l1_xo, BF16)
    xf = residual_norm(xf, y, l1_norm_x_post, None)
    idx, wts = moe_router(xf, l1_norm_ffn_pre, l1_router)
    tm = min(MOE_TM, T)
    slot, token_of_slot, plan = _moe_plan(idx[:, :TOP_K], T, tm)
    xs = moe_dispatch(xf, l1_norm_ffn_pre, token_of_slot, plan[1], tm)
    hs = moe_up(xs, l1_moe_gate, l1_moe_up, plan, tm)
    ys = moe_down(hs, l1_moe_down, plan, tm)
    xf = moe_combine_residual(xf, ys, slot, wts, l1_norm_ffn_post)
    return xf.reshape(B, S, D)
```

```python
import functools

import jax
import jax.numpy as jnp
from jax import lax
from jax.experimental import pallas as pl
from jax.experimental.pallas import tpu as pltpu

F32 = jnp.float32
BF16 = jnp.bfloat16

EPS = 1e-6
CHUNK = 128
SGU_GROUPS = 8
MLSTM_HEADS = 8
CONV_WIDTH = 4
GATE_CAP = 15.0
X_HEADS = 4
N_EXPERTS = 8
TOP_K = 2
LANES = 128
MOE_TM = 512
MOE_ROW_CLASSES = (128, 256)
GATHER_ROWS = 16
GATHER_UNROLL = 4
COMBINE_ROWS = 8
COMBINE_UNROLL = 4
MLSTM_HEADS_PER_STEP = 2
W_DMA_SPLIT = 1
CONV_TAIL = 8
VMEM_LIMIT = 56 * 1024 * 1024


def _cparams(n_axes, vmem=VMEM_LIMIT):
    return pltpu.CompilerParams(
        dimension_semantics=("arbitrary",) * n_axes, vmem_limit_bytes=vmem)


def _tile(pref, dim):
    t = min(pref, dim)
    assert dim % t == 0, (pref, dim)
    return t


def _rms(x, g):
    return x * lax.rsqrt(jnp.mean(x * x, axis=-1, keepdims=True) + EPS) * g


def _sigmoid(x):
    return 1.0 / (1.0 + jnp.exp(-x))


def _rms_kernel(x_ref, g_ref, o_ref):
    o_ref[...] = _rms(x_ref[...], g_ref[...]).astype(o_ref.dtype)


def rmsnorm_rows(x, g, tm=256):
    M, D = x.shape
    tm = _tile(tm, M)
    return pl.pallas_call(
        _rms_kernel,
        grid=(M // tm,),
        in_specs=[pl.BlockSpec((tm, D), lambda i: (i, 0)),
                  pl.BlockSpec((1, D), lambda i: (0, 0))],
        out_specs=pl.BlockSpec((tm, D), lambda i: (i, 0)),
        out_shape=jax.ShapeDtypeStruct((M, D), BF16),
        compiler_params=_cparams(1),
        name="rmsnorm_rows",
    )(x, g.reshape(1, D))


def _gelu(x):
    return 0.5 * x * (1.0 + lax.erf(x * 0.7071067811865476))


def _stream_weights(w_hbms, stage_refs, wb_refs, sem_ref, *, tn, nj, transposed, first=0):
    j = pl.program_id(0)
    i = pl.program_id(1)

    def copies(jj):
        off = pl.multiple_of(first + jj * tn, tn)
        out = []
        for n, (w, s) in enumerate(zip(w_hbms, stage_refs)):
            rows = s.shape[0] // W_DMA_SPLIT
            for q in range(W_DMA_SPLIT):
                band = pl.ds(q * rows, rows)
                src = (w.at[pl.ds(off + q * rows, rows), :] if transposed
                       else w.at[band, pl.ds(off, tn)])
                out.append(pltpu.make_async_copy(src, s.at[band, :], sem_ref.at[n]))
        return out

    @pl.when((j == 0) & (i == 0))
    def _():
        for c in copies(0):
            c.start()

    @pl.when(i == 0)
    def _():
        for c in copies(j):
            c.wait()
        for s, wb in zip(stage_refs, wb_refs):
            wb[...] = (s[...].T if transposed else s[...]).astype(BF16)

        @pl.when(j + 1 < nj)
        def _():
            for c in copies(j + 1):
                c.start()


def _mm_kernel(a_ref, w_hbm, o_ref, stage_ref, wb_ref, sem_ref, *, act, tn, nj, transposed,
               first):
    _stream_weights((w_hbm,), (stage_ref,), (wb_ref,), sem_ref,
                    tn=tn, nj=nj, transposed=transposed, first=first)
    acc = jnp.dot(a_ref[...], wb_ref[...], preferred_element_type=F32)
    if act == "gelu":
        acc = _gelu(acc)
    o_ref[...] = acc.astype(o_ref.dtype)


def matmul_fullk(a, w, out_dtype, act=None, w_rows=None, tm=1024, tn=1024):
    M, K = a.shape
    transposed = w_rows is not None
    first, N = w_rows if transposed else (0, w.shape[1])
    tm = _tile(tm, M)
    tn = _tile(tn, N)
    assert first % tn == 0
    nj = N // tn
    stage_block = (tn, K) if transposed else (K, tn)
    return pl.pallas_call(
        functools.partial(_mm_kernel, act=act, tn=tn, nj=nj, transposed=transposed,
                          first=first),
        grid=(nj, M // tm),
        in_specs=[pl.BlockSpec((tm, K), lambda j, i: (i, 0)),
                  pl.BlockSpec(memory_space=pl.ANY)],
        out_specs=pl.BlockSpec((tm, tn), lambda j, i: (i, j)),
        out_shape=jax.ShapeDtypeStruct((M, N), out_dtype),
        scratch_shapes=[pltpu.VMEM(stage_block, F32), pltpu.VMEM((K, tn), BF16),
                        pltpu.SemaphoreType.DMA((1,))],
        compiler_params=_cparams(2),
        name="matmul_fullk" + ("_" + act if act else "") + ("_t" if transposed else ""),
    )(a, w)


def _swiglu_kernel(a_ref, wg_hbm, wu_hbm, o_ref, sg_ref, su_ref, wgb_ref, wub_ref, sem_ref,
                   *, tn, nj):
    _stream_weights((wg_hbm, wu_hbm), (sg_ref, su_ref), (wgb_ref, wub_ref), sem_ref,
                    tn=tn, nj=nj, transposed=False)
    a = a_ref[...]
    g = jnp.dot(a, wgb_ref[...], preferred_element_type=F32)
    u = jnp.dot(a, wub_ref[...], preferred_element_type=F32)
    o_ref[...] = (g * _sigmoid(g) * u).astype(o_ref.dtype)


def swiglu_up(a, wg, wu, tm=512, tn=512):
    M, K = a.shape
    F = wg.shape[1]
    tm = _tile(tm, M)
    tn = _tile(tn, F)
    nj = F // tn
    hbm = pl.BlockSpec(memory_space=pl.ANY)
    return pl.pallas_call(
        functools.partial(_swiglu_kernel, tn=tn, nj=nj),
        grid=(nj, M // tm),
        in_specs=[pl.BlockSpec((tm, K), lambda j, i: (i, 0)), hbm, hbm],
        out_specs=pl.BlockSpec((tm, tn), lambda j, i: (i, j)),
        out_shape=jax.ShapeDtypeStruct((M, F), BF16),
        scratch_shapes=[pltpu.VMEM((K, tn), F32), pltpu.VMEM((K, tn), F32),
                        pltpu.VMEM((K, tn), BF16), pltpu.VMEM((K, tn), BF16),
                        pltpu.SemaphoreType.DMA((2,))],
        compiler_params=_cparams(2),
        name="swiglu_up",
    )(a, wg, wu)


def _mmk_kernel(a_ref, w_ref, o_ref, acc_ref, *, nk):
    k = pl.program_id(2)

    @pl.when(k == 0)
    def _():
        acc_ref[...] = jnp.zeros_like(acc_ref)

    acc_ref[...] += jnp.dot(a_ref[...], w_ref[...].astype(BF16),
                            preferred_element_type=F32)

    @pl.when(k == nk - 1)
    def _():
        o_ref[...] = acc_ref[...].astype(o_ref.dtype)


def matmul_tiledk(a, w, out_dtype, tm=2048, tn=1024, tk=1024):
    M, K = a.shape
    N = w.shape[1]
    tm, tn, tk = _tile(tm, M), _tile(tn, N), _tile(tk, K)
    nk = K // tk
    return pl.pallas_call(
        functools.partial(_mmk_kernel, nk=nk),
        grid=(N // tn, M // tm, nk),
        in_specs=[pl.BlockSpec((tm, tk), lambda j, i, k: (i, k)),
                  pl.BlockSpec((tk, tn), lambda j, i, k: (k, j))],
        out_specs=pl.BlockSpec((tm, tn), lambda j, i, k: (i, j)),
        out_shape=jax.ShapeDtypeStruct((M, N), out_dtype),
        scratch_shapes=[pltpu.VMEM((tm, tn), F32)],
        compiler_params=_cparams(3),
        name="matmul_tiledk",
    )(a, w)


def _res_kernel(x_ref, y_ref, gp_ref, gn_ref, xo_ref, hn_ref):
    xn = x_ref[...] + _rms(y_ref[...].astype(F32), gp_ref[...])
    xo_ref[...] = xn
    hn_ref[...] = _rms(xn, gn_ref[...]).astype(hn_ref.dtype)


def residual_norm(x, y, g_post, g_next, tm=256):
    M, D = x.shape
    tm = _tile(tm, M)
    row = pl.BlockSpec((tm, D), lambda i: (i, 0))
    vec = pl.BlockSpec((1, D), lambda i: (0, 0))
    return pl.pallas_call(
        _res_kernel, grid=(M // tm,),
        in_specs=[row, row, vec, vec], out_specs=[row, row],
        out_shape=[jax.ShapeDtypeStruct((M, D), F32), jax.ShapeDtypeStruct((M, D), BF16)],
        compiler_params=_cparams(1), name="residual_norm",
    )(x, y, g_post.reshape(1, D), g_next.reshape(1, D))


def _sgu_kernel(u_ref, v_ref, lg_ref, lb_ref, w_ref, bt_ref, o_ref, wm_ref, *, groups):
    @pl.when(pl.program_id(0) == 0)
    def _():
        t = lax.broadcasted_iota(jnp.int32, (CHUNK, CHUNK), 0)
        s = lax.broadcasted_iota(jnp.int32, (CHUNK, CHUNK), 1)
        for g in range(groups):
            wm_ref[g] = jnp.where(t >= s, w_ref[g], 0.0).astype(BF16)

    v = v_ref[...].astype(F32)
    vc = v - jnp.mean(v, axis=-1, keepdims=True)
    vn = vc * lax.rsqrt(jnp.mean(vc * vc, axis=-1, keepdims=True) + EPS)
    vn = (vn * lg_ref[...] + lb_ref[...]).astype(BF16)
    tm, width = vn.shape
    gd = width // groups
    for c in range(tm // CHUNK):
        rows = slice(c * CHUNK, (c + 1) * CHUNK)
        for g in range(groups):
            cols = slice(g * gd, (g + 1) * gd)
            mixed = jnp.dot(wm_ref[g], vn[rows, cols], preferred_element_type=F32)
            mixed = mixed + bt_ref[:, g:g + 1]
            o_ref[rows, cols] = (u_ref[rows, cols].astype(F32) * mixed).astype(o_ref.dtype)


def sgu_mix(z, ln_g, ln_b, sgu_w, sgu_b, tm=256):
    M, W2 = z.shape
    W = W2 // 2
    G = sgu_w.shape[0]
    tm = _tile(tm, M)
    return pl.pallas_call(
        functools.partial(_sgu_kernel, groups=G),
        grid=(M // tm,),
        in_specs=[pl.BlockSpec((tm, W), lambda i: (i, 0)),
                  pl.BlockSpec((tm, W), lambda i: (i, 1)),
                  pl.BlockSpec((1, W), lambda i: (0, 0)),
                  pl.BlockSpec((1, W), lambda i: (0, 0)),
                  pl.BlockSpec((G, CHUNK, CHUNK), lambda i: (0, 0, 0)),
                  pl.BlockSpec((CHUNK, G), lambda i: (0, 0))],
        out_specs=pl.BlockSpec((tm, W), lambda i: (i, 0)),
        out_shape=jax.ShapeDtypeStruct((M, W), BF16),
        scratch_shapes=[pltpu.VMEM((G, CHUNK, CHUNK), BF16)],
        compiler_params=_cparams(1),
        name="sgu_mix",
    )(z, z, ln_g.reshape(1, W), ln_b.reshape(1, W), sgu_w, sgu_b.T)


def _xattn_kernel(q_ref, k_ref, v_ref, o_ref, *, heads, scale):
    hd = q_ref.shape[-1] // heads
    for h in range(heads):
        cols = slice(h * hd, (h + 1) * hd)
        s = lax.dot_general(q_ref[:, cols], k_ref[0, :, cols], (((1,), (1,)), ((), ())),
                            preferred_element_type=F32) * scale
        p = jnp.exp(s - jnp.max(s, axis=-1, keepdims=True))
        p = p / jnp.sum(p, axis=-1, keepdims=True)
        o = jnp.dot(p.astype(BF16), v_ref[0, :, cols], preferred_element_type=F32)
        o_ref[:, cols] = o.astype(o_ref.dtype)


def cross_attention(q, kv, batch, tm=512):
    M, D = q.shape
    S = M // batch
    n_mem = kv.shape[1]
    tm = _tile(tm, S)
    spt = S // tm
    return pl.pallas_call(
        functools.partial(_xattn_kernel, heads=X_HEADS, scale=(D // X_HEADS) ** -0.5),
        grid=(batch, spt),
        in_specs=[pl.BlockSpec((tm, D), lambda b, m: (b * spt + m, 0)),
                  pl.BlockSpec((1, n_mem, D), lambda b, m: (b, 0, 0)),
                  pl.BlockSpec((1, n_mem, D), lambda b, m: (b, 0, 1))],
        out_specs=pl.BlockSpec((tm, D), lambda b, m: (b * spt + m, 0)),
        out_shape=jax.ShapeDtypeStruct((M, D), BF16),
        compiler_params=_cparams(2),
        name="cross_attention",
    )(q, kv, kv)


def _gates_kernel(hn_ref, w_ref, b_ref, o_ref):
    acc = lax.dot_general(w_ref[...].astype(BF16), hn_ref[...], (((1,), (1,)), ((), ())),
                          preferred_element_type=F32)
    o_ref[...] = GATE_CAP * jnp.tanh((acc + b_ref[...]) / GATE_CAP)


def mlstm_gates(hn, w_gates_t, gate_b, tm=512):
    M, D = hn.shape
    n_gates = w_gates_t.shape[0]
    tm = _tile(tm, M)
    return pl.pallas_call(
        _gates_kernel,
        grid=(M // tm,),
        in_specs=[pl.BlockSpec((tm, D), lambda i: (i, 0)),
                  pl.BlockSpec((n_gates, D), lambda i: (0, 0)),
                  pl.BlockSpec((n_gates, 1), lambda i: (0, 0))],
        out_specs=pl.BlockSpec((n_gates, tm), lambda i: (0, i)),
        out_shape=jax.ShapeDtypeStruct((n_gates, M), F32),
        compiler_params=_cparams(1),
        name="mlstm_gates",
    )(hn, w_gates_t, gate_b.reshape(n_gates, 1))


def _conv_silu(x, tail_ref, w):
    prev = tail_ref[...]
    rid = lax.broadcasted_iota(jnp.int32, prev.shape, 0)
    acc = x * w[CONV_WIDTH - 1:CONV_WIDTH, :]
    for r in range(1, CONV_WIDTH):
        rolled = pltpu.roll(x, r, 0)
        head = jnp.where(rid < r, pltpu.roll(prev, r, 0), rolled[:CONV_TAIL])
        shifted = jnp.concatenate([head, rolled[CONV_TAIL:]], axis=0)
        acc = acc + shifted * w[CONV_WIDTH - 1 - r:CONV_WIDTH - r, :]
    tail_ref[...] = x[x.shape[0] - CONV_TAIL:, :]
    return acc * _sigmoid(acc)


def _mlstm_kernel(q_ref, k_ref, v_ref, o_ref, ig_ref, fg_ref, cq_ref, ck_ref, hg_ref,
                  y_ref, s_ref, n_ref, m_ref, qt_ref, kt_ref, *, heads):
    @pl.when(pl.program_id(2) == 0)
    def _():
        s_ref[...] = jnp.zeros_like(s_ref)
        n_ref[...] = jnp.zeros_like(n_ref)
        m_ref[...] = jnp.zeros_like(m_ref)
        qt_ref[...] = jnp.zeros_like(qt_ref)
        kt_ref[...] = jnp.zeros_like(kt_ref)

    dk = q_ref.shape[-1] // heads
    dv = v_ref.shape[-1] // heads
    for h in range(heads):
        qc = slice(h * dk, (h + 1) * dk)
        vc = slice(h * dv, (h + 1) * dv)
        q = _conv_silu(q_ref[:, qc].astype(F32), qt_ref.at[h], cq_ref[:, qc]) * dk ** -0.5
        k = _conv_silu(k_ref[:, qc].astype(F32), kt_ref.at[h], ck_ref[:, qc])
        _mlstm_head(q, k, v_ref[:, vc], o_ref[:, vc], ig_ref[h], fg_ref[h], hg_ref[:, vc],
                    y_ref.at[:, vc], s_ref.at[h], n_ref.at[h], m_ref.at[h])


def _mlstm_head(q, k, vb, ob, i_row, f_row, hg, y_ref, s_ref, n_ref, m_ref):
    L = CHUNK
    qb = q.astype(BF16)
    kb = k.astype(BF16)
    v = vb.astype(F32)

    t_id = lax.broadcasted_iota(jnp.int32, (L, L), 0)
    s_id = lax.broadcasted_iota(jnp.int32, (L, L), 1)
    eye = t_id == s_id
    causal = t_id >= s_id

    logf_row = -(jnp.maximum(-f_row, 0.0) + jnp.log1p(jnp.exp(-jnp.abs(f_row))))
    logf_b = jnp.broadcast_to(logf_row, (L, L))
    i_b = jnp.broadcast_to(i_row, (L, L))
    logf_col = jnp.sum(jnp.where(eye, logf_b, 0.0), axis=1, keepdims=True)
    i_col = jnp.sum(jnp.where(eye, i_b, 0.0), axis=1, keepdims=True)
    b_col = jnp.sum(jnp.where(causal, logf_b, 0.0), axis=1, keepdims=True)
    b_row = jnp.sum(jnp.where(t_id <= s_id, jnp.broadcast_to(logf_col, (L, L)), 0.0),
                    axis=0, keepdims=True)
    g_tot = jnp.sum(logf_row, axis=1, keepdims=True)
    m0 = m_ref[:, 0:1]

    d_log = jnp.where(causal, b_col - b_row + i_row, -jnp.inf)
    m_inter = b_col + m0
    m_t = jnp.maximum(m_inter, jnp.max(d_log, axis=1, keepdims=True))
    qk = lax.dot_general(qb, kb, (((1,), (1,)), ((), ())), preferred_element_type=F32)
    scores = qk * jnp.exp(d_log - m_t)
    inter = jnp.exp(m_inter - m_t)
    num = (jnp.dot(scores.astype(BF16), vb, preferred_element_type=F32)
           + inter * jnp.dot(qb, s_ref[...].astype(BF16), preferred_element_type=F32))
    den = (jnp.sum(scores, axis=1, keepdims=True)
           + inter * jnp.sum(q * n_ref[...], axis=1, keepdims=True))
    h = num / jnp.maximum(jnp.abs(den), jnp.exp(-m_t))
    y_ref[...] = (_sigmoid(ob.astype(F32)) * _rms(h, hg)).astype(y_ref.dtype)

    w_log = g_tot - b_col + i_col
    a = jnp.max(w_log, axis=0, keepdims=True)
    w = jnp.exp(w_log - a)
    s_loc = lax.dot_general(kb, (w * v).astype(BF16), (((0,), (0,)), ((), ())),
                            preferred_element_type=F32)
    n_loc = jnp.sum(w * k, axis=0, keepdims=True)
    m_new = jnp.maximum(g_tot + m0, a)
    sc_prev = jnp.exp(g_tot + m0 - m_new)
    sc_loc = jnp.exp(a - m_new)
    s_ref[...] = sc_prev * s_ref[...] + sc_loc * s_loc
    n_ref[...] = sc_prev * n_ref[...] + sc_loc * n_loc
    m_ref[...] = jnp.broadcast_to(m_new, m_ref.shape)


def mlstm_core(proj, gates_t, conv_w, head_g, batch):
    M = proj.shape[0]
    H = MLSTM_HEADS
    G = MLSTM_HEADS_PER_STEP
    assert H % G == 0
    ng = H // G
    qkw = conv_w.shape[1]
    dk = qkw // (2 * H)
    dv = head_g.shape[0] // H
    nc = M // batch // CHUNK
    assert qkw % (G * dv) == 0
    v0 = qkw // (G * dv)
    gates4 = gates_t.reshape(2 * ng, G, 1, M)
    row = lambda b, c: b * nc + c
    return pl.pallas_call(
        functools.partial(_mlstm_kernel, heads=G),
        grid=(batch, ng, nc),
        in_specs=[pl.BlockSpec((CHUNK, G * dk), lambda b, g, c: (row(b, c), g)),
                  pl.BlockSpec((CHUNK, G * dk), lambda b, g, c: (row(b, c), ng + g)),
                  pl.BlockSpec((CHUNK, G * dv), lambda b, g, c: (row(b, c), v0 + g)),
                  pl.BlockSpec((CHUNK, G * dv), lambda b, g, c: (row(b, c), v0 + ng + g)),
                  pl.BlockSpec((None, G, 1, CHUNK), lambda b, g, c: (g, 0, 0, row(b, c))),
                  pl.BlockSpec((None, G, 1, CHUNK), lambda b, g, c: (ng + g, 0, 0, row(b, c))),
                  pl.BlockSpec((CONV_WIDTH, G * dk), lambda b, g, c: (0, g)),
                  pl.BlockSpec((CONV_WIDTH, G * dk), lambda b, g, c: (0, ng + g)),
                  pl.BlockSpec((1, G * dv), lambda b, g, c: (0, g))],
        out_specs=pl.BlockSpec((CHUNK, G * dv), lambda b, g, c: (row(b, c), g)),
        out_shape=jax.ShapeDtypeStruct((M, H * dv), BF16),
        scratch_shapes=[pltpu.VMEM((G, dk, dv), F32), pltpu.VMEM((G, 1, dk), F32),
                        pltpu.VMEM((G, 1, LANES), F32),
                        pltpu.VMEM((G, CONV_TAIL, dk), F32), pltpu.VMEM((G, CONV_TAIL, dk), F32)],
        compiler_params=_cparams(3),
        name="mlstm_core",
    )(proj, proj, proj, proj, gates4, gates4, conv_w, conv_w, head_g.reshape(1, H * dv))


def _res_router_kernel(x_ref, y_ref, gp_ref, gn_ref, w2_ref, xo_ref, idx_ref, wt_ref):
    xn = x_ref[...] + _rms(y_ref[...].astype(F32), gp_ref[...])
    xo_ref[...] = xn
    hn = _rms(xn, gn_ref[...])
    hi = hn.astype(BF16)
    lo = (hn - hi.astype(F32)).astype(BF16)
    t = jnp.dot(hi, w2_ref[...], preferred_element_type=F32)
    logits = (t[:, :LANES] + t[:, LANES:]
              + jnp.dot(lo, w2_ref[:, :LANES], preferred_element_type=F32))
    lane = lax.broadcasted_iota(jnp.int32, logits.shape, 1)
    neg = -jnp.inf
    l1 = jnp.where(lane < N_EXPERTS, logits, neg)
    m1 = jnp.max(l1, axis=1, keepdims=True)
    i1 = jnp.min(jnp.where(l1 == m1, lane, LANES), axis=1, keepdims=True)
    l2 = jnp.where(lane == i1, neg, l1)
    m2 = jnp.max(l2, axis=1, keepdims=True)
    i2 = jnp.min(jnp.where(l2 == m2, lane, LANES), axis=1, keepdims=True)
    r = jnp.exp(m2 - m1)
    w1 = 1.0 / (1.0 + r)
    w2 = r / (1.0 + r)
    idx_ref[...] = jnp.where(lane == 0, i1, jnp.where(lane == 1, i2, 0))
    wt_ref[...] = jnp.where(lane == 0, w1, jnp.where(lane == 1, w2, 0.0))


def residual_router(x, y, g_post, g_next, w_router, tm=256):
    M, D = x.shape
    tm = _tile(tm, M)
    w_pad = jnp.pad(w_router, ((0, 0), (0, LANES - w_router.shape[1])))
    w_hi = w_pad.astype(BF16)
    w_lo = (w_pad - w_hi.astype(F32)).astype(BF16)
    row = pl.BlockSpec((tm, D), lambda i: (i, 0))
    vec = pl.BlockSpec((1, D), lambda i: (0, 0))
    out = pl.BlockSpec((tm, LANES), lambda i: (i, 0))
    return pl.pallas_call(
        _res_router_kernel, grid=(M // tm,),
        in_specs=[row, row, vec, vec, pl.BlockSpec((D, 2 * LANES), lambda i: (0, 0))],
        out_specs=[row, out, out],
        out_shape=[jax.ShapeDtypeStruct((M, D), F32),
                   jax.ShapeDtypeStruct((M, LANES), jnp.int32),
                   jax.ShapeDtypeStruct((M, LANES), F32)],
        compiler_params=_cparams(1), name="residual_router",
    )(x, y, g_post.reshape(1, D), g_next.reshape(1, D),
      jnp.concatenate([w_hi, w_lo], axis=1))


def _row_copy(src_hbm, src_row, dst_ref, dst_row, sem):
    return pltpu.make_async_copy(src_hbm.at[pl.ds(src_row, 1)],
                                 dst_ref.at[pl.ds(dst_row, 1)], sem)


def _dispatch_kernel(tok_ref, nt_ref, x_hbm, g_ref, o_ref, buf_ref, sem_ref, *, tm):
    i = pl.program_id(0)
    nt = nt_ref[0]
    n_blocks = tm // GATHER_ROWS

    def start_rows(tile, slot, blk):
        for q in range(GATHER_ROWS):
            r = blk * GATHER_ROWS + q
            _row_copy(x_hbm, tok_ref[tile * tm + r], buf_ref.at[slot], r,
                      sem_ref.at[slot]).start()

    def wait_rows(slot, blk):
        for q in range(GATHER_ROWS):
            _row_copy(x_hbm, 0, buf_ref.at[slot], blk * GATHER_ROWS + q,
                      sem_ref.at[slot]).wait()

    def norm_rows(slot, blk):
        rows = pl.ds(pl.multiple_of(blk * GATHER_ROWS, GATHER_ROWS), GATHER_ROWS)
        o_ref[rows, :] = _rms(buf_ref[slot, rows, :], g_ref[...]).astype(o_ref.dtype)

    def loop(body):
        lax.fori_loop(0, n_blocks, lambda blk, c: (body(blk), c)[1], 0, unroll=GATHER_UNROLL)

    @pl.when(i == 0)
    def _():
        loop(lambda blk: start_rows(0, 0, blk))

    slot = i % 2

    @pl.when(i < nt)
    def _():
        loop(lambda blk: wait_rows(slot, blk))

    @pl.when(i + 1 < nt)
    def _():
        def body(blk):
            start_rows(i + 1, 1 - slot, blk)
            norm_rows(slot, blk)
        loop(body)

    @pl.when(i + 1 == nt)
    def _():
        loop(lambda blk: norm_rows(slot, blk))

    @pl.when(i >= nt)
    def _():
        o_ref[...] = jnp.zeros_like(o_ref)


def moe_dispatch(x, g, token_of_slot, n_tiles, tm):
    T, D = x.shape
    P = token_of_slot.shape[0]
    return pl.pallas_call(
        functools.partial(_dispatch_kernel, tm=tm),
        grid_spec=pltpu.PrefetchScalarGridSpec(
            num_scalar_prefetch=2,
            grid=(P // tm,),
            in_specs=[pl.BlockSpec(memory_space=pl.ANY),
                      pl.BlockSpec((1, D), lambda i, tok, nt: (0, 0))],
            out_specs=pl.BlockSpec((tm, D), lambda i, tok, nt: (i, 0)),
            scratch_shapes=[pltpu.VMEM((2, tm, D), F32), pltpu.SemaphoreType.DMA((2,))]),
        out_shape=jax.ShapeDtypeStruct((P, D), BF16),
        compiler_params=_cparams(1),
        name="moe_dispatch",
    )(token_of_slot, n_tiles, x, g.reshape(1, D))


def _weights_changed(te_ref, i):
    return (i == 0) | (te_ref[i] != te_ref[jnp.maximum(i - 1, 0)])


def _stage_copies(w_hbms, e, j, tn, stage_refs, sem_ref):
    col = pl.multiple_of(j * tn, tn)
    out = []
    for n, (w, s) in enumerate(zip(w_hbms, stage_refs)):
        rows = s.shape[0] // W_DMA_SPLIT
        for q in range(W_DMA_SPLIT):
            band = pl.ds(q * rows, rows)
            out.append(pltpu.make_async_copy(w.at[e, band, pl.ds(col, tn)], s.at[band, :],
                                             sem_ref.at[n]))
    return out


def _restage_weights(te_ref, nx_ref, w_hbms, stage_refs, wb_refs, sem_ref, *, tn, nj):
    j = pl.program_id(0)
    i = pl.program_id(1)

    @pl.when((j == 0) & (i == 0))
    def _():
        for c in _stage_copies(w_hbms, te_ref[0], 0, tn, stage_refs, sem_ref):
            c.start()

    @pl.when(_weights_changed(te_ref, i))
    def _():
        for c in _stage_copies(w_hbms, te_ref[i], j, tn, stage_refs, sem_ref):
            c.wait()
        for s, wb in zip(stage_refs, wb_refs):
            wb[...] = s[...].astype(BF16)
        nxt = nx_ref[i]
        last = nxt < 0
        e_next = jnp.where(last, te_ref[0], nxt)
        j_next = jnp.where(last, j + 1, j)

        @pl.when(j_next < nj)
        def _():
            for c in _stage_copies(w_hbms, e_next, j_next, tn, stage_refs, sem_ref):
                c.start()


def _for_used_rows(n_used, o_ref, compute):
    tm = o_ref.shape[0]
    classes = sorted({min(c, tm) for c in MOE_ROW_CLASSES} | {tm})
    lo = 0
    for r in classes:
        @pl.when((n_used > lo) & (n_used <= r))
        def _(r=r):
            o_ref[:r, :] = compute(r).astype(o_ref.dtype)
            if r < tm:
                o_ref[r:, :] = jnp.zeros((tm - r, o_ref.shape[1]), o_ref.dtype)
        lo = r

    @pl.when(n_used == 0)
    def _():
        o_ref[...] = jnp.zeros_like(o_ref)


def _moe_up_kernel(te_ref, nt_ref, nx_ref, nu_ref, a_ref, wg_hbm, wu_hbm, o_ref,
                   sg_ref, su_ref, wgb_ref, wub_ref, sem_ref, *, tn, nj):
    _restage_weights(te_ref, nx_ref, (wg_hbm, wu_hbm), (sg_ref, su_ref),
                     (wgb_ref, wub_ref), sem_ref, tn=tn, nj=nj)

    def compute(r):
        a = a_ref[:r, :]
        g = jnp.dot(a, wgb_ref[...], preferred_element_type=F32)
        u = jnp.dot(a, wub_ref[...], preferred_element_type=F32)
        return g * _sigmoid(g) * u

    _for_used_rows(nu_ref[pl.program_id(1)], o_ref, compute)


def _moe_down_kernel(te_ref, nt_ref, nx_ref, nu_ref, a_ref, w_hbm, o_ref,
                     s_ref, wb_ref, sem_ref, *, tn, nj):
    _restage_weights(te_ref, nx_ref, (w_hbm,), (s_ref,), (wb_ref,), sem_ref, tn=tn, nj=nj)
    _for_used_rows(nu_ref[pl.program_id(1)], o_ref,
                   lambda r: jnp.dot(a_ref[:r, :], wb_ref[...], preferred_element_type=F32))


def _moe_grouped(kernel_fn, name, a, weights, plan, tm, tn, out_dtype):
    te, n_tiles, nxt, n_used = plan
    P, K = a.shape
    N = weights[0].shape[2]
    tn = _tile(tn, N)
    nj = N // tn
    used = lambda i, nt: jnp.minimum(i, nt[0] - 1)
    n_w = len(weights)
    return pl.pallas_call(
        functools.partial(kernel_fn, tn=tn, nj=nj),
        grid_spec=pltpu.PrefetchScalarGridSpec(
            num_scalar_prefetch=4,
            grid=(nj, P // tm),
            in_specs=[pl.BlockSpec((tm, K), lambda j, i, te, nt, nx, nu: (used(i, nt), 0))]
                     + [pl.BlockSpec(memory_space=pl.ANY)] * n_w,
            out_specs=pl.BlockSpec((tm, tn), lambda j, i, te, nt, nx, nu: (i, j)),
            scratch_shapes=[pltpu.VMEM((K, tn), F32)] * n_w + [pltpu.VMEM((K, tn), BF16)] * n_w
                           + [pltpu.SemaphoreType.DMA((n_w,))]),
        out_shape=jax.ShapeDtypeStruct((P, N), out_dtype),
        compiler_params=_cparams(2),
        name=name,
    )(te, n_tiles, nxt, n_used, a, *weights)


def moe_up(xs, wg, wu, plan, tm, tn=512):
    return _moe_grouped(_moe_up_kernel, "moe_up", xs, (wg, wu), plan, tm, tn, BF16)


def moe_down(hs, wd, plan, tm, tn=1024):
    return _moe_grouped(_moe_down_kernel, "moe_down", hs, (wd,), plan, tm, tn, F32)


def _combine_kernel(slot_ref, x_ref, wt_ref, gp_ref, ys_hbm, xo_ref, buf_ref, sem_ref, *, tm):
    i = pl.program_id(0)
    n_blocks = tm // COMBINE_ROWS

    def start_rows(tile, s, blk):
        for q in range(COMBINE_ROWS):
            r = blk * COMBINE_ROWS + q
            for k in range(TOP_K):
                _row_copy(ys_hbm, slot_ref[(tile * tm + r) * TOP_K + k], buf_ref.at[s, k], r,
                          sem_ref.at[s]).start()

    def wait_rows(s, blk):
        for q in range(COMBINE_ROWS):
            for k in range(TOP_K):
                _row_copy(ys_hbm, 0, buf_ref.at[s, k], blk * COMBINE_ROWS + q,
                          sem_ref.at[s]).wait()

    def combine_rows(s, blk):
        rows = pl.ds(pl.multiple_of(blk * COMBINE_ROWS, COMBINE_ROWS), COMBINE_ROWS)
        y = (wt_ref[rows, 0:1] * buf_ref[s, 0, rows, :]
             + wt_ref[rows, 1:2] * buf_ref[s, 1, rows, :])
        xo_ref[rows, :] = x_ref[rows, :] + _rms(y, gp_ref[...])

    def loop(body):
        lax.fori_loop(0, n_blocks, lambda blk, c: (body(blk), c)[1], 0, unroll=COMBINE_UNROLL)

    @pl.when(i == 0)
    def _():
        loop(lambda blk: start_rows(0, 0, blk))

    s = i % 2
    loop(lambda blk: wait_rows(s, blk))

    @pl.when(i + 1 < pl.num_programs(0))
    def _():
        def body(blk):
            start_rows(i + 1, 1 - s, blk)
            combine_rows(s, blk)
        loop(body)

    @pl.when(i + 1 == pl.num_programs(0))
    def _():
        loop(lambda blk: combine_rows(s, blk))


def moe_combine_residual(x, ys, slot, wts, g_post, tm=256):
    M, D = x.shape
    tm = _tile(tm, M)
    row = pl.BlockSpec((tm, D), lambda i, sl: (i, 0))
    return pl.pallas_call(
        functools.partial(_combine_kernel, tm=tm),
        grid_spec=pltpu.PrefetchScalarGridSpec(
            num_scalar_prefetch=1,
            grid=(M // tm,),
            in_specs=[row, pl.BlockSpec((tm, LANES), lambda i, sl: (i, 0)),
                      pl.BlockSpec((1, D), lambda i, sl: (0, 0)),
                      pl.BlockSpec(memory_space=pl.ANY)],
            out_specs=row,
            scratch_shapes=[pltpu.VMEM((2, TOP_K, tm, D), F32), pltpu.SemaphoreType.DMA((2,))]),
        out_shape=jax.ShapeDtypeStruct((M, D), F32),
        compiler_params=_cparams(1), name="moe_combine_residual",
    )(slot, x, wts, g_post.reshape(1, D), ys)


def _moe_plan(idx, n_tokens, tm):
    e_flat = idx.reshape(-1)
    onehot = (e_flat[:, None] == jnp.arange(N_EXPERTS)[None, :]).astype(jnp.int32)
    counts = jnp.sum(onehot, axis=0)
    rank = jnp.sum((jnp.cumsum(onehot, axis=0) - onehot) * onehot, axis=1)
    tiles_per = (counts + tm - 1) // tm
    tile_end = jnp.cumsum(tiles_per)
    tile_start = tile_end - tiles_per
    slot = (tile_start[e_flat] * tm + rank).astype(jnp.int32)
    n_tiles_max = (n_tokens * TOP_K) // tm + N_EXPERTS
    n_tiles = tile_end[-1]
    tile_ids = jnp.minimum(jnp.arange(n_tiles_max), n_tiles - 1)
    expert_of = lambda t: jnp.minimum(
        jnp.sum((tile_end[None, :] <= t[:, None]).astype(jnp.int32), axis=1), N_EXPERTS - 1)
    te = expert_of(tile_ids)
    group_end = tile_end[te]
    nxt = jnp.where(group_end < n_tiles, expert_of(jnp.minimum(group_end, n_tiles - 1)), -1)
    all_ids = jnp.arange(n_tiles_max)
    n_used = jnp.where(all_ids < n_tiles,
                       jnp.clip(counts[te] - (all_ids - tile_start[te]) * tm, 0, tm), 0)
    token_of_pair = jnp.arange(n_tokens * TOP_K, dtype=jnp.int32) // TOP_K
    token_of_slot = jnp.zeros((n_tiles_max * tm,), jnp.int32).at[slot].set(token_of_pair)
    plan = (te.astype(jnp.int32), n_tiles.reshape(1).astype(jnp.int32), nxt.astype(jnp.int32),
            n_used.astype(jnp.int32))
    return slot, token_of_slot, plan


def kernel(x, mem, mem_norm, mem_kv, l0_norm_mix_pre, l0_mix_in, l0_sgu_ln_g, l0_sgu_ln_b, l0_sgu_w, l0_sgu_b, l0_mix_out, l0_norm_mix_post, l0_norm_x_pre, l0_xq, l0_xo, l0_norm_x_post, l0_norm_ffn_pre, l0_ffn_gate, l0_ffn_up, l0_ffn_down, l0_norm_ffn_post, l1_norm_mix_pre, l1_mix_in, l1_gate_b, l1_conv, l1_head_norm, l1_mix_out, l1_norm_mix_post, l1_norm_x_pre, l1_xq, l1_xo, l1_norm_x_post, l1_norm_ffn_pre, l1_router, l1_moe_gate, l1_moe_up, l1_moe_down, l1_norm_ffn_post):
    B, S, D = x.shape
    T = B * S
    n_mem = mem.shape[1]
    xf = x.reshape(T, D)

    memn = rmsnorm_rows(mem.reshape(B * n_mem, D), mem_norm)
    kv = matmul_fullk(memn, mem_kv, BF16).reshape(B, n_mem, 2 * D)

    hn = rmsnorm_rows(xf, l0_norm_mix_pre)
    z = matmul_fullk(hn, l0_mix_in, BF16, act="gelu")
    y = sgu_mix(z, l0_sgu_ln_g, l0_sgu_ln_b, l0_sgu_w, l0_sgu_b)
    y = matmul_fullk(y, l0_mix_out, BF16)
    xf, hn = residual_norm(xf, y, l0_norm_mix_post, l0_norm_x_pre)
    o = cross_attention(matmul_fullk(hn, l0_xq, BF16), kv, B)
    y = matmul_fullk(o, l0_xo, BF16)
    xf, hn = residual_norm(xf, y, l0_norm_x_post, l0_norm_ffn_pre)
    hmid = swiglu_up(hn, l0_ffn_gate, l0_ffn_up)
    y = matmul_tiledk(hmid, l0_ffn_down, BF16)
    xf, hn = residual_norm(xf, y, l0_norm_ffn_post, l1_norm_mix_pre)

    n_gates = 2 * MLSTM_HEADS
    n_main = l1_mix_in.shape[1] - n_gates
    w_in_t = l1_mix_in.T
    proj = matmul_fullk(hn, w_in_t, BF16, w_rows=(0, n_main))
    gates_t = mlstm_gates(hn, w_in_t[n_main:], l1_gate_b)
    y = mlstm_core(proj, gates_t, l1_conv, l1_head_norm, B)
    y = matmul_fullk(y, l1_mix_out, BF16)
    xf, hn = residual_norm(xf, y, l1_norm_mix_post, l1_norm_x_pre)
    o = cross_attention(matmul_fullk(hn, l1_xq, BF16), kv, B)
    y = matmul_fullk(o, l1_xo, BF16)
    xf, idx, wts = residual_router(xf, y, l1_norm_x_post, l1_norm_ffn_pre, l1_router)
    tm = min(MOE_TM, T)
    slot, token_of_slot, plan = _moe_plan(idx[:, :TOP_K], T, tm)
    xs = moe_dispatch(xf, l1_norm_ffn_pre, token_of_slot, plan[1], tm)
    hs = moe_up(xs, l1_moe_gate, l1_moe_up, plan, tm)
    ys = moe_down(hs, l1_moe_down, plan, tm)
    xf = moe_combine_residual(xf, ys, slot, wts, l1_norm_ffn_post)
    return xf.reshape(B, S, D)
```

```python
import functools

import jax
import jax.numpy as jnp
from jax import lax
from jax.experimental import pallas as pl
from jax.experimental.pallas import tpu as pltpu

F32 = jnp.float32
BF16 = jnp.bfloat16

EPS = 1e-6
CHUNK = 128
SGU_GROUPS = 8
MLSTM_HEADS = 8
CONV_WIDTH = 4
GATE_CAP = 15.0
X_HEADS = 4
N_EXPERTS = 8
TOP_K = 2
LANES = 128
MOE_TM = 1024
MOE_ROW_CLASSES = (128, 256, 512)
GATHER_ROWS = 16
GATHER_UNROLL = 2
COMBINE_ROWS = 8
COMBINE_UNROLL = 4
MLSTM_HEADS_PER_STEP = 4
CONV_TAIL = 8
VMEM_LIMIT = 56 * 1024 * 1024


def _cparams(n_axes, vmem=VMEM_LIMIT):
    return pltpu.CompilerParams(
        dimension_semantics=("arbitrary",) * n_axes, vmem_limit_bytes=vmem)


def _tile(pref, dim):
    t = min(pref, dim)
    assert dim % t == 0, (pref, dim)
    return t


def _rms(x, g):
    return x * lax.rsqrt(jnp.mean(x * x, axis=-1, keepdims=True) + EPS) * g


def _sigmoid(x):
    return 1.0 / (1.0 + jnp.exp(-x))


def _rms_kernel(x_ref, g_ref, o_ref):
    o_ref[...] = _rms(x_ref[...], g_ref[...]).astype(o_ref.dtype)


def rmsnorm_rows(x, g, tm=256):
    M, D = x.shape
    tm = _tile(tm, M)
    return pl.pallas_call(
        _rms_kernel,
        grid=(M // tm,),
        in_specs=[pl.BlockSpec((tm, D), lambda i: (i, 0)),
                  pl.BlockSpec((1, D), lambda i: (0, 0))],
        out_specs=pl.BlockSpec((tm, D), lambda i: (i, 0)),
        out_shape=jax.ShapeDtypeStruct((M, D), BF16),
        compiler_params=_cparams(1),
        name="rmsnorm_rows",
    )(x, g.reshape(1, D))


def _gelu(x):
    return 0.5 * x * (1.0 + lax.erf(x * 0.7071067811865476))


def _stream_weights(w_hbms, stage_refs, wb_refs, sem_ref, *, tn, nj, transposed, first=0):
    j = pl.program_id(0)
    i = pl.program_id(1)

    def copies(jj):
        off = pl.multiple_of(first + jj * tn, tn)
        return [pltpu.make_async_copy(
                    w.at[pl.ds(off, tn), :] if transposed else w.at[:, pl.ds(off, tn)],
                    s, sem_ref.at[n])
                for n, (w, s) in enumerate(zip(w_hbms, stage_refs))]

    @pl.when((j == 0) & (i == 0))
    def _():
        for c in copies(0):
            c.start()

    @pl.when(i == 0)
    def _():
        for c in copies(j):
            c.wait()
        for s, wb in zip(stage_refs, wb_refs):
            wb[...] = (s[...].T if transposed else s[...]).astype(BF16)

        @pl.when(j + 1 < nj)
        def _():
            for c in copies(j + 1):
                c.start()


def _mm_kernel(a_ref, w_hbm, o_ref, stage_ref, wb_ref, sem_ref, *, act, tn, nj, transposed,
               first):
    _stream_weights((w_hbm,), (stage_ref,), (wb_ref,), sem_ref,
                    tn=tn, nj=nj, transposed=transposed, first=first)
    acc = jnp.dot(a_ref[...], wb_ref[...], preferred_element_type=F32)
    if act == "gelu":
        acc = _gelu(acc)
    o_ref[...] = acc.astype(o_ref.dtype)


def matmul_fullk(a, w, out_dtype, act=None, w_rows=None, tm=1024, tn=1024):
    M, K = a.shape
    transposed = w_rows is not None
    first, N = w_rows if transposed else (0, w.shape[1])
    tm = _tile(tm, M)
    tn = _tile(tn, N)
    assert first % tn == 0
    nj = N // tn
    stage_block = (tn, K) if transposed else (K, tn)
    return pl.pallas_call(
        functools.partial(_mm_kernel, act=act, tn=tn, nj=nj, transposed=transposed,
                          first=first),
        grid=(nj, M // tm),
        in_specs=[pl.BlockSpec((tm, K), lambda j, i: (i, 0)),
                  pl.BlockSpec(memory_space=pl.ANY)],
        out_specs=pl.BlockSpec((tm, tn), lambda j, i: (i, j)),
        out_shape=jax.ShapeDtypeStruct((M, N), out_dtype),
        scratch_shapes=[pltpu.VMEM(stage_block, F32), pltpu.VMEM((K, tn), BF16),
                        pltpu.SemaphoreType.DMA((1,))],
        compiler_params=_cparams(2),
        name="matmul_fullk" + ("_" + act if act else "") + ("_t" if transposed else ""),
    )(a, w)


def _swiglu_kernel(a_ref, wg_hbm, wu_hbm, o_ref, sg_ref, su_ref, wb_ref, sem_ref, *, tn, nj):
    _stream_weights((wg_hbm, wu_hbm), (sg_ref, su_ref),
                    (wb_ref.at[:, :tn], wb_ref.at[:, tn:]), sem_ref,
                    tn=tn, nj=nj, transposed=False)
    gu = jnp.dot(a_ref[...], wb_ref[...], preferred_element_type=F32)
    g = gu[:, :tn]
    o_ref[...] = (g * _sigmoid(g) * gu[:, tn:]).astype(o_ref.dtype)


def swiglu_up(a, wg, wu, tm=1024, tn=512):
    M, K = a.shape
    F = wg.shape[1]
    tm = _tile(tm, M)
    tn = _tile(tn, F)
    nj = F // tn
    hbm = pl.BlockSpec(memory_space=pl.ANY)
    return pl.pallas_call(
        functools.partial(_swiglu_kernel, tn=tn, nj=nj),
        grid=(nj, M // tm),
        in_specs=[pl.BlockSpec((tm, K), lambda j, i: (i, 0)), hbm, hbm],
        out_specs=pl.BlockSpec((tm, tn), lambda j, i: (i, j)),
        out_shape=jax.ShapeDtypeStruct((M, F), BF16),
        scratch_shapes=[pltpu.VMEM((K, tn), F32), pltpu.VMEM((K, tn), F32),
                        pltpu.VMEM((K, 2 * tn), BF16), pltpu.SemaphoreType.DMA((2,))],
        compiler_params=_cparams(2),
        name="swiglu_up",
    )(a, wg, wu)


def _mmk_kernel(a_ref, w_ref, o_ref, acc_ref, *, nk):
    k = pl.program_id(2)

    @pl.when(k == 0)
    def _():
        acc_ref[...] = jnp.zeros_like(acc_ref)

    acc_ref[...] += jnp.dot(a_ref[...], w_ref[...].astype(BF16),
                            preferred_element_type=F32)

    @pl.when(k == nk - 1)
    def _():
        o_ref[...] = acc_ref[...].astype(o_ref.dtype)


def matmul_tiledk(a, w, out_dtype, tm=2048, tn=1024, tk=1024):
    M, K = a.shape
    N = w.shape[1]
    tm, tn, tk = _tile(tm, M), _tile(tn, N), _tile(tk, K)
    nk = K // tk
    return pl.pallas_call(
        functools.partial(_mmk_kernel, nk=nk),
        grid=(N // tn, M // tm, nk),
        in_specs=[pl.BlockSpec((tm, tk), lambda j, i, k: (i, k)),
                  pl.BlockSpec((tk, tn), lambda j, i, k: (k, j))],
        out_specs=pl.BlockSpec((tm, tn), lambda j, i, k: (i, j)),
        out_shape=jax.ShapeDtypeStruct((M, N), out_dtype),
        scratch_shapes=[pltpu.VMEM((tm, tn), F32)],
        compiler_params=_cparams(3),
        name="matmul_tiledk",
    )(a, w)


def _res_kernel(x_ref, y_ref, gp_ref, gn_ref, xo_ref, hn_ref):
    xn = x_ref[...] + _rms(y_ref[...].astype(F32), gp_ref[...])
    xo_ref[...] = xn
    hn_ref[...] = _rms(xn, gn_ref[...]).astype(hn_ref.dtype)


def residual_norm(x, y, g_post, g_next, tm=256):
    M, D = x.shape
    tm = _tile(tm, M)
    row = pl.BlockSpec((tm, D), lambda i: (i, 0))
    vec = pl.BlockSpec((1, D), lambda i: (0, 0))
    return pl.pallas_call(
        _res_kernel, grid=(M // tm,),
        in_specs=[row, row, vec, vec], out_specs=[row, row],
        out_shape=[jax.ShapeDtypeStruct((M, D), F32), jax.ShapeDtypeStruct((M, D), BF16)],
        compiler_params=_cparams(1), name="residual_norm",
    )(x, y, g_post.reshape(1, D), g_next.reshape(1, D))


def _sgu_kernel(u_ref, v_ref, lg_ref, lb_ref, w_ref, bt_ref, o_ref, wm_ref, *, groups):
    @pl.when(pl.program_id(0) == 0)
    def _():
        t = lax.broadcasted_iota(jnp.int32, (CHUNK, CHUNK), 0)
        s = lax.broadcasted_iota(jnp.int32, (CHUNK, CHUNK), 1)
        for g in range(groups):
            wm_ref[g] = jnp.where(t >= s, w_ref[g], 0.0).astype(BF16)

    v = v_ref[...].astype(F32)
    vc = v - jnp.mean(v, axis=-1, keepdims=True)
    vn = vc * lax.rsqrt(jnp.mean(vc * vc, axis=-1, keepdims=True) + EPS)
    vn = (vn * lg_ref[...] + lb_ref[...]).astype(BF16)
    tm, width = vn.shape
    gd = width // groups
    for c in range(tm // CHUNK):
        rows = slice(c * CHUNK, (c + 1) * CHUNK)
        for g in range(groups):
            cols = slice(g * gd, (g + 1) * gd)
            mixed = jnp.dot(wm_ref[g], vn[rows, cols], preferred_element_type=F32)
            mixed = mixed + bt_ref[:, g:g + 1]
            o_ref[rows, cols] = (u_ref[rows, cols].astype(F32) * mixed).astype(o_ref.dtype)


def sgu_mix(z, ln_g, ln_b, sgu_w, sgu_b, tm=256):
    M, W2 = z.shape
    W = W2 // 2
    G = sgu_w.shape[0]
    tm = _tile(tm, M)
    return pl.pallas_call(
        functools.partial(_sgu_kernel, groups=G),
        grid=(M // tm,),
        in_specs=[pl.BlockSpec((tm, W), lambda i: (i, 0)),
                  pl.BlockSpec((tm, W), lambda i: (i, 1)),
                  pl.BlockSpec((1, W), lambda i: (0, 0)),
                  pl.BlockSpec((1, W), lambda i: (0, 0)),
                  pl.BlockSpec((G, CHUNK, CHUNK), lambda i: (0, 0, 0)),
                  pl.BlockSpec((CHUNK, G), lambda i: (0, 0))],
        out_specs=pl.BlockSpec((tm, W), lambda i: (i, 0)),
        out_shape=jax.ShapeDtypeStruct((M, W), BF16),
        scratch_shapes=[pltpu.VMEM((G, CHUNK, CHUNK), BF16)],
        compiler_params=_cparams(1),
        name="sgu_mix",
    )(z, z, ln_g.reshape(1, W), ln_b.reshape(1, W), sgu_w, sgu_b.T)


def _xattn_kernel(q_ref, k_ref, v_ref, o_ref, *, heads, scale):
    hd = q_ref.shape[-1] // heads
    for h in range(heads):
        cols = slice(h * hd, (h + 1) * hd)
        s = lax.dot_general(q_ref[:, cols], k_ref[0, :, cols], (((1,), (1,)), ((), ())),
                            preferred_element_type=F32) * scale
        p = jnp.exp(s - jnp.max(s, axis=-1, keepdims=True))
        p = p / jnp.sum(p, axis=-1, keepdims=True)
        o = jnp.dot(p.astype(BF16), v_ref[0, :, cols], preferred_element_type=F32)
        o_ref[:, cols] = o.astype(o_ref.dtype)


def cross_attention(q, kv, batch, tm=512):
    M, D = q.shape
    S = M // batch
    n_mem = kv.shape[1]
    tm = _tile(tm, S)
    spt = S // tm
    return pl.pallas_call(
        functools.partial(_xattn_kernel, heads=X_HEADS, scale=(D // X_HEADS) ** -0.5),
        grid=(batch, spt),
        in_specs=[pl.BlockSpec((tm, D), lambda b, m: (b * spt + m, 0)),
                  pl.BlockSpec((1, n_mem, D), lambda b, m: (b, 0, 0)),
                  pl.BlockSpec((1, n_mem, D), lambda b, m: (b, 0, 1))],
        out_specs=pl.BlockSpec((tm, D), lambda b, m: (b * spt + m, 0)),
        out_shape=jax.ShapeDtypeStruct((M, D), BF16),
        compiler_params=_cparams(2),
        name="cross_attention",
    )(q, kv, kv)


def _gates_kernel(hn_ref, w_ref, b_ref, o_ref):
    acc = lax.dot_general(w_ref[...].astype(BF16), hn_ref[...], (((1,), (1,)), ((), ())),
                          preferred_element_type=F32)
    o_ref[...] = GATE_CAP * jnp.tanh((acc + b_ref[...]) / GATE_CAP)


def mlstm_gates(hn, w_gates_t, gate_b, tm=512):
    M, D = hn.shape
    n_gates = w_gates_t.shape[0]
    tm = _tile(tm, M)
    return pl.pallas_call(
        _gates_kernel,
        grid=(M // tm,),
        in_specs=[pl.BlockSpec((tm, D), lambda i: (i, 0)),
                  pl.BlockSpec((n_gates, D), lambda i: (0, 0)),
                  pl.BlockSpec((n_gates, 1), lambda i: (0, 0))],
        out_specs=pl.BlockSpec((n_gates, tm), lambda i: (0, i)),
        out_shape=jax.ShapeDtypeStruct((n_gates, M), F32),
        compiler_params=_cparams(1),
        name="mlstm_gates",
    )(hn, w_gates_t, gate_b.reshape(n_gates, 1))


def _conv_silu(x, tail_ref, w):
    prev = tail_ref[...]
    rid = lax.broadcasted_iota(jnp.int32, prev.shape, 0)
    acc = x * w[CONV_WIDTH - 1:CONV_WIDTH, :]
    for r in range(1, CONV_WIDTH):
        rolled = pltpu.roll(x, r, 0)
        head = jnp.where(rid < r, pltpu.roll(prev, r, 0), rolled[:CONV_TAIL])
        shifted = jnp.concatenate([head, rolled[CONV_TAIL:]], axis=0)
        acc = acc + shifted * w[CONV_WIDTH - 1 - r:CONV_WIDTH - r, :]
    tail_ref[...] = x[x.shape[0] - CONV_TAIL:, :]
    return acc * _sigmoid(acc)


def _mlstm_kernel(q_ref, k_ref, v_ref, o_ref, ig_ref, fg_ref, cq_ref, ck_ref, hg_ref,
                  y_ref, s_ref, n_ref, m_ref, qt_ref, kt_ref, *, heads):
    @pl.when(pl.program_id(2) == 0)
    def _():
        s_ref[...] = jnp.zeros_like(s_ref)
        n_ref[...] = jnp.zeros_like(n_ref)
        m_ref[...] = jnp.zeros_like(m_ref)
        qt_ref[...] = jnp.zeros_like(qt_ref)
        kt_ref[...] = jnp.zeros_like(kt_ref)

    dk = q_ref.shape[-1] // heads
    dv = v_ref.shape[-1] // heads
    for h in range(heads):
        qc = slice(h * dk, (h + 1) * dk)
        vc = slice(h * dv, (h + 1) * dv)
        q = _conv_silu(q_ref[:, qc].astype(F32), qt_ref.at[h], cq_ref[:, qc]) * dk ** -0.5
        k = _conv_silu(k_ref[:, qc].astype(F32), kt_ref.at[h], ck_ref[:, qc])
        _mlstm_head(q, k, v_ref[:, vc], o_ref[:, vc], ig_ref[h], fg_ref[h], hg_ref[:, vc],
                    y_ref.at[:, vc], s_ref.at[h], n_ref.at[h], m_ref.at[h])


def _mlstm_head(q, k, vb, ob, i_row, f_row, hg, y_ref, s_ref, n_ref, m_ref):
    L = CHUNK
    qb = q.astype(BF16)
    kb = k.astype(BF16)
    v = vb.astype(F32)

    t_id = lax.broadcasted_iota(jnp.int32, (L, L), 0)
    s_id = lax.broadcasted_iota(jnp.int32, (L, L), 1)
    eye = t_id == s_id
    causal = t_id >= s_id

    logf_row = -(jnp.maximum(-f_row, 0.0) + jnp.log1p(jnp.exp(-jnp.abs(f_row))))
    logf_b = jnp.broadcast_to(logf_row, (L, L))
    i_b = jnp.broadcast_to(i_row, (L, L))
    logf_col = jnp.sum(jnp.where(eye, logf_b, 0.0), axis=1, keepdims=True)
    i_col = jnp.sum(jnp.where(eye, i_b, 0.0), axis=1, keepdims=True)
    b_col = jnp.sum(jnp.where(causal, logf_b, 0.0), axis=1, keepdims=True)
    b_row = jnp.sum(jnp.where(t_id <= s_id, jnp.broadcast_to(logf_col, (L, L)), 0.0),
                    axis=0, keepdims=True)
    g_tot = jnp.sum(logf_row, axis=1, keepdims=True)
    m0 = m_ref[:, 0:1]

    d_log = jnp.where(causal, b_col - b_row + i_row, -jnp.inf)
    m_inter = b_col + m0
    m_t = jnp.maximum(m_inter, jnp.max(d_log, axis=1, keepdims=True))
    qk = lax.dot_general(qb, kb, (((1,), (1,)), ((), ())), preferred_element_type=F32)
    scores = qk * jnp.exp(d_log - m_t)
    inter = jnp.exp(m_inter - m_t)
    num = (jnp.dot(scores.astype(BF16), vb, preferred_element_type=F32)
           + inter * jnp.dot(qb, s_ref[...].astype(BF16), preferred_element_type=F32))
    den = (jnp.sum(scores, axis=1, keepdims=True)
           + inter * jnp.sum(q * n_ref[...], axis=1, keepdims=True))
    h = num / jnp.maximum(jnp.abs(den), jnp.exp(-m_t))
    y_ref[...] = (_sigmoid(ob.astype(F32)) * _rms(h, hg)).astype(y_ref.dtype)

    w_log = g_tot - b_col + i_col
    a = jnp.max(w_log, axis=0, keepdims=True)
    w = jnp.exp(w_log - a)
    s_loc = lax.dot_general(kb, (w * v).astype(BF16), (((0,), (0,)), ((), ())),
                            preferred_element_type=F32)
    n_loc = jnp.sum(w * k, axis=0, keepdims=True)
    m_new = jnp.maximum(g_tot + m0, a)
    sc_prev = jnp.exp(g_tot + m0 - m_new)
    sc_loc = jnp.exp(a - m_new)
    s_ref[...] = sc_prev * s_ref[...] + sc_loc * s_loc
    n_ref[...] = sc_prev * n_ref[...] + sc_loc * n_loc
    m_ref[...] = jnp.broadcast_to(m_new, m_ref.shape)


def mlstm_core(proj, gates_t, conv_w, head_g, batch):
    M = proj.shape[0]
    H = MLSTM_HEADS
    G = MLSTM_HEADS_PER_STEP
    assert H % G == 0
    ng = H // G
    qkw = conv_w.shape[1]
    dk = qkw // (2 * H)
    dv = head_g.shape[0] // H
    nc = M // batch // CHUNK
    assert qkw % (G * dv) == 0
    v0 = qkw // (G * dv)
    gates4 = gates_t.reshape(2 * ng, G, 1, M)
    row = lambda b, c: b * nc + c
    return pl.pallas_call(
        functools.partial(_mlstm_kernel, heads=G),
        grid=(batch, ng, nc),
        in_specs=[pl.BlockSpec((CHUNK, G * dk), lambda b, g, c: (row(b, c), g)),
                  pl.BlockSpec((CHUNK, G * dk), lambda b, g, c: (row(b, c), ng + g)),
                  pl.BlockSpec((CHUNK, G * dv), lambda b, g, c: (row(b, c), v0 + g)),
                  pl.BlockSpec((CHUNK, G * dv), lambda b, g, c: (row(b, c), v0 + ng + g)),
                  pl.BlockSpec((None, G, 1, CHUNK), lambda b, g, c: (g, 0, 0, row(b, c))),
                  pl.BlockSpec((None, G, 1, CHUNK), lambda b, g, c: (ng + g, 0, 0, row(b, c))),
                  pl.BlockSpec((CONV_WIDTH, G * dk), lambda b, g, c: (0, g)),
                  pl.BlockSpec((CONV_WIDTH, G * dk), lambda b, g, c: (0, ng + g)),
                  pl.BlockSpec((1, G * dv), lambda b, g, c: (0, g))],
        out_specs=pl.BlockSpec((CHUNK, G * dv), lambda b, g, c: (row(b, c), g)),
        out_shape=jax.ShapeDtypeStruct((M, H * dv), BF16),
        scratch_shapes=[pltpu.VMEM((G, dk, dv), F32), pltpu.VMEM((G, 1, dk), F32),
                        pltpu.VMEM((G, 1, LANES), F32),
                        pltpu.VMEM((G, CONV_TAIL, dk), F32), pltpu.VMEM((G, CONV_TAIL, dk), F32)],
        compiler_params=_cparams(3),
        name="mlstm_core",
    )(proj, proj, proj, proj, gates4, gates4, conv_w, conv_w, head_g.reshape(1, H * dv))


def _res_router_kernel(x_ref, y_ref, gp_ref, gn_ref, w2_ref, xo_ref, idx_ref, wt_ref):
    xn = x_ref[...] + _rms(y_ref[...].astype(F32), gp_ref[...])
    xo_ref[...] = xn
    hn = _rms(xn, gn_ref[...])
    hi = hn.astype(BF16)
    lo = (hn - hi.astype(F32)).astype(BF16)
    t = jnp.dot(hi, w2_ref[...], preferred_element_type=F32)
    logits = (t[:, :LANES] + t[:, LANES:]
              + jnp.dot(lo, w2_ref[:, :LANES], preferred_element_type=F32))
    lane = lax.broadcasted_iota(jnp.int32, logits.shape, 1)
    neg = -jnp.inf
    l1 = jnp.where(lane < N_EXPERTS, logits, neg)
    m1 = jnp.max(l1, axis=1, keepdims=True)
    i1 = jnp.min(jnp.where(l1 == m1, lane, LANES), axis=1, keepdims=True)
    l2 = jnp.where(lane == i1, neg, l1)
    m2 = jnp.max(l2, axis=1, keepdims=True)
    i2 = jnp.min(jnp.where(l2 == m2, lane, LANES), axis=1, keepdims=True)
    r = jnp.exp(m2 - m1)
    w1 = 1.0 / (1.0 + r)
    w2 = r / (1.0 + r)
    idx_ref[...] = jnp.where(lane == 0, i1, jnp.where(lane == 1, i2, 0))
    wt_ref[...] = jnp.where(lane == 0, w1, jnp.where(lane == 1, w2, 0.0))


def residual_router(x, y, g_post, g_next, w_router, tm=256):
    M, D = x.shape
    tm = _tile(tm, M)
    w_pad = jnp.pad(w_router, ((0, 0), (0, LANES - w_router.shape[1])))
    w_hi = w_pad.astype(BF16)
    w_lo = (w_pad - w_hi.astype(F32)).astype(BF16)
    row = pl.BlockSpec((tm, D), lambda i: (i, 0))
    vec = pl.BlockSpec((1, D), lambda i: (0, 0))
    out = pl.BlockSpec((tm, LANES), lambda i: (i, 0))
    return pl.pallas_call(
        _res_router_kernel, grid=(M // tm,),
        in_specs=[row, row, vec, vec, pl.BlockSpec((D, 2 * LANES), lambda i: (0, 0))],
        out_specs=[row, out, out],
        out_shape=[jax.ShapeDtypeStruct((M, D), F32),
                   jax.ShapeDtypeStruct((M, LANES), jnp.int32),
                   jax.ShapeDtypeStruct((M, LANES), F32)],
        compiler_params=_cparams(1), name="residual_router",
    )(x, y, g_post.reshape(1, D), g_next.reshape(1, D),
      jnp.concatenate([w_hi, w_lo], axis=1))


def _row_copy(src_hbm, src_row, dst_ref, dst_row, sem):
    return pltpu.make_async_copy(src_hbm.at[pl.ds(src_row, 1)],
                                 dst_ref.at[pl.ds(dst_row, 1)], sem)


def _dispatch_kernel(tok_ref, nt_ref, x_hbm, g_ref, o_ref, buf_ref, sem_ref, *, tm):
    i = pl.program_id(0)
    nt = nt_ref[0]
    n_blocks = tm // GATHER_ROWS

    def start_rows(tile, slot, blk):
        for q in range(GATHER_ROWS):
            r = blk * GATHER_ROWS + q
            _row_copy(x_hbm, tok_ref[tile * tm + r], buf_ref.at[slot], r,
                      sem_ref.at[slot]).start()

    def wait_rows(slot, blk):
        for q in range(GATHER_ROWS):
            _row_copy(x_hbm, 0, buf_ref.at[slot], blk * GATHER_ROWS + q,
                      sem_ref.at[slot]).wait()

    def norm_rows(slot, blk):
        rows = pl.ds(pl.multiple_of(blk * GATHER_ROWS, GATHER_ROWS), GATHER_ROWS)
        o_ref[rows, :] = _rms(buf_ref[slot, rows, :], g_ref[...]).astype(o_ref.dtype)

    def loop(body):
        lax.fori_loop(0, n_blocks, lambda blk, c: (body(blk), c)[1], 0, unroll=GATHER_UNROLL)

    @pl.when(i == 0)
    def _():
        loop(lambda blk: start_rows(0, 0, blk))

    slot = i % 2

    @pl.when(i < nt)
    def _():
        loop(lambda blk: wait_rows(slot, blk))

    @pl.when(i + 1 < nt)
    def _():
        def body(blk):
            start_rows(i + 1, 1 - slot, blk)
            norm_rows(slot, blk)
        loop(body)

    @pl.when(i + 1 == nt)
    def _():
        loop(lambda blk: norm_rows(slot, blk))

    @pl.when(i >= nt)
    def _():
        o_ref[...] = jnp.zeros_like(o_ref)


def moe_dispatch(x, g, token_of_slot, n_tiles, tm):
    T, D = x.shape
    P = token_of_slot.shape[0]
    return pl.pallas_call(
        functools.partial(_dispatch_kernel, tm=tm),
        grid_spec=pltpu.PrefetchScalarGridSpec(
            num_scalar_prefetch=2,
            grid=(P // tm,),
            in_specs=[pl.BlockSpec(memory_space=pl.ANY),
                      pl.BlockSpec((1, D), lambda i, tok, nt: (0, 0))],
            out_specs=pl.BlockSpec((tm, D), lambda i, tok, nt: (i, 0)),
            scratch_shapes=[pltpu.VMEM((2, tm, D), F32), pltpu.SemaphoreType.DMA((2,))]),
        out_shape=jax.ShapeDtypeStruct((P, D), BF16),
        compiler_params=_cparams(1),
        name="moe_dispatch",
    )(token_of_slot, n_tiles, x, g.reshape(1, D))


def _weights_changed(te_ref, i):
    return (i == 0) | (te_ref[i] != te_ref[jnp.maximum(i - 1, 0)])


def _stage_copies(w_hbms, e, j, tn, stage_refs, sem_ref):
    col = pl.multiple_of(j * tn, tn)
    return [pltpu.make_async_copy(w.at[e, :, pl.ds(col, tn)], s, sem_ref.at[n])
            for n, (w, s) in enumerate(zip(w_hbms, stage_refs))]


def _restage_weights(te_ref, nx_ref, w_hbms, stage_refs, wb_refs, sem_ref, *, tn, nj):
    j = pl.program_id(0)
    i = pl.program_id(1)

    @pl.when((j == 0) & (i == 0))
    def _():
        for c in _stage_copies(w_hbms, te_ref[0], 0, tn, stage_refs, sem_ref):
            c.start()

    @pl.when(_weights_changed(te_ref, i))
    def _():
        for c in _stage_copies(w_hbms, te_ref[i], j, tn, stage_refs, sem_ref):
            c.wait()
        for s, wb in zip(stage_refs, wb_refs):
            wb[...] = s[...].astype(BF16)
        nxt = nx_ref[i]
        last = nxt < 0
        e_next = jnp.where(last, te_ref[0], nxt)
        j_next = jnp.where(last, j + 1, j)

        @pl.when(j_next < nj)
        def _():
            for c in _stage_copies(w_hbms, e_next, j_next, tn, stage_refs, sem_ref):
                c.start()


def _for_used_rows(n_used, o_ref, compute):
    tm = o_ref.shape[0]
    classes = sorted({min(c, tm) for c in MOE_ROW_CLASSES} | {tm})
    lo = 0
    for r in classes:
        @pl.when((n_used > lo) & (n_used <= r))
        def _(r=r):
            o_ref[:r, :] = compute(r).astype(o_ref.dtype)
            if r < tm:
                o_ref[r:, :] = jnp.zeros((tm - r, o_ref.shape[1]), o_ref.dtype)
        lo = r

    @pl.when(n_used == 0)
    def _():
        o_ref[...] = jnp.zeros_like(o_ref)


def _moe_up_kernel(te_ref, nt_ref, nx_ref, nu_ref, a_ref, wg_hbm, wu_hbm, o_ref,
                   sg_ref, su_ref, wgb_ref, wub_ref, sem_ref, *, tn, nj):
    _restage_weights(te_ref, nx_ref, (wg_hbm, wu_hbm), (sg_ref, su_ref),
                     (wgb_ref, wub_ref), sem_ref, tn=tn, nj=nj)

    def compute(r):
        a = a_ref[:r, :]
        g = jnp.dot(a, wgb_ref[...], preferred_element_type=F32)
        u = jnp.dot(a, wub_ref[...], preferred_element_type=F32)
        return g * _sigmoid(g) * u

    _for_used_rows(nu_ref[pl.program_id(1)], o_ref, compute)


def _moe_down_kernel(te_ref, nt_ref, nx_ref, nu_ref, a_ref, w_hbm, o_ref,
                     s_ref, wb_ref, sem_ref, *, tn, nj):
    _restage_weights(te_ref, nx_ref, (w_hbm,), (s_ref,), (wb_ref,), sem_ref, tn=tn, nj=nj)
    _for_used_rows(nu_ref[pl.program_id(1)], o_ref,
                   lambda r: jnp.dot(a_ref[:r, :], wb_ref[...], preferred_element_type=F32))


def _moe_grouped(kernel_fn, name, a, weights, plan, tm, tn, out_dtype):
    te, n_tiles, nxt, n_used = plan
    P, K = a.shape
    N = weights[0].shape[2]
    tn = _tile(tn, N)
    nj = N // tn
    used = lambda i, nt: jnp.minimum(i, nt[0] - 1)
    n_w = len(weights)
    return pl.pallas_call(
        functools.partial(kernel_fn, tn=tn, nj=nj),
        grid_spec=pltpu.PrefetchScalarGridSpec(
            num_scalar_prefetch=4,
            grid=(nj, P // tm),
            in_specs=[pl.BlockSpec((tm, K), lambda j, i, te, nt, nx, nu: (used(i, nt), 0))]
                     + [pl.BlockSpec(memory_space=pl.ANY)] * n_w,
            out_specs=pl.BlockSpec((tm, tn), lambda j, i, te, nt, nx, nu: (i, j)),
            scratch_shapes=[pltpu.VMEM((K, tn), F32)] * n_w + [pltpu.VMEM((K, tn), BF16)] * n_w
                           + [pltpu.SemaphoreType.DMA((n_w,))]),
        out_shape=jax.ShapeDtypeStruct((P, N), out_dtype),
        compiler_params=_cparams(2),
        name=name,
    )(te, n_tiles, nxt, n_used, a, *weights)


def moe_up(xs, wg, wu, plan, tm, tn=512):
    return _moe_grouped(_moe_up_kernel, "moe_up", xs, (wg, wu), plan, tm, tn, BF16)


def moe_down(hs, wd, plan, tm, tn=1024):
    return _moe_grouped(_moe_down_kernel, "moe_down", hs, (wd,), plan, tm, tn, F32)


def _combine_kernel(slot_ref, x_ref, wt_ref, gp_ref, ys_hbm, xo_ref, buf_ref, sem_ref, *, tm):
    i = pl.program_id(0)
    n_blocks = tm // COMBINE_ROWS

    def start_rows(tile, s, blk):
        for q in range(COMBINE_ROWS):
            r = blk * COMBINE_ROWS + q
            for k in range(TOP_K):
                _row_copy(ys_hbm, slot_ref[(tile * tm + r) * TOP_K + k], buf_ref.at[s, k], r,
                          sem_ref.at[s]).start()

    def wait_rows(s, blk):
        for q in range(COMBINE_ROWS):
            for k in range(TOP_K):
                _row_copy(ys_hbm, 0, buf_ref.at[s, k], blk * COMBINE_ROWS + q,
                          sem_ref.at[s]).wait()

    def combine_rows(s, blk):
        rows = pl.ds(pl.multiple_of(blk * COMBINE_ROWS, COMBINE_ROWS), COMBINE_ROWS)
        y = (wt_ref[rows, 0:1] * buf_ref[s, 0, rows, :]
             + wt_ref[rows, 1:2] * buf_ref[s, 1, rows, :])
        xo_ref[rows, :] = x_ref[rows, :] + _rms(y, gp_ref[...])

    def loop(body):
        lax.fori_loop(0, n_blocks, lambda blk, c: (body(blk), c)[1], 0, unroll=COMBINE_UNROLL)

    @pl.when(i == 0)
    def _():
        loop(lambda blk: start_rows(0, 0, blk))

    s = i % 2
    loop(lambda blk: wait_rows(s, blk))

    @pl.when(i + 1 < pl.num_programs(0))
    def _():
        def body(blk):
            start_rows(i + 1, 1 - s, blk)
            combine_rows(s, blk)
        loop(body)

    @pl.when(i + 1 == pl.num_programs(0))
    def _():
        loop(lambda blk: combine_rows(s, blk))


def moe_combine_residual(x, ys, slot, wts, g_post, tm=256):
    M, D = x.shape
    tm = _tile(tm, M)
    row = pl.BlockSpec((tm, D), lambda i, sl: (i, 0))
    return pl.pallas_call(
        functools.partial(_combine_kernel, tm=tm),
        grid_spec=pltpu.PrefetchScalarGridSpec(
            num_scalar_prefetch=1,
            grid=(M // tm,),
            in_specs=[row, pl.BlockSpec((tm, LANES), lambda i, sl: (i, 0)),
                      pl.BlockSpec((1, D), lambda i, sl: (0, 0)),
                      pl.BlockSpec(memory_space=pl.ANY)],
            out_specs=row,
            scratch_shapes=[pltpu.VMEM((2, TOP_K, tm, D), F32), pltpu.SemaphoreType.DMA((2,))]),
        out_shape=jax.ShapeDtypeStruct((M, D), F32),
        compiler_params=_cparams(1), name="moe_combine_residual",
    )(slot, x, wts, g_post.reshape(1, D), ys)


def _moe_plan(idx, n_tokens, tm):
    e_flat = idx.reshape(-1)
    onehot = (e_flat[:, None] == jnp.arange(N_EXPERTS)[None, :]).astype(jnp.int32)
    counts = jnp.sum(onehot, axis=0)
    rank = jnp.sum((jnp.cumsum(onehot, axis=0) - onehot) * onehot, axis=1)
    tiles_per = (counts + tm - 1) // tm
    tile_end = jnp.cumsum(tiles_per)
    tile_start = tile_end - tiles_per
    slot = (tile_start[e_flat] * tm + rank).astype(jnp.int32)
    n_tiles_max = (n_tokens * TOP_K) // tm + N_EXPERTS
    n_tiles = tile_end[-1]
    tile_ids = jnp.minimum(jnp.arange(n_tiles_max), n_tiles - 1)
    expert_of = lambda t: jnp.minimum(
        jnp.sum((tile_end[None, :] <= t[:, None]).astype(jnp.int32), axis=1), N_EXPERTS - 1)
    te = expert_of(tile_ids)
    group_end = tile_end[te]
    nxt = jnp.where(group_end < n_tiles, expert_of(jnp.minimum(group_end, n_tiles - 1)), -1)
    all_ids = jnp.arange(n_tiles_max)
    n_used = jnp.where(all_ids < n_tiles,
                       jnp.clip(counts[te] - (all_ids - tile_start[te]) * tm, 0, tm), 0)
    token_of_pair = jnp.arange(n_tokens * TOP_K, dtype=jnp.int32) // TOP_K
    token_of_slot = jnp.zeros((n_tiles_max * tm,), jnp.int32).at[slot].set(token_of_pair)
    plan = (te.astype(jnp.int32), n_tiles.reshape(1).astype(jnp.int32), nxt.astype(jnp.int32),
            n_used.astype(jnp.int32))
    return slot, token_of_slot, plan


def kernel(x, mem, mem_norm, mem_kv, l0_norm_mix_pre, l0_mix_in, l0_sgu_ln_g, l0_sgu_ln_b, l0_sgu_w, l0_sgu_b, l0_mix_out, l0_norm_mix_post, l0_norm_x_pre, l0_xq, l0_xo, l0_norm_x_post, l0_norm_ffn_pre, l0_ffn_gate, l0_ffn_up, l0_ffn_down, l0_norm_ffn_post, l1_norm_mix_pre, l1_mix_in, l1_gate_b, l1_conv, l1_head_norm, l1_mix_out, l1_norm_mix_post, l1_norm_x_pre, l1_xq, l1_xo, l1_norm_x_post, l1_norm_ffn_pre, l1_router, l1_moe_gate, l1_moe_up, l1_moe_down, l1_norm_ffn_post):
    B, S, D = x.shape
    T = B * S
    n_mem = mem.shape[1]
    xf = x.reshape(T, D)

    memn = rmsnorm_rows(mem.reshape(B * n_mem, D), mem_norm)
    kv = matmul_fullk(memn, mem_kv, BF16).reshape(B, n_mem, 2 * D)

    hn = rmsnorm_rows(xf, l0_norm_mix_pre)
    z = matmul_fullk(hn, l0_mix_in, BF16, act="gelu")
    y = sgu_mix(z, l0_sgu_ln_g, l0_sgu_ln_b, l0_sgu_w, l0_sgu_b)
    y = matmul_fullk(y, l0_mix_out, BF16)
    xf, hn = residual_norm(xf, y, l0_norm_mix_post, l0_norm_x_pre)
    o = cross_attention(matmul_fullk(hn, l0_xq, BF16), kv, B)
    y = matmul_fullk(o, l0_xo, BF16)
    xf, hn = residual_norm(xf, y, l0_norm_x_post, l0_norm_ffn_pre)
    hmid = swiglu_up(hn, l0_ffn_gate, l0_ffn_up)
    y = matmul_tiledk(hmid, l0_ffn_down, BF16)
    xf, hn = residual_norm(xf, y, l0_norm_ffn_post, l1_norm_mix_pre)

    n_gates = 2 * MLSTM_HEADS
    n_main = l1_mix_in.shape[1] - n_gates
    w_in_t = l1_mix_in.T
    proj = matmul_fullk(hn, w_in_t, BF16, w_rows=(0, n_main))
    gates_t = mlstm_gates(hn, w_in_t[n_main:], l1_gate_b)
    y = mlstm_core(proj, gates_t, l1_conv, l1_head_norm, B)
    y = matmul_fullk(y, l1_mix_out, BF16)
    xf, hn = residual_norm(xf, y, l1_norm_mix_post, l1_norm_x_pre)
    o = cross_attention(matmul_fullk(hn, l1_xq, BF16), kv, B)
    y = matmul_fullk(o, l1_xo, BF16)
    xf, idx, wts = residual_router(xf, y, l1_norm_x_post, l1_norm_ffn_pre, l1_router)
    tm = min(MOE_TM, T)
    slot, token_of_slot, plan = _moe_plan(idx[:, :TOP_K], T, tm)
    xs = moe_dispatch(xf, l1_norm_ffn_pre, token_of_slot, plan[1], tm)
    hs = moe_up(xs, l1_moe_gate, l1_moe_up, plan, tm)
    ys = moe_down(hs, l1_moe_down, plan, tm)
    xf = moe_combine_residual(xf, ys, slot, wts, l1_norm_ffn_post)
    return xf.reshape(B, S, D)
```

```python
import functools

import jax
import jax.numpy as jnp
from jax import lax
from jax.experimental import pallas as pl
from jax.experimental.pallas import tpu as pltpu

F32 = jnp.float32
BF16 = jnp.bfloat16

EPS = 1e-6
CHUNK = 128
SGU_GROUPS = 8
MLSTM_HEADS = 8
CONV_WIDTH = 4
GATE_CAP = 15.0
X_HEADS = 4
N_EXPERTS = 8
TOP_K = 2
LANES = 128
MOE_TM = 512
MOE_ROW_CLASSES = (128, 256)
GATHER_ROWS = 16
GATHER_UNROLL = 2
GATHER_SLOTS = 3
COMBINE_ROWS = 8
COMBINE_UNROLL = 4
MLSTM_HEADS_PER_STEP = 2
CONV_TAIL = 8
VMEM_LIMIT = 56 * 1024 * 1024


def _cparams(n_axes, vmem=VMEM_LIMIT):
    return pltpu.CompilerParams(
        dimension_semantics=("arbitrary",) * n_axes, vmem_limit_bytes=vmem)


def _tile(pref, dim):
    t = min(pref, dim)
    assert dim % t == 0, (pref, dim)
    return t


def _rms(x, g):
    return x * lax.rsqrt(jnp.mean(x * x, axis=-1, keepdims=True) + EPS) * g


def _sigmoid(x):
    return 1.0 / (1.0 + jnp.exp(-x))


def _rms_kernel(x_ref, g_ref, o_ref):
    o_ref[...] = _rms(x_ref[...], g_ref[...]).astype(o_ref.dtype)


def rmsnorm_rows(x, g, tm=256):
    M, D = x.shape
    tm = _tile(tm, M)
    return pl.pallas_call(
        _rms_kernel,
        grid=(M // tm,),
        in_specs=[pl.BlockSpec((tm, D), lambda i: (i, 0)),
                  pl.BlockSpec((1, D), lambda i: (0, 0))],
        out_specs=pl.BlockSpec((tm, D), lambda i: (i, 0)),
        out_shape=jax.ShapeDtypeStruct((M, D), BF16),
        compiler_params=_cparams(1),
        name="rmsnorm_rows",
    )(x, g.reshape(1, D))


def _gelu(x):
    return 0.5 * x * (1.0 + lax.erf(x * 0.7071067811865476))


def _stream_weights(w_hbms, stage_refs, wb_refs, sem_ref, *, tn, nj, transposed, first=0):
    j = pl.program_id(0)
    i = pl.program_id(1)

    def copies(jj):
        off = pl.multiple_of(first + jj * tn, tn)
        return [pltpu.make_async_copy(
                    w.at[pl.ds(off, tn), :] if transposed else w.at[:, pl.ds(off, tn)],
                    s, sem_ref.at[n])
                for n, (w, s) in enumerate(zip(w_hbms, stage_refs))]

    @pl.when((j == 0) & (i == 0))
    def _():
        for c in copies(0):
            c.start()

    @pl.when(i == 0)
    def _():
        for c in copies(j):
            c.wait()
        for s, wb in zip(stage_refs, wb_refs):
            wb[...] = (s[...].T if transposed else s[...]).astype(BF16)

        @pl.when(j + 1 < nj)
        def _():
            for c in copies(j + 1):
                c.start()


def _mm_kernel(a_ref, w_hbm, o_ref, stage_ref, wb_ref, sem_ref, *, act, tn, nj, transposed,
               first):
    _stream_weights((w_hbm,), (stage_ref,), (wb_ref,), sem_ref,
                    tn=tn, nj=nj, transposed=transposed, first=first)
    acc = jnp.dot(a_ref[...], wb_ref[...], preferred_element_type=F32)
    if act == "gelu":
        acc = _gelu(acc)
    o_ref[...] = acc.astype(o_ref.dtype)


def matmul_fullk(a, w, out_dtype, act=None, w_rows=None, tm=1024, tn=1024):
    M, K = a.shape
    transposed = w_rows is not None
    first, N = w_rows if transposed else (0, w.shape[1])
    tm = _tile(tm, M)
    tn = _tile(tn, N)
    assert first % tn == 0
    nj = N // tn
    stage_block = (tn, K) if transposed else (K, tn)
    return pl.pallas_call(
        functools.partial(_mm_kernel, act=act, tn=tn, nj=nj, transposed=transposed,
                          first=first),
        grid=(nj, M // tm),
        in_specs=[pl.BlockSpec((tm, K), lambda j, i: (i, 0)),
                  pl.BlockSpec(memory_space=pl.ANY)],
        out_specs=pl.BlockSpec((tm, tn), lambda j, i: (i, j)),
        out_shape=jax.ShapeDtypeStruct((M, N), out_dtype),
        scratch_shapes=[pltpu.VMEM(stage_block, F32), pltpu.VMEM((K, tn), BF16),
                        pltpu.SemaphoreType.DMA((1,))],
        compiler_params=_cparams(2),
        name="matmul_fullk" + ("_" + act if act else "") + ("_t" if transposed else ""),
    )(a, w)


def _swiglu_kernel(a_ref, wg_hbm, wu_hbm, o_ref, sg_ref, su_ref, wb_ref, sem_ref, *, tn, nj):
    _stream_weights((wg_hbm, wu_hbm), (sg_ref, su_ref),
                    (wb_ref.at[:, :tn], wb_ref.at[:, tn:]), sem_ref,
                    tn=tn, nj=nj, transposed=False)
    gu = jnp.dot(a_ref[...], wb_ref[...], preferred_element_type=F32)
    g = gu[:, :tn]
    o_ref[...] = (g * _sigmoid(g) * gu[:, tn:]).astype(o_ref.dtype)


def swiglu_up(a, wg, wu, tm=1024, tn=512):
    M, K = a.shape
    F = wg.shape[1]
    tm = _tile(tm, M)
    tn = _tile(tn, F)
    nj = F // tn
    hbm = pl.BlockSpec(memory_space=pl.ANY)
    return pl.pallas_call(
        functools.partial(_swiglu_kernel, tn=tn, nj=nj),
        grid=(nj, M // tm),
        in_specs=[pl.BlockSpec((tm, K), lambda j, i: (i, 0)), hbm, hbm],
        out_specs=pl.BlockSpec((tm, tn), lambda j, i: (i, j)),
        out_shape=jax.ShapeDtypeStruct((M, F), BF16),
        scratch_shapes=[pltpu.VMEM((K, tn), F32), pltpu.VMEM((K, tn), F32),
                        pltpu.VMEM((K, 2 * tn), BF16), pltpu.SemaphoreType.DMA((2,))],
        compiler_params=_cparams(2),
        name="swiglu_up",
    )(a, wg, wu)


def _mmk_kernel(a_ref, w_ref, o_ref, acc_ref, *, nk):
    k = pl.program_id(2)

    @pl.when(k == 0)
    def _():
        acc_ref[...] = jnp.zeros_like(acc_ref)

    acc_ref[...] += jnp.dot(a_ref[...], w_ref[...].astype(BF16),
                            preferred_element_type=F32)

    @pl.when(k == nk - 1)
    def _():
        o_ref[...] = acc_ref[...].astype(o_ref.dtype)


def matmul_tiledk(a, w, out_dtype, tm=2048, tn=1024, tk=1024):
    M, K = a.shape
    N = w.shape[1]
    tm, tn, tk = _tile(tm, M), _tile(tn, N), _tile(tk, K)
    nk = K // tk
    return pl.pallas_call(
        functools.partial(_mmk_kernel, nk=nk),
        grid=(N // tn, M // tm, nk),
        in_specs=[pl.BlockSpec((tm, tk), lambda j, i, k: (i, k)),
                  pl.BlockSpec((tk, tn), lambda j, i, k: (k, j))],
        out_specs=pl.BlockSpec((tm, tn), lambda j, i, k: (i, j)),
        out_shape=jax.ShapeDtypeStruct((M, N), out_dtype),
        scratch_shapes=[pltpu.VMEM((tm, tn), F32)],
        compiler_params=_cparams(3),
        name="matmul_tiledk",
    )(a, w)


def _res_kernel(x_ref, y_ref, gp_ref, gn_ref, xo_ref, hn_ref):
    xn = x_ref[...] + _rms(y_ref[...].astype(F32), gp_ref[...])
    xo_ref[...] = xn
    hn_ref[...] = _rms(xn, gn_ref[...]).astype(hn_ref.dtype)


def residual_norm(x, y, g_post, g_next, tm=256):
    M, D = x.shape
    tm = _tile(tm, M)
    row = pl.BlockSpec((tm, D), lambda i: (i, 0))
    vec = pl.BlockSpec((1, D), lambda i: (0, 0))
    return pl.pallas_call(
        _res_kernel, grid=(M // tm,),
        in_specs=[row, row, vec, vec], out_specs=[row, row],
        out_shape=[jax.ShapeDtypeStruct((M, D), F32), jax.ShapeDtypeStruct((M, D), BF16)],
        compiler_params=_cparams(1), name="residual_norm",
    )(x, y, g_post.reshape(1, D), g_next.reshape(1, D))


def _sgu_kernel(u_ref, v_ref, lg_ref, lb_ref, w_ref, bt_ref, o_ref, wm_ref, *, groups):
    @pl.when(pl.program_id(0) == 0)
    def _():
        t = lax.broadcasted_iota(jnp.int32, (CHUNK, CHUNK), 0)
        s = lax.broadcasted_iota(jnp.int32, (CHUNK, CHUNK), 1)
        for g in range(groups):
            wm_ref[g] = jnp.where(t >= s, w_ref[g], 0.0).astype(BF16)

    v = v_ref[...].astype(F32)
    vc = v - jnp.mean(v, axis=-1, keepdims=True)
    vn = vc * lax.rsqrt(jnp.mean(vc * vc, axis=-1, keepdims=True) + EPS)
    vn = (vn * lg_ref[...] + lb_ref[...]).astype(BF16)
    tm, width = vn.shape
    gd = width // groups
    for c in range(tm // CHUNK):
        rows = slice(c * CHUNK, (c + 1) * CHUNK)
        for g in range(groups):
            cols = slice(g * gd, (g + 1) * gd)
            mixed = jnp.dot(wm_ref[g], vn[rows, cols], preferred_element_type=F32)
            mixed = mixed + bt_ref[:, g:g + 1]
            o_ref[rows, cols] = (u_ref[rows, cols].astype(F32) * mixed).astype(o_ref.dtype)


def sgu_mix(z, ln_g, ln_b, sgu_w, sgu_b, tm=256):
    M, W2 = z.shape
    W = W2 // 2
    G = sgu_w.shape[0]
    tm = _tile(tm, M)
    return pl.pallas_call(
        functools.partial(_sgu_kernel, groups=G),
        grid=(M // tm,),
        in_specs=[pl.BlockSpec((tm, W), lambda i: (i, 0)),
                  pl.BlockSpec((tm, W), lambda i: (i, 1)),
                  pl.BlockSpec((1, W), lambda i: (0, 0)),
                  pl.BlockSpec((1, W), lambda i: (0, 0)),
                  pl.BlockSpec((G, CHUNK, CHUNK), lambda i: (0, 0, 0)),
                  pl.BlockSpec((CHUNK, G), lambda i: (0, 0))],
        out_specs=pl.BlockSpec((tm, W), lambda i: (i, 0)),
        out_shape=jax.ShapeDtypeStruct((M, W), BF16),
        scratch_shapes=[pltpu.VMEM((G, CHUNK, CHUNK), BF16)],
        compiler_params=_cparams(1),
        name="sgu_mix",
    )(z, z, ln_g.reshape(1, W), ln_b.reshape(1, W), sgu_w, sgu_b.T)


def _xattn_kernel(q_ref, k_ref, v_ref, o_ref, *, heads, scale):
    hd = q_ref.shape[-1] // heads
    for h in range(heads):
        cols = slice(h * hd, (h + 1) * hd)
        s = lax.dot_general(q_ref[:, cols], k_ref[0, :, cols], (((1,), (1,)), ((), ())),
                            preferred_element_type=F32) * scale
        p = jnp.exp(s - jnp.max(s, axis=-1, keepdims=True))
        p = p / jnp.sum(p, axis=-1, keepdims=True)
        o = jnp.dot(p.astype(BF16), v_ref[0, :, cols], preferred_element_type=F32)
        o_ref[:, cols] = o.astype(o_ref.dtype)


def cross_attention(q, kv, batch, tm=512):
    M, D = q.shape
    S = M // batch
    n_mem = kv.shape[1]
    tm = _tile(tm, S)
    spt = S // tm
    return pl.pallas_call(
        functools.partial(_xattn_kernel, heads=X_HEADS, scale=(D // X_HEADS) ** -0.5),
        grid=(batch, spt),
        in_specs=[pl.BlockSpec((tm, D), lambda b, m: (b * spt + m, 0)),
                  pl.BlockSpec((1, n_mem, D), lambda b, m: (b, 0, 0)),
                  pl.BlockSpec((1, n_mem, D), lambda b, m: (b, 0, 1))],
        out_specs=pl.BlockSpec((tm, D), lambda b, m: (b * spt + m, 0)),
        out_shape=jax.ShapeDtypeStruct((M, D), BF16),
        compiler_params=_cparams(2),
        name="cross_attention",
    )(q, kv, kv)


def _gates_kernel(hn_ref, w_ref, b_ref, o_ref):
    acc = lax.dot_general(w_ref[...].astype(BF16), hn_ref[...], (((1,), (1,)), ((), ())),
                          preferred_element_type=F32)
    o_ref[...] = GATE_CAP * jnp.tanh((acc + b_ref[...]) / GATE_CAP)


def mlstm_gates(hn, w_gates_t, gate_b, tm=512):
    M, D = hn.shape
    n_gates = w_gates_t.shape[0]
    tm = _tile(tm, M)
    return pl.pallas_call(
        _gates_kernel,
        grid=(M // tm,),
        in_specs=[pl.BlockSpec((tm, D), lambda i: (i, 0)),
                  pl.BlockSpec((n_gates, D), lambda i: (0, 0)),
                  pl.BlockSpec((n_gates, 1), lambda i: (0, 0))],
        out_specs=pl.BlockSpec((n_gates, tm), lambda i: (0, i)),
        out_shape=jax.ShapeDtypeStruct((n_gates, M), F32),
        compiler_params=_cparams(1),
        name="mlstm_gates",
    )(hn, w_gates_t, gate_b.reshape(n_gates, 1))


def _conv_silu(x, tail_ref, w):
    prev = tail_ref[...]
    rid = lax.broadcasted_iota(jnp.int32, prev.shape, 0)
    acc = x * w[CONV_WIDTH - 1:CONV_WIDTH, :]
    for r in range(1, CONV_WIDTH):
        rolled = pltpu.roll(x, r, 0)
        head = jnp.where(rid < r, pltpu.roll(prev, r, 0), rolled[:CONV_TAIL])
        shifted = jnp.concatenate([head, rolled[CONV_TAIL:]], axis=0)
        acc = acc + shifted * w[CONV_WIDTH - 1 - r:CONV_WIDTH - r, :]
    tail_ref[...] = x[x.shape[0] - CONV_TAIL:, :]
    return acc * _sigmoid(acc)


def _mlstm_kernel(q_ref, k_ref, v_ref, o_ref, ig_ref, fg_ref, cq_ref, ck_ref, hg_ref,
                  y_ref, s_ref, n_ref, m_ref, qt_ref, kt_ref, *, heads):
    @pl.when(pl.program_id(2) == 0)
    def _():
        s_ref[...] = jnp.zeros_like(s_ref)
        n_ref[...] = jnp.zeros_like(n_ref)
        m_ref[...] = jnp.zeros_like(m_ref)
        qt_ref[...] = jnp.zeros_like(qt_ref)
        kt_ref[...] = jnp.zeros_like(kt_ref)

    dk = q_ref.shape[-1] // heads
    dv = v_ref.shape[-1] // heads
    for h in range(heads):
        qc = slice(h * dk, (h + 1) * dk)
        vc = slice(h * dv, (h + 1) * dv)
        q = _conv_silu(q_ref[:, qc].astype(F32), qt_ref.at[h], cq_ref[:, qc]) * dk ** -0.5
        k = _conv_silu(k_ref[:, qc].astype(F32), kt_ref.at[h], ck_ref[:, qc])
        _mlstm_head(q, k, v_ref[:, vc], o_ref[:, vc], ig_ref[h], fg_ref[h], hg_ref[:, vc],
                    y_ref.at[:, vc], s_ref.at[h], n_ref.at[h], m_ref.at[h])


def _mlstm_head(q, k, vb, ob, i_row, f_row, hg, y_ref, s_ref, n_ref, m_ref):
    L = CHUNK
    qb = q.astype(BF16)
    kb = k.astype(BF16)
    v = vb.astype(F32)

    t_id = lax.broadcasted_iota(jnp.int32, (L, L), 0)
    s_id = lax.broadcasted_iota(jnp.int32, (L, L), 1)
    eye = t_id == s_id
    causal = t_id >= s_id

    logf_row = -(jnp.maximum(-f_row, 0.0) + jnp.log1p(jnp.exp(-jnp.abs(f_row))))
    logf_b = jnp.broadcast_to(logf_row, (L, L))
    i_b = jnp.broadcast_to(i_row, (L, L))
    logf_col = jnp.sum(jnp.where(eye, logf_b, 0.0), axis=1, keepdims=True)
    i_col = jnp.sum(jnp.where(eye, i_b, 0.0), axis=1, keepdims=True)
    b_col = jnp.sum(jnp.where(causal, logf_b, 0.0), axis=1, keepdims=True)
    b_row = jnp.sum(jnp.where(t_id <= s_id, jnp.broadcast_to(logf_col, (L, L)), 0.0),
                    axis=0, keepdims=True)
    g_tot = jnp.sum(logf_row, axis=1, keepdims=True)
    m0 = m_ref[:, 0:1]

    d_log = jnp.where(causal, b_col - b_row + i_row, -jnp.inf)
    m_inter = b_col + m0
    m_t = jnp.maximum(m_inter, jnp.max(d_log, axis=1, keepdims=True))
    qk = lax.dot_general(qb, kb, (((1,), (1,)), ((), ())), preferred_element_type=F32)
    scores = qk * jnp.exp(d_log - m_t)
    inter = jnp.exp(m_inter - m_t)
    num = (jnp.dot(scores.astype(BF16), vb, preferred_element_type=F32)
           + inter * jnp.dot(qb, s_ref[...].astype(BF16), preferred_element_type=F32))
    den = (jnp.sum(scores, axis=1, keepdims=True)
           + inter * jnp.sum(q * n_ref[...], axis=1, keepdims=True))
    h = num / jnp.maximum(jnp.abs(den), jnp.exp(-m_t))
    y_ref[...] = (_sigmoid(ob.astype(F32)) * _rms(h, hg)).astype(y_ref.dtype)

    w_log = g_tot - b_col + i_col
    a = jnp.max(w_log, axis=0, keepdims=True)
    w = jnp.exp(w_log - a)
    s_loc = lax.dot_general(kb, (w * v).astype(BF16), (((0,), (0,)), ((), ())),
                            preferred_element_type=F32)
    n_loc = jnp.sum(w * k, axis=0, keepdims=True)
    m_new = jnp.maximum(g_tot + m0, a)
    sc_prev = jnp.exp(g_tot + m0 - m_new)
    sc_loc = jnp.exp(a - m_new)
    s_ref[...] = sc_prev * s_ref[...] + sc_loc * s_loc
    n_ref[...] = sc_prev * n_ref[...] + sc_loc * n_loc
    m_ref[...] = jnp.broadcast_to(m_new, m_ref.shape)


def mlstm_core(proj, gates_t, conv_w, head_g, batch):
    M = proj.shape[0]
    H = MLSTM_HEADS
    G = MLSTM_HEADS_PER_STEP
    assert H % G == 0
    ng = H // G
    qkw = conv_w.shape[1]
    dk = qkw // (2 * H)
    dv = head_g.shape[0] // H
    nc = M // batch // CHUNK
    assert qkw % (G * dv) == 0
    v0 = qkw // (G * dv)
    gates4 = gates_t.reshape(2 * ng, G, 1, M)
    row = lambda b, c: b * nc + c
    return pl.pallas_call(
        functools.partial(_mlstm_kernel, heads=G),
        grid=(batch, ng, nc),
        in_specs=[pl.BlockSpec((CHUNK, G * dk), lambda b, g, c: (row(b, c), g)),
                  pl.BlockSpec((CHUNK, G * dk), lambda b, g, c: (row(b, c), ng + g)),
                  pl.BlockSpec((CHUNK, G * dv), lambda b, g, c: (row(b, c), v0 + g)),
                  pl.BlockSpec((CHUNK, G * dv), lambda b, g, c: (row(b, c), v0 + ng + g)),
                  pl.BlockSpec((None, G, 1, CHUNK), lambda b, g, c: (g, 0, 0, row(b, c))),
                  pl.BlockSpec((None, G, 1, CHUNK), lambda b, g, c: (ng + g, 0, 0, row(b, c))),
                  pl.BlockSpec((CONV_WIDTH, G * dk), lambda b, g, c: (0, g)),
                  pl.BlockSpec((CONV_WIDTH, G * dk), lambda b, g, c: (0, ng + g)),
                  pl.BlockSpec((1, G * dv), lambda b, g, c: (0, g))],
        out_specs=pl.BlockSpec((CHUNK, G * dv), lambda b, g, c: (row(b, c), g)),
        out_shape=jax.ShapeDtypeStruct((M, H * dv), BF16),
        scratch_shapes=[pltpu.VMEM((G, dk, dv), F32), pltpu.VMEM((G, 1, dk), F32),
                        pltpu.VMEM((G, 1, LANES), F32),
                        pltpu.VMEM((G, CONV_TAIL, dk), F32), pltpu.VMEM((G, CONV_TAIL, dk), F32)],
        compiler_params=_cparams(3),
        name="mlstm_core",
    )(proj, proj, proj, proj, gates4, gates4, conv_w, conv_w, head_g.reshape(1, H * dv))


def _res_router_kernel(x_ref, y_ref, gp_ref, gn_ref, w2_ref, xo_ref, idx_ref, wt_ref):
    xn = x_ref[...] + _rms(y_ref[...].astype(F32), gp_ref[...])
    xo_ref[...] = xn
    hn = _rms(xn, gn_ref[...])
    hi = hn.astype(BF16)
    lo = (hn - hi.astype(F32)).astype(BF16)
    t = jnp.dot(hi, w2_ref[...], preferred_element_type=F32)
    logits = (t[:, :LANES] + t[:, LANES:]
              + jnp.dot(lo, w2_ref[:, :LANES], preferred_element_type=F32))
    lane = lax.broadcasted_iota(jnp.int32, logits.shape, 1)
    neg = -jnp.inf
    l1 = jnp.where(lane < N_EXPERTS, logits, neg)
    m1 = jnp.max(l1, axis=1, keepdims=True)
    i1 = jnp.min(jnp.where(l1 == m1, lane, LANES), axis=1, keepdims=True)
    l2 = jnp.where(lane == i1, neg, l1)
    m2 = jnp.max(l2, axis=1, keepdims=True)
    i2 = jnp.min(jnp.where(l2 == m2, lane, LANES), axis=1, keepdims=True)
    r = jnp.exp(m2 - m1)
    w1 = 1.0 / (1.0 + r)
    w2 = r / (1.0 + r)
    idx_ref[...] = jnp.where(lane == 0, i1, jnp.where(lane == 1, i2, 0))
    wt_ref[...] = jnp.where(lane == 0, w1, jnp.where(lane == 1, w2, 0.0))


def residual_router(x, y, g_post, g_next, w_router, tm=256):
    M, D = x.shape
    tm = _tile(tm, M)
    w_pad = jnp.pad(w_router, ((0, 0), (0, LANES - w_router.shape[1])))
    w_hi = w_pad.astype(BF16)
    w_lo = (w_pad - w_hi.astype(F32)).astype(BF16)
    row = pl.BlockSpec((tm, D), lambda i: (i, 0))
    vec = pl.BlockSpec((1, D), lambda i: (0, 0))
    out = pl.BlockSpec((tm, LANES), lambda i: (i, 0))
    return pl.pallas_call(
        _res_router_kernel, grid=(M // tm,),
        in_specs=[row, row, vec, vec, pl.BlockSpec((D, 2 * LANES), lambda i: (0, 0))],
        out_specs=[row, out, out],
        out_shape=[jax.ShapeDtypeStruct((M, D), F32),
                   jax.ShapeDtypeStruct((M, LANES), jnp.int32),
                   jax.ShapeDtypeStruct((M, LANES), F32)],
        compiler_params=_cparams(1), name="residual_router",
    )(x, y, g_post.reshape(1, D), g_next.reshape(1, D),
      jnp.concatenate([w_hi, w_lo], axis=1))


def _row_copy(src_hbm, src_row, dst_ref, dst_row, sem):
    return pltpu.make_async_copy(src_hbm.at[pl.ds(src_row, 1)],
                                 dst_ref.at[pl.ds(dst_row, 1)], sem)


def _dispatch_kernel(tok_ref, nt_ref, x_hbm, g_ref, o_ref, buf_ref, sem_ref, *, tm):
    i = pl.program_id(0)
    nt = nt_ref[0]
    n_blocks = tm // GATHER_ROWS

    def start_rows(tile, slot, blk):
        for q in range(GATHER_ROWS):
            r = blk * GATHER_ROWS + q
            _row_copy(x_hbm, tok_ref[tile * tm + r], buf_ref.at[slot], r,
                      sem_ref.at[slot]).start()

    def wait_rows(slot, blk):
        for q in range(GATHER_ROWS):
            _row_copy(x_hbm, 0, buf_ref.at[slot], blk * GATHER_ROWS + q,
                      sem_ref.at[slot]).wait()

    def norm_rows(slot, blk):
        rows = pl.ds(pl.multiple_of(blk * GATHER_ROWS, GATHER_ROWS), GATHER_ROWS)
        o_ref[rows, :] = _rms(buf_ref[slot, rows, :], g_ref[...]).astype(o_ref.dtype)

    def loop(body):
        lax.fori_loop(0, n_blocks, lambda blk, c: (body(blk), c)[1], 0, unroll=GATHER_UNROLL)

    ahead = GATHER_SLOTS - 1

    @pl.when(i == 0)
    def _():
        for t in range(ahead):
            @pl.when(t < nt)
            def _(t=t):
                loop(lambda blk: start_rows(t, t, blk))

    slot = i % GATHER_SLOTS

    @pl.when(i < nt)
    def _():
        loop(lambda blk: wait_rows(slot, blk))

    @pl.when(i + ahead < nt)
    def _():
        def body(blk):
            start_rows(i + ahead, (i + ahead) % GATHER_SLOTS, blk)
            norm_rows(slot, blk)
        loop(body)

    @pl.when((i < nt) & (i + ahead >= nt))
    def _():
        loop(lambda blk: norm_rows(slot, blk))

    @pl.when(i >= nt)
    def _():
        o_ref[...] = jnp.zeros_like(o_ref)


def moe_dispatch(x, g, token_of_slot, n_tiles, tm):
    T, D = x.shape
    P = token_of_slot.shape[0]
    return pl.pallas_call(
        functools.partial(_dispatch_kernel, tm=tm),
        grid_spec=pltpu.PrefetchScalarGridSpec(
            num_scalar_prefetch=2,
            grid=(P // tm,),
            in_specs=[pl.BlockSpec(memory_space=pl.ANY),
                      pl.BlockSpec((1, D), lambda i, tok, nt: (0, 0))],
            out_specs=pl.BlockSpec((tm, D), lambda i, tok, nt: (i, 0)),
            scratch_shapes=[pltpu.VMEM((GATHER_SLOTS, tm, D), F32),
                            pltpu.SemaphoreType.DMA((GATHER_SLOTS,))]),
        out_shape=jax.ShapeDtypeStruct((P, D), BF16),
        compiler_params=_cparams(1),
        name="moe_dispatch",
    )(token_of_slot, n_tiles, x, g.reshape(1, D))


def _weights_changed(te_ref, i):
    return (i == 0) | (te_ref[i] != te_ref[jnp.maximum(i - 1, 0)])


def _stage_copies(w_hbms, e, j, tn, stage_refs, sem_ref):
    col = pl.multiple_of(j * tn, tn)
    return [pltpu.make_async_copy(w.at[e, :, pl.ds(col, tn)], s, sem_ref.at[n])
            for n, (w, s) in enumerate(zip(w_hbms, stage_refs))]


def _restage_weights(te_ref, nx_ref, w_hbms, stage_refs, wb_refs, sem_ref, *, tn, nj):
    j = pl.program_id(0)
    i = pl.program_id(1)

    @pl.when((j == 0) & (i == 0))
    def _():
        for c in _stage_copies(w_hbms, te_ref[0], 0, tn, stage_refs, sem_ref):
            c.start()

    @pl.when(_weights_changed(te_ref, i))
    def _():
        for c in _stage_copies(w_hbms, te_ref[i], j, tn, stage_refs, sem_ref):
            c.wait()
        for s, wb in zip(stage_refs, wb_refs):
            wb[...] = s[...].astype(BF16)
        nxt = nx_ref[i]
        last = nxt < 0
        e_next = jnp.where(last, te_ref[0], nxt)
        j_next = jnp.where(last, j + 1, j)

        @pl.when(j_next < nj)
        def _():
            for c in _stage_copies(w_hbms, e_next, j_next, tn, stage_refs, sem_ref):
                c.start()


def _for_used_rows(n_used, o_ref, compute):
    tm = o_ref.shape[0]
    classes = sorted({min(c, tm) for c in MOE_ROW_CLASSES} | {tm})
    lo = 0
    for r in classes:
        @pl.when((n_used > lo) & (n_used <= r))
        def _(r=r):
            o_ref[:r, :] = compute(r).astype(o_ref.dtype)
            if r < tm:
                o_ref[r:, :] = jnp.zeros((tm - r, o_ref.shape[1]), o_ref.dtype)
        lo = r

    @pl.when(n_used == 0)
    def _():
        o_ref[...] = jnp.zeros_like(o_ref)


def _moe_up_kernel(te_ref, nt_ref, nx_ref, nu_ref, a_ref, wg_hbm, wu_hbm, o_ref,
                   sg_ref, su_ref, wgb_ref, wub_ref, sem_ref, *, tn, nj):
    _restage_weights(te_ref, nx_ref, (wg_hbm, wu_hbm), (sg_ref, su_ref),
                     (wgb_ref, wub_ref), sem_ref, tn=tn, nj=nj)

    def compute(r):
        a = a_ref[:r, :]
        g = jnp.dot(a, wgb_ref[...], preferred_element_type=F32)
        u = jnp.dot(a, wub_ref[...], preferred_element_type=F32)
        return g * _sigmoid(g) * u

    _for_used_rows(nu_ref[pl.program_id(1)], o_ref, compute)


def _moe_down_kernel(te_ref, nt_ref, nx_ref, nu_ref, a_ref, w_hbm, o_ref,
                     s_ref, wb_ref, sem_ref, *, tn, nj):
    _restage_weights(te_ref, nx_ref, (w_hbm,), (s_ref,), (wb_ref,), sem_ref, tn=tn, nj=nj)
    _for_used_rows(nu_ref[pl.program_id(1)], o_ref,
                   lambda r: jnp.dot(a_ref[:r, :], wb_ref[...], preferred_element_type=F32))


def _moe_grouped(kernel_fn, name, a, weights, plan, tm, tn, out_dtype):
    te, n_tiles, nxt, n_used = plan
    P, K = a.shape
    N = weights[0].shape[2]
    tn = _tile(tn, N)
    nj = N // tn
    used = lambda i, nt: jnp.minimum(i, nt[0] - 1)
    n_w = len(weights)
    return pl.pallas_call(
        functools.partial(kernel_fn, tn=tn, nj=nj),
        grid_spec=pltpu.PrefetchScalarGridSpec(
            num_scalar_prefetch=4,
            grid=(nj, P // tm),
            in_specs=[pl.BlockSpec((tm, K), lambda j, i, te, nt, nx, nu: (used(i, nt), 0))]
                     + [pl.BlockSpec(memory_space=pl.ANY)] * n_w,
            out_specs=pl.BlockSpec((tm, tn), lambda j, i, te, nt, nx, nu: (i, j)),
            scratch_shapes=[pltpu.VMEM((K, tn), F32)] * n_w + [pltpu.VMEM((K, tn), BF16)] * n_w
                           + [pltpu.SemaphoreType.DMA((n_w,))]),
        out_shape=jax.ShapeDtypeStruct((P, N), out_dtype),
        compiler_params=_cparams(2),
        name=name,
    )(te, n_tiles, nxt, n_used, a, *weights)


def moe_up(xs, wg, wu, plan, tm, tn=512):
    return _moe_grouped(_moe_up_kernel, "moe_up", xs, (wg, wu), plan, tm, tn, BF16)


def moe_down(hs, wd, plan, tm, tn=1024):
    return _moe_grouped(_moe_down_kernel, "moe_down", hs, (wd,), plan, tm, tn, F32)


def _combine_kernel(slot_ref, x_ref, wt_ref, gp_ref, ys_hbm, xo_ref, buf_ref, sem_ref, *, tm):
    i = pl.program_id(0)
    n_blocks = tm // COMBINE_ROWS

    def start_rows(tile, s, blk):
        for q in range(COMBINE_ROWS):
            r = blk * COMBINE_ROWS + q
            for k in range(TOP_K):
                _row_copy(ys_hbm, slot_ref[(tile * tm + r) * TOP_K + k], buf_ref.at[s, k], r,
                          sem_ref.at[s]).start()

    def wait_rows(s, blk):
        for q in range(COMBINE_ROWS):
            for k in range(TOP_K):
                _row_copy(ys_hbm, 0, buf_ref.at[s, k], blk * COMBINE_ROWS + q,
                          sem_ref.at[s]).wait()

    def combine_rows(s, blk):
        rows = pl.ds(pl.multiple_of(blk * COMBINE_ROWS, COMBINE_ROWS), COMBINE_ROWS)
        y = (wt_ref[rows, 0:1] * buf_ref[s, 0, rows, :]
             + wt_ref[rows, 1:2] * buf_ref[s, 1, rows, :])
        xo_ref[rows, :] = x_ref[rows, :] + _rms(y, gp_ref[...])

    def loop(body):
        lax.fori_loop(0, n_blocks, lambda blk, c: (body(blk), c)[1], 0, unroll=COMBINE_UNROLL)

    n = pl.num_programs(0)
    ahead = GATHER_SLOTS - 1

    @pl.when(i == 0)
    def _():
        for t in range(ahead):
            @pl.when(t < n)
            def _(t=t):
                loop(lambda blk: start_rows(t, t, blk))

    s = i % GATHER_SLOTS
    loop(lambda blk: wait_rows(s, blk))

    @pl.when(i + ahead < n)
    def _():
        def body(blk):
            start_rows(i + ahead, (i + ahead) % GATHER_SLOTS, blk)
            combine_rows(s, blk)
        loop(body)

    @pl.when(i + ahead >= n)
    def _():
        loop(lambda blk: combine_rows(s, blk))


def moe_combine_residual(x, ys, slot, wts, g_post, tm=256):
    M, D = x.shape
    tm = _tile(tm, M)
    row = pl.BlockSpec((tm, D), lambda i, sl: (i, 0))
    return pl.pallas_call(
        functools.partial(_combine_kernel, tm=tm),
        grid_spec=pltpu.PrefetchScalarGridSpec(
            num_scalar_prefetch=1,
            grid=(M // tm,),
            in_specs=[row, pl.BlockSpec((tm, LANES), lambda i, sl: (i, 0)),
                      pl.BlockSpec((1, D), lambda i, sl: (0, 0)),
                      pl.BlockSpec(memory_space=pl.ANY)],
            out_specs=row,
            scratch_shapes=[pltpu.VMEM((GATHER_SLOTS, TOP_K, tm, D), F32),
                            pltpu.SemaphoreType.DMA((GATHER_SLOTS,))]),
        out_shape=jax.ShapeDtypeStruct((M, D), F32),
        compiler_params=_cparams(1), name="moe_combine_residual",
    )(slot, x, wts, g_post.reshape(1, D), ys)


def _moe_plan(idx, n_tokens, tm):
    e_flat = idx.reshape(-1)
    onehot = (e_flat[:, None] == jnp.arange(N_EXPERTS)[None, :]).astype(jnp.int32)
    counts = jnp.sum(onehot, axis=0)
    rank = jnp.sum((jnp.cumsum(onehot, axis=0) - onehot) * onehot, axis=1)
    tiles_per = (counts + tm - 1) // tm
    tile_end = jnp.cumsum(tiles_per)
    tile_start = tile_end - tiles_per
    slot = (tile_start[e_flat] * tm + rank).astype(jnp.int32)
    n_tiles_max = (n_tokens * TOP_K) // tm + N_EXPERTS
    n_tiles = tile_end[-1]
    tile_ids = jnp.minimum(jnp.arange(n_tiles_max), n_tiles - 1)
    expert_of = lambda t: jnp.minimum(
        jnp.sum((tile_end[None, :] <= t[:, None]).astype(jnp.int32), axis=1), N_EXPERTS - 1)
    te = expert_of(tile_ids)
    group_end = tile_end[te]
    nxt = jnp.where(group_end < n_tiles, expert_of(jnp.minimum(group_end, n_tiles - 1)), -1)
    all_ids = jnp.arange(n_tiles_max)
    n_used = jnp.where(all_ids < n_tiles,
                       jnp.clip(counts[te] - (all_ids - tile_start[te]) * tm, 0, tm), 0)
    token_of_pair = jnp.arange(n_tokens * TOP_K, dtype=jnp.int32) // TOP_K
    token_of_slot = jnp.zeros((n_tiles_max * tm,), jnp.int32).at[slot].set(token_of_pair)
    plan = (te.astype(jnp.int32), n_tiles.reshape(1).astype(jnp.int32), nxt.astype(jnp.int32),
            n_used.astype(jnp.int32))
    return slot, token_of_slot, plan


def kernel(x, mem, mem_norm, mem_kv, l0_norm_mix_pre, l0_mix_in, l0_sgu_ln_g, l0_sgu_ln_b, l0_sgu_w, l0_sgu_b, l0_mix_out, l0_norm_mix_post, l0_norm_x_pre, l0_xq, l0_xo, l0_norm_x_post, l0_norm_ffn_pre, l0_ffn_gate, l0_ffn_up, l0_ffn_down, l0_norm_ffn_post, l1_norm_mix_pre, l1_mix_in, l1_gate_b, l1_conv, l1_head_norm, l1_mix_out, l1_norm_mix_post, l1_norm_x_pre, l1_xq, l1_xo, l1_norm_x_post, l1_norm_ffn_pre, l1_router, l1_moe_gate, l1_moe_up, l1_moe_down, l1_norm_ffn_post):
    B, S, D = x.shape
    T = B * S
    n_mem = mem.shape[1]
    xf = x.reshape(T, D)

    memn = rmsnorm_rows(mem.reshape(B * n_mem, D), mem_norm)
    kv = matmul_fullk(memn, mem_kv, BF16).reshape(B, n_mem, 2 * D)

    hn = rmsnorm_rows(xf, l0_norm_mix_pre)
    z = matmul_fullk(hn, l0_mix_in, BF16, act="gelu")
    y = sgu_mix(z, l0_sgu_ln_g, l0_sgu_ln_b, l0_sgu_w, l0_sgu_b)
    y = matmul_fullk(y, l0_mix_out, BF16)
    xf, hn = residual_norm(xf, y, l0_norm_mix_post, l0_norm_x_pre)
    o = cross_attention(matmul_fullk(hn, l0_xq, BF16), kv, B)
    y = matmul_fullk(o, l0_xo, BF16)
    xf, hn = residual_norm(xf, y, l0_norm_x_post, l0_norm_ffn_pre)
    hmid = swiglu_up(hn, l0_ffn_gate, l0_ffn_up)
    y = matmul_tiledk(hmid, l0_ffn_down, BF16)
    xf, hn = residual_norm(xf, y, l0_norm_ffn_post, l1_norm_mix_pre)

    n_gates = 2 * MLSTM_HEADS
    n_main = l1_mix_in.shape[1] - n_gates
    w_in_t = l1_mix_in.T
    proj = matmul_fullk(hn, w_in_t, BF16, w_rows=(0, n_main))
    gates_t = mlstm_gates(hn, w_in_t[n_main:], l1_gate_b)
    y = mlstm_core(proj, gates_t, l1_conv, l1_head_norm, B)
    y = matmul_fullk(y, l1_mix_out, BF16)
    xf, hn = residual_norm(xf, y, l1_norm_mix_post, l1_norm_x_pre)
    o = cross_attention(matmul_fullk(hn, l1_xq, BF16), kv, B)
    y = matmul_fullk(o, l1_xo, BF16)
    xf, idx, wts = residual_router(xf, y, l1_norm_x_post, l1_norm_ffn_pre, l1_router)
    tm = min(MOE_TM, T)
    slot, token_of_slot, plan = _moe_plan(idx[:, :TOP_K], T, tm)
    xs = moe_dispatch(xf, l1_norm_ffn_pre, token_of_slot, plan[1], tm)
    hs = moe_up(xs, l1_moe_gate, l1_moe_up, plan, tm)
    ys = moe_down(hs, l1_moe_down, plan, tm)
    xf = moe_combine_residual(xf, ys, slot, wts, l1_norm_ffn_post)
    return xf.reshape(B, S, D)
```

```python
import functools

import jax
import jax.numpy as jnp
from jax import lax
from jax.experimental import pallas as pl
from jax.experimental.pallas import tpu as pltpu

F32 = jnp.float32
BF16 = jnp.bfloat16

EPS = 1e-6
CHUNK = 128
SGU_GROUPS = 8
MLSTM_HEADS = 8
CONV_WIDTH = 4
GATE_CAP = 15.0
X_HEADS = 4
N_EXPERTS = 8
TOP_K = 2
LANES = 128
MOE_TM = 512
MOE_ROW_CLASSES = (128, 256)
GATHER_ROWS = 16
GATHER_UNROLL = 2
GATHER_SLOTS = 3
COMBINE_ROWS = 8
COMBINE_UNROLL = 4
MLSTM_HEADS_PER_STEP = 2
W_DMA_PARTS = 4
CONV_TAIL = 8
VMEM_LIMIT = 56 * 1024 * 1024


def _cparams(n_axes, vmem=VMEM_LIMIT):
    return pltpu.CompilerParams(
        dimension_semantics=("arbitrary",) * n_axes, vmem_limit_bytes=vmem)


def _tile(pref, dim):
    t = min(pref, dim)
    assert dim % t == 0, (pref, dim)
    return t


def _rms(x, g):
    return x * lax.rsqrt(jnp.mean(x * x, axis=-1, keepdims=True) + EPS) * g


def _sigmoid(x):
    return 1.0 / (1.0 + jnp.exp(-x))


def _rms_kernel(x_ref, g_ref, o_ref):
    o_ref[...] = _rms(x_ref[...], g_ref[...]).astype(o_ref.dtype)


def rmsnorm_rows(x, g, tm=256):
    M, D = x.shape
    tm = _tile(tm, M)
    return pl.pallas_call(
        _rms_kernel,
        grid=(M // tm,),
        in_specs=[pl.BlockSpec((tm, D), lambda i: (i, 0)),
                  pl.BlockSpec((1, D), lambda i: (0, 0))],
        out_specs=pl.BlockSpec((tm, D), lambda i: (i, 0)),
        out_shape=jax.ShapeDtypeStruct((M, D), BF16),
        compiler_params=_cparams(1),
        name="rmsnorm_rows",
    )(x, g.reshape(1, D))


def _gelu(x):
    return 0.5 * x * (1.0 + lax.erf(x * 0.7071067811865476))


def _stream_weights(w_hbms, stage_refs, wb_refs, sem_ref, *, tn, nj, transposed, first=0,
                    parts=1):
    j = pl.program_id(0)
    i = pl.program_id(1)

    def copies(jj, parts=1):
        off = pl.multiple_of(first + jj * tn, tn)
        out = []
        for n, (w, s) in enumerate(zip(w_hbms, stage_refs)):
            rows = s.shape[0] // parts
            for q in range(parts):
                band = pl.ds(q * rows, rows)
                src = (w.at[pl.ds(off + q * rows, rows), :] if transposed
                       else w.at[band, pl.ds(off, tn)])
                out.append(pltpu.make_async_copy(src, s.at[band, :], sem_ref.at[n]))
        return out

    @pl.when((j == 0) & (i == 0))
    def _():
        for c in copies(0, parts):
            c.start()

    @pl.when(i == 0)
    def _():
        for c in copies(j):
            c.wait()
        for s, wb in zip(stage_refs, wb_refs):
            wb[...] = (s[...].T if transposed else s[...]).astype(BF16)

        @pl.when(j + 1 < nj)
        def _():
            for c in copies(j + 1, parts):
                c.start()


def _mm_kernel(a_ref, w_hbm, o_ref, stage_ref, wb_ref, sem_ref, *, act, tn, nj, transposed,
               first, parts):
    _stream_weights((w_hbm,), (stage_ref,), (wb_ref,), sem_ref,
                    tn=tn, nj=nj, transposed=transposed, first=first, parts=parts)
    acc = jnp.dot(a_ref[...], wb_ref[...], preferred_element_type=F32)
    if act == "gelu":
        acc = _gelu(acc)
    o_ref[...] = acc.astype(o_ref.dtype)


def matmul_fullk(a, w, out_dtype, act=None, w_rows=None, tm=1024, tn=1024):
    M, K = a.shape
    transposed = w_rows is not None
    first, N = w_rows if transposed else (0, w.shape[1])
    tm = _tile(tm, M)
    tn = _tile(tn, N)
    assert first % tn == 0
    nj = N // tn
    stage_block = (tn, K) if transposed else (K, tn)
    parts = W_DMA_PARTS if M // tm == 1 else 1
    return pl.pallas_call(
        functools.partial(_mm_kernel, act=act, tn=tn, nj=nj, transposed=transposed,
                          first=first, parts=parts),
        grid=(nj, M // tm),
        in_specs=[pl.BlockSpec((tm, K), lambda j, i: (i, 0)),
                  pl.BlockSpec(memory_space=pl.ANY)],
        out_specs=pl.BlockSpec((tm, tn), lambda j, i: (i, j)),
        out_shape=jax.ShapeDtypeStruct((M, N), out_dtype),
        scratch_shapes=[pltpu.VMEM(stage_block, F32), pltpu.VMEM((K, tn), BF16),
                        pltpu.SemaphoreType.DMA((1,))],
        compiler_params=_cparams(2),
        name="matmul_fullk" + ("_" + act if act else "") + ("_t" if transposed else ""),
    )(a, w)


def _swiglu_kernel(a_ref, wg_hbm, wu_hbm, o_ref, sg_ref, su_ref, wb_ref, sem_ref, *, tn, nj):
    _stream_weights((wg_hbm, wu_hbm), (sg_ref, su_ref),
                    (wb_ref.at[:, :tn], wb_ref.at[:, tn:]), sem_ref,
                    tn=tn, nj=nj, transposed=False)
    gu = jnp.dot(a_ref[...], wb_ref[...], preferred_element_type=F32)
    g = gu[:, :tn]
    o_ref[...] = (g * _sigmoid(g) * gu[:, tn:]).astype(o_ref.dtype)


def swiglu_up(a, wg, wu, tm=1024, tn=512):
    M, K = a.shape
    F = wg.shape[1]
    tm = _tile(tm, M)
    tn = _tile(tn, F)
    nj = F // tn
    hbm = pl.BlockSpec(memory_space=pl.ANY)
    return pl.pallas_call(
        functools.partial(_swiglu_kernel, tn=tn, nj=nj),
        grid=(nj, M // tm),
        in_specs=[pl.BlockSpec((tm, K), lambda j, i: (i, 0)), hbm, hbm],
        out_specs=pl.BlockSpec((tm, tn), lambda j, i: (i, j)),
        out_shape=jax.ShapeDtypeStruct((M, F), BF16),
        scratch_shapes=[pltpu.VMEM((K, tn), F32), pltpu.VMEM((K, tn), F32),
                        pltpu.VMEM((K, 2 * tn), BF16), pltpu.SemaphoreType.DMA((2,))],
        compiler_params=_cparams(2),
        name="swiglu_up",
    )(a, wg, wu)


def _mmk_kernel(a_ref, w_ref, o_ref, acc_ref, *, nk):
    k = pl.program_id(2)

    @pl.when(k == 0)
    def _():
        acc_ref[...] = jnp.zeros_like(acc_ref)

    acc_ref[...] += jnp.dot(a_ref[...], w_ref[...].astype(BF16),
                            preferred_element_type=F32)

    @pl.when(k == nk - 1)
    def _():
        o_ref[...] = acc_ref[...].astype(o_ref.dtype)


def matmul_tiledk(a, w, out_dtype, tm=2048, tn=1024, tk=1024):
    M, K = a.shape
    N = w.shape[1]
    tm, tn, tk = _tile(tm, M), _tile(tn, N), _tile(tk, K)
    nk = K // tk
    return pl.pallas_call(
        functools.partial(_mmk_kernel, nk=nk),
        grid=(N // tn, M // tm, nk),
        in_specs=[pl.BlockSpec((tm, tk), lambda j, i, k: (i, k)),
                  pl.BlockSpec((tk, tn), lambda j, i, k: (k, j))],
        out_specs=pl.BlockSpec((tm, tn), lambda j, i, k: (i, j)),
        out_shape=jax.ShapeDtypeStruct((M, N), out_dtype),
        scratch_shapes=[pltpu.VMEM((tm, tn), F32)],
        compiler_params=_cparams(3),
        name="matmul_tiledk",
    )(a, w)


def _res_kernel(x_ref, y_ref, gp_ref, gn_ref, xo_ref, hn_ref):
    xn = x_ref[...] + _rms(y_ref[...].astype(F32), gp_ref[...])
    xo_ref[...] = xn
    hn_ref[...] = _rms(xn, gn_ref[...]).astype(hn_ref.dtype)


def residual_norm(x, y, g_post, g_next, tm=256):
    M, D = x.shape
    tm = _tile(tm, M)
    row = pl.BlockSpec((tm, D), lambda i: (i, 0))
    vec = pl.BlockSpec((1, D), lambda i: (0, 0))
    return pl.pallas_call(
        _res_kernel, grid=(M // tm,),
        in_specs=[row, row, vec, vec], out_specs=[row, row],
        out_shape=[jax.ShapeDtypeStruct((M, D), F32), jax.ShapeDtypeStruct((M, D), BF16)],
        compiler_params=_cparams(1), name="residual_norm",
    )(x, y, g_post.reshape(1, D), g_next.reshape(1, D))


def _sgu_kernel(u_ref, v_ref, lg_ref, lb_ref, w_ref, bt_ref, o_ref, wm_ref, *, groups):
    @pl.when(pl.program_id(0) == 0)
    def _():
        t = lax.broadcasted_iota(jnp.int32, (CHUNK, CHUNK), 0)
        s = lax.broadcasted_iota(jnp.int32, (CHUNK, CHUNK), 1)
        for g in range(groups):
            wm_ref[g] = jnp.where(t >= s, w_ref[g], 0.0).astype(BF16)

    v = v_ref[...].astype(F32)
    vc = v - jnp.mean(v, axis=-1, keepdims=True)
    vn = vc * lax.rsqrt(jnp.mean(vc * vc, axis=-1, keepdims=True) + EPS)
    vn = (vn * lg_ref[...] + lb_ref[...]).astype(BF16)
    tm, width = vn.shape
    gd = width // groups
    for c in range(tm // CHUNK):
        rows = slice(c * CHUNK, (c + 1) * CHUNK)
        for g in range(groups):
            cols = slice(g * gd, (g + 1) * gd)
            mixed = jnp.dot(wm_ref[g], vn[rows, cols], preferred_element_type=F32)
            mixed = mixed + bt_ref[:, g:g + 1]
            o_ref[rows, cols] = (u_ref[rows, cols].astype(F32) * mixed).astype(o_ref.dtype)


def sgu_mix(z, ln_g, ln_b, sgu_w, sgu_b, tm=256):
    M, W2 = z.shape
    W = W2 // 2
    G = sgu_w.shape[0]
    tm = _tile(tm, M)
    return pl.pallas_call(
        functools.partial(_sgu_kernel, groups=G),
        grid=(M // tm,),
        in_specs=[pl.BlockSpec((tm, W), lambda i: (i, 0)),
                  pl.BlockSpec((tm, W), lambda i: (i, 1)),
                  pl.BlockSpec((1, W), lambda i: (0, 0)),
                  pl.BlockSpec((1, W), lambda i: (0, 0)),
                  pl.BlockSpec((G, CHUNK, CHUNK), lambda i: (0, 0, 0)),
                  pl.BlockSpec((CHUNK, G), lambda i: (0, 0))],
        out_specs=pl.BlockSpec((tm, W), lambda i: (i, 0)),
        out_shape=jax.ShapeDtypeStruct((M, W), BF16),
        scratch_shapes=[pltpu.VMEM((G, CHUNK, CHUNK), BF16)],
        compiler_params=_cparams(1),
        name="sgu_mix",
    )(z, z, ln_g.reshape(1, W), ln_b.reshape(1, W), sgu_w, sgu_b.T)


def _xattn_kernel(q_ref, k_ref, v_ref, o_ref, *, heads, scale):
    hd = q_ref.shape[-1] // heads
    for h in range(heads):
        cols = slice(h * hd, (h + 1) * hd)
        s = lax.dot_general(q_ref[:, cols], k_ref[0, :, cols], (((1,), (1,)), ((), ())),
                            preferred_element_type=F32) * scale
        p = jnp.exp(s - jnp.max(s, axis=-1, keepdims=True))
        p = p / jnp.sum(p, axis=-1, keepdims=True)
        o = jnp.dot(p.astype(BF16), v_ref[0, :, cols], preferred_element_type=F32)
        o_ref[:, cols] = o.astype(o_ref.dtype)


def cross_attention(q, kv, batch, tm=512):
    M, D = q.shape
    S = M // batch
    n_mem = kv.shape[1]
    tm = _tile(tm, S)
    spt = S // tm
    return pl.pallas_call(
        functools.partial(_xattn_kernel, heads=X_HEADS, scale=(D // X_HEADS) ** -0.5),
        grid=(batch, spt),
        in_specs=[pl.BlockSpec((tm, D), lambda b, m: (b * spt + m, 0)),
                  pl.BlockSpec((1, n_mem, D), lambda b, m: (b, 0, 0)),
                  pl.BlockSpec((1, n_mem, D), lambda b, m: (b, 0, 1))],
        out_specs=pl.BlockSpec((tm, D), lambda b, m: (b * spt + m, 0)),
        out_shape=jax.ShapeDtypeStruct((M, D), BF16),
        compiler_params=_cparams(2),
        name="cross_attention",
    )(q, kv, kv)


def _gates_kernel(hn_ref, w_ref, b_ref, o_ref):
    acc = lax.dot_general(w_ref[...].astype(BF16), hn_ref[...], (((1,), (1,)), ((), ())),
                          preferred_element_type=F32)
    o_ref[...] = GATE_CAP * jnp.tanh((acc + b_ref[...]) / GATE_CAP)


def mlstm_gates(hn, w_gates_t, gate_b, tm=512):
    M, D = hn.shape
    n_gates = w_gates_t.shape[0]
    tm = _tile(tm, M)
    return pl.pallas_call(
        _gates_kernel,
        grid=(M // tm,),
        in_specs=[pl.BlockSpec((tm, D), lambda i: (i, 0)),
                  pl.BlockSpec((n_gates, D), lambda i: (0, 0)),
                  pl.BlockSpec((n_gates, 1), lambda i: (0, 0))],
        out_specs=pl.BlockSpec((n_gates, tm), lambda i: (0, i)),
        out_shape=jax.ShapeDtypeStruct((n_gates, M), F32),
        compiler_params=_cparams(1),
        name="mlstm_gates",
    )(hn, w_gates_t, gate_b.reshape(n_gates, 1))


def _conv_silu(x, tail_ref, w):
    prev = tail_ref[...]
    rid = lax.broadcasted_iota(jnp.int32, prev.shape, 0)
    acc = x * w[CONV_WIDTH - 1:CONV_WIDTH, :]
    for r in range(1, CONV_WIDTH):
        rolled = pltpu.roll(x, r, 0)
        head = jnp.where(rid < r, pltpu.roll(prev, r, 0), rolled[:CONV_TAIL])
        shifted = jnp.concatenate([head, rolled[CONV_TAIL:]], axis=0)
        acc = acc + shifted * w[CONV_WIDTH - 1 - r:CONV_WIDTH - r, :]
    tail_ref[...] = x[x.shape[0] - CONV_TAIL:, :]
    return acc * _sigmoid(acc)


def _mlstm_kernel(q_ref, k_ref, v_ref, o_ref, ig_ref, fg_ref, cq_ref, ck_ref, hg_ref,
                  y_ref, s_ref, n_ref, m_ref, qt_ref, kt_ref, *, heads):
    @pl.when(pl.program_id(2) == 0)
    def _():
        s_ref[...] = jnp.zeros_like(s_ref)
        n_ref[...] = jnp.zeros_like(n_ref)
        m_ref[...] = jnp.zeros_like(m_ref)
        qt_ref[...] = jnp.zeros_like(qt_ref)
        kt_ref[...] = jnp.zeros_like(kt_ref)

    dk = q_ref.shape[-1] // heads
    dv = v_ref.shape[-1] // heads
    for h in range(heads):
        qc = slice(h * dk, (h + 1) * dk)
        vc = slice(h * dv, (h + 1) * dv)
        q = _conv_silu(q_ref[:, qc].astype(F32), qt_ref.at[h], cq_ref[:, qc]) * dk ** -0.5
        k = _conv_silu(k_ref[:, qc].astype(F32), kt_ref.at[h], ck_ref[:, qc])
        _mlstm_head(q, k, v_ref[:, vc], o_ref[:, vc], ig_ref[h], fg_ref[h], hg_ref[:, vc],
                    y_ref.at[:, vc], s_ref.at[h], n_ref.at[h], m_ref.at[h])


def _mlstm_head(q, k, vb, ob, i_row, f_row, hg, y_ref, s_ref, n_ref, m_ref):
    L = CHUNK
    qb = q.astype(BF16)
    kb = k.astype(BF16)
    v = vb.astype(F32)

    t_id = lax.broadcasted_iota(jnp.int32, (L, L), 0)
    s_id = lax.broadcasted_iota(jnp.int32, (L, L), 1)
    eye = t_id == s_id
    causal = t_id >= s_id

    logf_row = -(jnp.maximum(-f_row, 0.0) + jnp.log1p(jnp.exp(-jnp.abs(f_row))))
    logf_b = jnp.broadcast_to(logf_row, (L, L))
    i_b = jnp.broadcast_to(i_row, (L, L))
    logf_col = jnp.sum(jnp.where(eye, logf_b, 0.0), axis=1, keepdims=True)
    i_col = jnp.sum(jnp.where(eye, i_b, 0.0), axis=1, keepdims=True)
    b_col = jnp.sum(jnp.where(causal, logf_b, 0.0), axis=1, keepdims=True)
    b_row = jnp.sum(jnp.where(t_id <= s_id, jnp.broadcast_to(logf_col, (L, L)), 0.0),
                    axis=0, keepdims=True)
    g_tot = jnp.sum(logf_row, axis=1, keepdims=True)
    m0 = m_ref[:, 0:1]

    d_log = jnp.where(causal, b_col - b_row + i_row, -jnp.inf)
    m_inter = b_col + m0
    m_t = jnp.maximum(m_inter, jnp.max(d_log, axis=1, keepdims=True))
    qk = lax.dot_general(qb, kb, (((1,), (1,)), ((), ())), preferred_element_type=F32)
    scores = qk * jnp.exp(d_log - m_t)
    inter = jnp.exp(m_inter - m_t)
    num = (jnp.dot(scores.astype(BF16), vb, preferred_element_type=F32)
           + inter * jnp.dot(qb, s_ref[...].astype(BF16), preferred_element_type=F32))
    den = (jnp.sum(scores, axis=1, keepdims=True)
           + inter * jnp.sum(q * n_ref[...], axis=1, keepdims=True))
    h = num / jnp.maximum(jnp.abs(den), jnp.exp(-m_t))
    y_ref[...] = (_sigmoid(ob.astype(F32)) * _rms(h, hg)).astype(y_ref.dtype)

    w_log = g_tot - b_col + i_col
    a = jnp.max(w_log, axis=0, keepdims=True)
    w = jnp.exp(w_log - a)
    s_loc = lax.dot_general(kb, (w * v).astype(BF16), (((0,), (0,)), ((), ())),
                            preferred_element_type=F32)
    n_loc = jnp.sum(w * k, axis=0, keepdims=True)
    m_new = jnp.maximum(g_tot + m0, a)
    sc_prev = jnp.exp(g_tot + m0 - m_new)
    sc_loc = jnp.exp(a - m_new)
    s_ref[...] = sc_prev * s_ref[...] + sc_loc * s_loc
    n_ref[...] = sc_prev * n_ref[...] + sc_loc * n_loc
    m_ref[...] = jnp.broadcast_to(m_new, m_ref.shape)


def mlstm_core(proj, gates_t, conv_w, head_g, batch):
    M = proj.shape[0]
    H = MLSTM_HEADS
    G = MLSTM_HEADS_PER_STEP
    assert H % G == 0
    ng = H // G
    qkw = conv_w.shape[1]
    dk = qkw // (2 * H)
    dv = head_g.shape[0] // H
    nc = M // batch // CHUNK
    assert qkw % (G * dv) == 0
    v0 = qkw // (G * dv)
    gates4 = gates_t.reshape(2 * ng, G, 1, M)
    row = lambda b, c: b * nc + c
    return pl.pallas_call(
        functools.partial(_mlstm_kernel, heads=G),
        grid=(batch, ng, nc),
        in_specs=[pl.BlockSpec((CHUNK, G * dk), lambda b, g, c: (row(b, c), g)),
                  pl.BlockSpec((CHUNK, G * dk), lambda b, g, c: (row(b, c), ng + g)),
                  pl.BlockSpec((CHUNK, G * dv), lambda b, g, c: (row(b, c), v0 + g)),
                  pl.BlockSpec((CHUNK, G * dv), lambda b, g, c: (row(b, c), v0 + ng + g)),
                  pl.BlockSpec((None, G, 1, CHUNK), lambda b, g, c: (g, 0, 0, row(b, c))),
                  pl.BlockSpec((None, G, 1, CHUNK), lambda b, g, c: (ng + g, 0, 0, row(b, c))),
                  pl.BlockSpec((CONV_WIDTH, G * dk), lambda b, g, c: (0, g)),
                  pl.BlockSpec((CONV_WIDTH, G * dk), lambda b, g, c: (0, ng + g)),
                  pl.BlockSpec((1, G * dv), lambda b, g, c: (0, g))],
        out_specs=pl.BlockSpec((CHUNK, G * dv), lambda b, g, c: (row(b, c), g)),
        out_shape=jax.ShapeDtypeStruct((M, H * dv), BF16),
        scratch_shapes=[pltpu.VMEM((G, dk, dv), F32), pltpu.VMEM((G, 1, dk), F32),
                        pltpu.VMEM((G, 1, LANES), F32),
                        pltpu.VMEM((G, CONV_TAIL, dk), F32), pltpu.VMEM((G, CONV_TAIL, dk), F32)],
        compiler_params=_cparams(3),
        name="mlstm_core",
    )(proj, proj, proj, proj, gates4, gates4, conv_w, conv_w, head_g.reshape(1, H * dv))


def _res_router_kernel(x_ref, y_ref, gp_ref, gn_ref, w2_ref, xo_ref, idx_ref, wt_ref):
    xn = x_ref[...] + _rms(y_ref[...].astype(F32), gp_ref[...])
    xo_ref[...] = xn
    hn = _rms(xn, gn_ref[...])
    hi = hn.astype(BF16)
    lo = (hn - hi.astype(F32)).astype(BF16)
    t = jnp.dot(hi, w2_ref[...], preferred_element_type=F32)
    logits = (t[:, :LANES] + t[:, LANES:]
              + jnp.dot(lo, w2_ref[:, :LANES], preferred_element_type=F32))
    lane = lax.broadcasted_iota(jnp.int32, logits.shape, 1)
    neg = -jnp.inf
    l1 = jnp.where(lane < N_EXPERTS, logits, neg)
    m1 = jnp.max(l1, axis=1, keepdims=True)
    i1 = jnp.min(jnp.where(l1 == m1, lane, LANES), axis=1, keepdims=True)
    l2 = jnp.where(lane == i1, neg, l1)
    m2 = jnp.max(l2, axis=1, keepdims=True)
    i2 = jnp.min(jnp.where(l2 == m2, lane, LANES), axis=1, keepdims=True)
    r = jnp.exp(m2 - m1)
    w1 = 1.0 / (1.0 + r)
    w2 = r / (1.0 + r)
    idx_ref[...] = jnp.where(lane == 0, i1, jnp.where(lane == 1, i2, 0))
    wt_ref[...] = jnp.where(lane == 0, w1, jnp.where(lane == 1, w2, 0.0))


def residual_router(x, y, g_post, g_next, w_router, tm=256):
    M, D = x.shape
    tm = _tile(tm, M)
    w_pad = jnp.pad(w_router, ((0, 0), (0, LANES - w_router.shape[1])))
    w_hi = w_pad.astype(BF16)
    w_lo = (w_pad - w_hi.astype(F32)).astype(BF16)
    row = pl.BlockSpec((tm, D), lambda i: (i, 0))
    vec = pl.BlockSpec((1, D), lambda i: (0, 0))
    out = pl.BlockSpec((tm, LANES), lambda i: (i, 0))
    return pl.pallas_call(
        _res_router_kernel, grid=(M // tm,),
        in_specs=[row, row, vec, vec, pl.BlockSpec((D, 2 * LANES), lambda i: (0, 0))],
        out_specs=[row, out, out],
        out_shape=[jax.ShapeDtypeStruct((M, D), F32),
                   jax.ShapeDtypeStruct((M, LANES), jnp.int32),
                   jax.ShapeDtypeStruct((M, LANES), F32)],
        compiler_params=_cparams(1), name="residual_router",
    )(x, y, g_post.reshape(1, D), g_next.reshape(1, D),
      jnp.concatenate([w_hi, w_lo], axis=1))


def _row_copy(src_hbm, src_row, dst_ref, dst_row, sem):
    return pltpu.make_async_copy(src_hbm.at[pl.ds(src_row, 1)],
                                 dst_ref.at[pl.ds(dst_row, 1)], sem)


def _dispatch_kernel(tok_ref, nt_ref, x_hbm, g_ref, o_ref, buf_ref, sem_ref, *, tm):
    i = pl.program_id(0)
    nt = nt_ref[0]
    n_blocks = tm // GATHER_ROWS

    def start_rows(tile, slot, blk):
        for q in range(GATHER_ROWS):
            r = blk * GATHER_ROWS + q
            _row_copy(x_hbm, tok_ref[tile * tm + r], buf_ref.at[slot], r,
                      sem_ref.at[slot]).start()

    def wait_rows(slot, blk):
        for q in range(GATHER_ROWS):
            _row_copy(x_hbm, 0, buf_ref.at[slot], blk * GATHER_ROWS + q,
                      sem_ref.at[slot]).wait()

    def norm_rows(slot, blk):
        rows = pl.ds(pl.multiple_of(blk * GATHER_ROWS, GATHER_ROWS), GATHER_ROWS)
        o_ref[rows, :] = _rms(buf_ref[slot, rows, :], g_ref[...]).astype(o_ref.dtype)

    def loop(body):
        lax.fori_loop(0, n_blocks, lambda blk, c: (body(blk), c)[1], 0, unroll=GATHER_UNROLL)

    ahead = GATHER_SLOTS - 1

    @pl.when(i == 0)
    def _():
        for t in range(ahead):
            @pl.when(t < nt)
            def _(t=t):
                loop(lambda blk: start_rows(t, t, blk))

    slot = i % GATHER_SLOTS

    @pl.when(i < nt)
    def _():
        loop(lambda blk: wait_rows(slot, blk))

    @pl.when(i + ahead < nt)
    def _():
        def body(blk):
            start_rows(i + ahead, (i + ahead) % GATHER_SLOTS, blk)
            norm_rows(slot, blk)
        loop(body)

    @pl.when((i < nt) & (i + ahead >= nt))
    def _():
        loop(lambda blk: norm_rows(slot, blk))

    @pl.when(i >= nt)
    def _():
        o_ref[...] = jnp.zeros_like(o_ref)


def moe_dispatch(x, g, token_of_slot, n_tiles, tm):
    T, D = x.shape
    P = token_of_slot.shape[0]
    return pl.pallas_call(
        functools.partial(_dispatch_kernel, tm=tm),
        grid_spec=pltpu.PrefetchScalarGridSpec(
            num_scalar_prefetch=2,
            grid=(P // tm,),
            in_specs=[pl.BlockSpec(memory_space=pl.ANY),
                      pl.BlockSpec((1, D), lambda i, tok, nt: (0, 0))],
            out_specs=pl.BlockSpec((tm, D), lambda i, tok, nt: (i, 0)),
            scratch_shapes=[pltpu.VMEM((GATHER_SLOTS, tm, D), F32),
                            pltpu.SemaphoreType.DMA((GATHER_SLOTS,))]),
        out_shape=jax.ShapeDtypeStruct((P, D), BF16),
        compiler_params=_cparams(1),
        name="moe_dispatch",
    )(token_of_slot, n_tiles, x, g.reshape(1, D))


def _weights_changed(te_ref, i):
    return (i == 0) | (te_ref[i] != te_ref[jnp.maximum(i - 1, 0)])


def _stage_copies(w_hbms, e, j, tn, stage_refs, sem_ref, parts=1):
    col = pl.multiple_of(j * tn, tn)
    out = []
    for n, (w, s) in enumerate(zip(w_hbms, stage_refs)):
        rows = s.shape[0] // parts
        for q in range(parts):
            band = pl.ds(q * rows, rows)
            out.append(pltpu.make_async_copy(w.at[e, band, pl.ds(col, tn)], s.at[band, :],
                                             sem_ref.at[n]))
    return out


def _restage_weights(te_ref, nx_ref, w_hbms, stage_refs, wb_refs, sem_ref, *, tn, nj):
    j = pl.program_id(0)
    i = pl.program_id(1)

    @pl.when((j == 0) & (i == 0))
    def _():
        for c in _stage_copies(w_hbms, te_ref[0], 0, tn, stage_refs, sem_ref, W_DMA_PARTS):
            c.start()

    @pl.when(_weights_changed(te_ref, i))
    def _():
        for c in _stage_copies(w_hbms, te_ref[i], j, tn, stage_refs, sem_ref):
            c.wait()
        for s, wb in zip(stage_refs, wb_refs):
            wb[...] = s[...].astype(BF16)
        nxt = nx_ref[i]
        last = nxt < 0
        e_next = jnp.where(last, te_ref[0], nxt)
        j_next = jnp.where(last, j + 1, j)

        @pl.when(j_next < nj)
        def _():
            for c in _stage_copies(w_hbms, e_next, j_next, tn, stage_refs, sem_ref,
                                   W_DMA_PARTS):
                c.start()


def _for_used_rows(n_used, o_ref, compute):
    tm = o_ref.shape[0]
    classes = sorted({min(c, tm) for c in MOE_ROW_CLASSES} | {tm})
    lo = 0
    for r in classes:
        @pl.when((n_used > lo) & (n_used <= r))
        def _(r=r):
            o_ref[:r, :] = compute(r).astype(o_ref.dtype)
            if r < tm:
                o_ref[r:, :] = jnp.zeros((tm - r, o_ref.shape[1]), o_ref.dtype)
        lo = r

    @pl.when(n_used == 0)
    def _():
        o_ref[...] = jnp.zeros_like(o_ref)


def _moe_up_kernel(te_ref, nt_ref, nx_ref, nu_ref, a_ref, wg_hbm, wu_hbm, o_ref,
                   sg_ref, su_ref, wgb_ref, wub_ref, sem_ref, *, tn, nj):
    _restage_weights(te_ref, nx_ref, (wg_hbm, wu_hbm), (sg_ref, su_ref),
                     (wgb_ref, wub_ref), sem_ref, tn=tn, nj=nj)

    def compute(r):
        a = a_ref[:r, :]
        g = jnp.dot(a, wgb_ref[...], preferred_element_type=F32)
        u = jnp.dot(a, wub_ref[...], preferred_element_type=F32)
        return g * _sigmoid(g) * u

    _for_used_rows(nu_ref[pl.program_id(1)], o_ref, compute)


def _moe_down_kernel(te_ref, nt_ref, nx_ref, nu_ref, a_ref, w_hbm, o_ref,
                     s_ref, wb_ref, sem_ref, *, tn, nj):
    _restage_weights(te_ref, nx_ref, (w_hbm,), (s_ref,), (wb_ref,), sem_ref, tn=tn, nj=nj)
    _for_used_rows(nu_ref[pl.program_id(1)], o_ref,
                   lambda r: jnp.dot(a_ref[:r, :], wb_ref[...], preferred_element_type=F32))


def _moe_grouped(kernel_fn, name, a, weights, plan, tm, tn, out_dtype):
    te, n_tiles, nxt, n_used = plan
    P, K = a.shape
    N = weights[0].shape[2]
    tn = _tile(tn, N)
    nj = N // tn
    used = lambda i, nt: jnp.minimum(i, nt[0] - 1)
    n_w = len(weights)
    return pl.pallas_call(
        functools.partial(kernel_fn, tn=tn, nj=nj),
        grid_spec=pltpu.PrefetchScalarGridSpec(
            num_scalar_prefetch=4,
            grid=(nj, P // tm),
            in_specs=[pl.BlockSpec((tm, K), lambda j, i, te, nt, nx, nu: (used(i, nt), 0))]
                     + [pl.BlockSpec(memory_space=pl.ANY)] * n_w,
            out_specs=pl.BlockSpec((tm, tn), lambda j, i, te, nt, nx, nu: (i, j)),
            scratch_shapes=[pltpu.VMEM((K, tn), F32)] * n_w + [pltpu.VMEM((K, tn), BF16)] * n_w
                           + [pltpu.SemaphoreType.DMA((n_w,))]),
        out_shape=jax.ShapeDtypeStruct((P, N), out_dtype),
        compiler_params=_cparams(2),
        name=name,
    )(te, n_tiles, nxt, n_used, a, *weights)


def moe_up(xs, wg, wu, plan, tm, tn=512):
    return _moe_grouped(_moe_up_kernel, "moe_up", xs, (wg, wu), plan, tm, tn, BF16)


def moe_down(hs, wd, plan, tm, tn=1024):
    return _moe_grouped(_moe_down_kernel, "moe_down", hs, (wd,), plan, tm, tn, F32)


def _combine_kernel(slot_ref, x_ref, wt_ref, gp_ref, ys_hbm, xo_ref, buf_ref, sem_ref, *, tm):
    i = pl.program_id(0)
    n_blocks = tm // COMBINE_ROWS

    def start_rows(tile, s, blk):
        for q in range(COMBINE_ROWS):
            r = blk * COMBINE_ROWS + q
            for k in range(TOP_K):
                _row_copy(ys_hbm, slot_ref[(tile * tm + r) * TOP_K + k], buf_ref.at[s, k], r,
                          sem_ref.at[s]).start()

    def wait_rows(s, blk):
        for q in range(COMBINE_ROWS):
            for k in range(TOP_K):
                _row_copy(ys_hbm, 0, buf_ref.at[s, k], blk * COMBINE_ROWS + q,
                          sem_ref.at[s]).wait()

    def combine_rows(s, blk):
        rows = pl.ds(pl.multiple_of(blk * COMBINE_ROWS, COMBINE_ROWS), COMBINE_ROWS)
        y = (wt_ref[rows, 0:1] * buf_ref[s, 0, rows, :]
             + wt_ref[rows, 1:2] * buf_ref[s, 1, rows, :])
        xo_ref[rows, :] = x_ref[rows, :] + _rms(y, gp_ref[...])

    def loop(body):
        lax.fori_loop(0, n_blocks, lambda blk, c: (body(blk), c)[1], 0, unroll=COMBINE_UNROLL)

    n = pl.num_programs(0)
    ahead = GATHER_SLOTS - 1

    @pl.when(i == 0)
    def _():
        for t in range(ahead):
            @pl.when(t < n)
            def _(t=t):
                loop(lambda blk: start_rows(t, t, blk))

    s = i % GATHER_SLOTS
    loop(lambda blk: wait_rows(s, blk))

    @pl.when(i + ahead < n)
    def _():
        def body(blk):
            start_rows(i + ahead, (i + ahead) % GATHER_SLOTS, blk)
            combine_rows(s, blk)
        loop(body)

    @pl.when(i + ahead >= n)
    def _():
        loop(lambda blk: combine_rows(s, blk))


def moe_combine_residual(x, ys, slot, wts, g_post, tm=256):
    M, D = x.shape
    tm = _tile(tm, M)
    row = pl.BlockSpec((tm, D), lambda i, sl: (i, 0))
    return pl.pallas_call(
        functools.partial(_combine_kernel, tm=tm),
        grid_spec=pltpu.PrefetchScalarGridSpec(
            num_scalar_prefetch=1,
            grid=(M // tm,),
            in_specs=[row, pl.BlockSpec((tm, LANES), lambda i, sl: (i, 0)),
                      pl.BlockSpec((1, D), lambda i, sl: (0, 0)),
                      pl.BlockSpec(memory_space=pl.ANY)],
            out_specs=row,
            scratch_shapes=[pltpu.VMEM((GATHER_SLOTS, TOP_K, tm, D), F32),
                            pltpu.SemaphoreType.DMA((GATHER_SLOTS,))]),
        out_shape=jax.ShapeDtypeStruct((M, D), F32),
        compiler_params=_cparams(1), name="moe_combine_residual",
    )(slot, x, wts, g_post.reshape(1, D), ys)


def _moe_plan(idx, n_tokens, tm):
    e_flat = idx.reshape(-1)
    onehot = (e_flat[:, None] == jnp.arange(N_EXPERTS)[None, :]).astype(jnp.int32)
    counts = jnp.sum(onehot, axis=0)
    rank = jnp.sum((jnp.cumsum(onehot, axis=0) - onehot) * onehot, axis=1)
    tiles_per = (counts + tm - 1) // tm
    tile_end = jnp.cumsum(tiles_per)
    tile_start = tile_end - tiles_per
    slot = (tile_start[e_flat] * tm + rank).astype(jnp.int32)
    n_tiles_max = (n_tokens * TOP_K) // tm + N_EXPERTS
    n_tiles = tile_end[-1]
    tile_ids = jnp.minimum(jnp.arange(n_tiles_max), n_tiles - 1)
    expert_of = lambda t: jnp.minimum(
        jnp.sum((tile_end[None, :] <= t[:, None]).astype(jnp.int32), axis=1), N_EXPERTS - 1)
    te = expert_of(tile_ids)
    group_end = tile_end[te]
    nxt = jnp.where(group_end < n_tiles, expert_of(jnp.minimum(group_end, n_tiles - 1)), -1)
    all_ids = jnp.arange(n_tiles_max)
    n_used = jnp.where(all_ids < n_tiles,
                       jnp.clip(counts[te] - (all_ids - tile_start[te]) * tm, 0, tm), 0)
    token_of_pair = jnp.arange(n_tokens * TOP_K, dtype=jnp.int32) // TOP_K
    token_of_slot = jnp.zeros((n_tiles_max * tm,), jnp.int32).at[slot].set(token_of_pair)
    plan = (te.astype(jnp.int32), n_tiles.reshape(1).astype(jnp.int32), nxt.astype(jnp.int32),
            n_used.astype(jnp.int32))
    return slot, token_of_slot, plan


def kernel(x, mem, mem_norm, mem_kv, l0_norm_mix_pre, l0_mix_in, l0_sgu_ln_g, l0_sgu_ln_b, l0_sgu_w, l0_sgu_b, l0_mix_out, l0_norm_mix_post, l0_norm_x_pre, l0_xq, l0_xo, l0_norm_x_post, l0_norm_ffn_pre, l0_ffn_gate, l0_ffn_up, l0_ffn_down, l0_norm_ffn_post, l1_norm_mix_pre, l1_mix_in, l1_gate_b, l1_conv, l1_head_norm, l1_mix_out, l1_norm_mix_post, l1_norm_x_pre, l1_xq, l1_xo, l1_norm_x_post, l1_norm_ffn_pre, l1_router, l1_moe_gate, l1_moe_up, l1_moe_down, l1_norm_ffn_post):
    B, S, D = x.shape
    T = B * S
    n_mem = mem.shape[1]
    xf = x.reshape(T, D)

    memn = rmsnorm_rows(mem.reshape(B * n_mem, D), mem_norm)
    kv = matmul_fullk(memn, mem_kv, BF16).reshape(B, n_mem, 2 * D)

    hn = rmsnorm_rows(xf, l0_norm_mix_pre)
    z = matmul_fullk(hn, l0_mix_in, BF16, act="gelu")
    y = sgu_mix(z, l0_sgu_ln_g, l0_sgu_ln_b, l0_sgu_w, l0_sgu_b)
    y = matmul_fullk(y, l0_mix_out, BF16)
    xf, hn = residual_norm(xf, y, l0_norm_mix_post, l0_norm_x_pre)
    o = cross_attention(matmul_fullk(hn, l0_xq, BF16), kv, B)
    y = matmul_fullk(o, l0_xo, BF16)
    xf, hn = residual_norm(xf, y, l0_norm_x_post, l0_norm_ffn_pre)
    hmid = swiglu_up(hn, l0_ffn_gate, l0_ffn_up)
    y = matmul_tiledk(hmid, l0_ffn_down, BF16)
    xf, hn = residual_norm(xf, y, l0_norm_ffn_post, l1_norm_mix_pre)

    n_gates = 2 * MLSTM_HEADS
    n_main = l1_mix_in.shape[1] - n_gates
    w_in_t = l1_mix_in.T
    proj = matmul_fullk(hn, w_in_t, BF16, w_rows=(0, n_main))
    gates_t = mlstm_gates(hn, w_in_t[n_main:], l1_gate_b)
    y = mlstm_core(proj, gates_t, l1_conv, l1_head_norm, B)
    y = matmul_fullk(y, l1_mix_out, BF16)
    xf, hn = residual_norm(xf, y, l1_norm_mix_post, l1_norm_x_pre)
    o = cross_attention(matmul_fullk(hn, l1_xq, BF16), kv, B)
    y = matmul_fullk(o, l1_xo, BF16)
    xf, idx, wts = residual_router(xf, y, l1_norm_x_post, l1_norm_ffn_pre, l1_router)
    tm = min(MOE_TM, T)
    slot, token_of_slot, plan = _moe_plan(idx[:, :TOP_K], T, tm)
    xs = moe_dispatch(xf, l1_norm_ffn_pre, token_of_slot, plan[1], tm)
    hs = moe_up(xs, l1_moe_gate, l1_moe_up, plan, tm)
    ys = moe_down(hs, l1_moe_down, plan, tm)
    xf = moe_combine_residual(xf, ys, slot, wts, l1_norm_ffn_post)
    return xf.reshape(B, S, D)
```

```python
import functools

import jax
import jax.numpy as jnp
from jax import lax
from jax.experimental import pallas as pl
from jax.experimental.pallas import tpu as pltpu

F32 = jnp.float32
BF16 = jnp.bfloat16

EPS = 1e-6
CHUNK = 128
SGU_GROUPS = 8
MLSTM_HEADS = 8
CONV_WIDTH = 4
GATE_CAP = 15.0
X_HEADS = 4
N_EXPERTS = 8
TOP_K = 2
LANES = 128
MOE_TM = 512
MOE_ROW_CLASSES = (128, 256)
GATHER_ROWS = 16
GATHER_UNROLL = 2
GATHER_SLOTS = 3
COMBINE_ROWS = 8
COMBINE_UNROLL = 4
MLSTM_HEADS_PER_STEP = 2
CONV_TAIL = 8
VMEM_LIMIT = 56 * 1024 * 1024


def _cparams(n_axes, vmem=VMEM_LIMIT):
    return pltpu.CompilerParams(
        dimension_semantics=("arbitrary",) * n_axes, vmem_limit_bytes=vmem)


def _tile(pref, dim):
    t = min(pref, dim)
    assert dim % t == 0, (pref, dim)
    return t


def _rms(x, g):
    return x * lax.rsqrt(jnp.mean(x * x, axis=-1, keepdims=True) + EPS) * g


def _sigmoid(x):
    return 1.0 / (1.0 + jnp.exp(-x))


def _rms_kernel(x_ref, g_ref, o_ref):
    o_ref[...] = _rms(x_ref[...], g_ref[...]).astype(o_ref.dtype)


def rmsnorm_rows(x, g, tm=256):
    M, D = x.shape
    tm = _tile(tm, M)
    return pl.pallas_call(
        _rms_kernel,
        grid=(M // tm,),
        in_specs=[pl.BlockSpec((tm, D), lambda i: (i, 0)),
                  pl.BlockSpec((1, D), lambda i: (0, 0))],
        out_specs=pl.BlockSpec((tm, D), lambda i: (i, 0)),
        out_shape=jax.ShapeDtypeStruct((M, D), BF16),
        compiler_params=_cparams(1),
        name="rmsnorm_rows",
    )(x, g.reshape(1, D))


def _gelu(x):
    return 0.5 * x * (1.0 + lax.erf(x * 0.7071067811865476))


def _stream_weights(w_hbms, stage_refs, wb_refs, sem_ref, *, tn, nj, transposed, first=0):
    j = pl.program_id(0)
    i = pl.program_id(1)

    def copies(jj):
        off = pl.multiple_of(first + jj * tn, tn)
        return [pltpu.make_async_copy(
                    w.at[pl.ds(off, tn), :] if transposed else w.at[:, pl.ds(off, tn)],
                    s, sem_ref.at[n])
                for n, (w, s) in enumerate(zip(w_hbms, stage_refs))]

    @pl.when((j == 0) & (i == 0))
    def _():
        for c in copies(0):
            c.start()

    @pl.when(i == 0)
    def _():
        for c in copies(j):
            c.wait()
        for s, wb in zip(stage_refs, wb_refs):
            wb[...] = (s[...].T if transposed else s[...]).astype(BF16)

        @pl.when(j + 1 < nj)
        def _():
            for c in copies(j + 1):
                c.start()


def _mm_kernel(a_ref, w_hbm, o_ref, stage_ref, wb_ref, sem_ref, *, act, tn, nj, transposed,
               first):
    _stream_weights((w_hbm,), (stage_ref,), (wb_ref,), sem_ref,
                    tn=tn, nj=nj, transposed=transposed, first=first)
    acc = jnp.dot(a_ref[...], wb_ref[...], preferred_element_type=F32)
    if act == "gelu":
        acc = _gelu(acc)
    o_ref[...] = acc.astype(o_ref.dtype)


def matmul_fullk(a, w, out_dtype, act=None, w_rows=None, tm=1024, tn=1024):
    M, K = a.shape
    transposed = w_rows is not None
    first, N = w_rows if transposed else (0, w.shape[1])
    tm = _tile(tm, M)
    tn = _tile(tn, N)
    assert first % tn == 0
    nj = N // tn
    stage_block = (tn, K) if transposed else (K, tn)
    return pl.pallas_call(
        functools.partial(_mm_kernel, act=act, tn=tn, nj=nj, transposed=transposed,
                          first=first),
        grid=(nj, M // tm),
        in_specs=[pl.BlockSpec((tm, K), lambda j, i: (i, 0)),
                  pl.BlockSpec(memory_space=pl.ANY)],
        out_specs=pl.BlockSpec((tm, tn), lambda j, i: (i, j)),
        out_shape=jax.ShapeDtypeStruct((M, N), out_dtype),
        scratch_shapes=[pltpu.VMEM(stage_block, F32), pltpu.VMEM((K, tn), BF16),
                        pltpu.SemaphoreType.DMA((1,))],
        compiler_params=_cparams(2),
        name="matmul_fullk" + ("_" + act if act else "") + ("_t" if transposed else ""),
    )(a, w)


def _swiglu_kernel(a_ref, wg_hbm, wu_hbm, o_ref, sg_ref, su_ref, wb_ref, sem_ref, *, tn, nj):
    _stream_weights((wg_hbm, wu_hbm), (sg_ref, su_ref),
                    (wb_ref.at[:, :tn], wb_ref.at[:, tn:]), sem_ref,
                    tn=tn, nj=nj, transposed=False)
    gu = jnp.dot(a_ref[...], wb_ref[...], preferred_element_type=F32)
    g = gu[:, :tn]
    o_ref[...] = (g * _sigmoid(g) * gu[:, tn:]).astype(o_ref.dtype)


def swiglu_up(a, wg, wu, tm=1024, tn=512):
    M, K = a.shape
    F = wg.shape[1]
    tm = _tile(tm, M)
    tn = _tile(tn, F)
    nj = F // tn
    hbm = pl.BlockSpec(memory_space=pl.ANY)
    return pl.pallas_call(
        functools.partial(_swiglu_kernel, tn=tn, nj=nj),
        grid=(nj, M // tm),
        in_specs=[pl.BlockSpec((tm, K), lambda j, i: (i, 0)), hbm, hbm],
        out_specs=pl.BlockSpec((tm, tn), lambda j, i: (i, j)),
        out_shape=jax.ShapeDtypeStruct((M, F), BF16),
        scratch_shapes=[pltpu.VMEM((K, tn), F32), pltpu.VMEM((K, tn), F32),
                        pltpu.VMEM((K, 2 * tn), BF16), pltpu.SemaphoreType.DMA((2,))],
        compiler_params=_cparams(2),
        name="swiglu_up",
    )(a, wg, wu)


def _mmk_kernel(a_ref, w_ref, o_ref, acc_ref, *, nk):
    k = pl.program_id(2)

    @pl.when(k == 0)
    def _():
        acc_ref[...] = jnp.zeros_like(acc_ref)

    acc_ref[...] += jnp.dot(a_ref[...], w_ref[...].astype(BF16),
                            preferred_element_type=F32)

    @pl.when(k == nk - 1)
    def _():
        o_ref[...] = acc_ref[...].astype(o_ref.dtype)


def matmul_tiledk(a, w, out_dtype, tm=2048, tn=1024, tk=1024):
    M, K = a.shape
    N = w.shape[1]
    tm, tn, tk = _tile(tm, M), _tile(tn, N), _tile(tk, K)
    nk = K // tk
    return pl.pallas_call(
        functools.partial(_mmk_kernel, nk=nk),
        grid=(N // tn, M // tm, nk),
        in_specs=[pl.BlockSpec((tm, tk), lambda j, i, k: (i, k)),
                  pl.BlockSpec((tk, tn), lambda j, i, k: (k, j))],
        out_specs=pl.BlockSpec((tm, tn), lambda j, i, k: (i, j)),
        out_shape=jax.ShapeDtypeStruct((M, N), out_dtype),
        scratch_shapes=[pltpu.VMEM((tm, tn), F32)],
        compiler_params=_cparams(3),
        name="matmul_tiledk",
    )(a, w)


def _res_kernel(x_ref, y_ref, gp_ref, gn_ref, xo_ref, hn_ref):
    xn = x_ref[...] + _rms(y_ref[...].astype(F32), gp_ref[...])
    xo_ref[...] = xn
    hn_ref[...] = _rms(xn, gn_ref[...]).astype(hn_ref.dtype)


def residual_norm(x, y, g_post, g_next, tm=256):
    M, D = x.shape
    tm = _tile(tm, M)
    row = pl.BlockSpec((tm, D), lambda i: (i, 0))
    vec = pl.BlockSpec((1, D), lambda i: (0, 0))
    return pl.pallas_call(
        _res_kernel, grid=(M // tm,),
        in_specs=[row, row, vec, vec], out_specs=[row, row],
        out_shape=[jax.ShapeDtypeStruct((M, D), F32), jax.ShapeDtypeStruct((M, D), BF16)],
        compiler_params=_cparams(1), name="residual_norm",
    )(x, y, g_post.reshape(1, D), g_next.reshape(1, D))


def _sgu_kernel(u_ref, v_ref, lg_ref, lb_ref, w_ref, bt_ref, o_ref, wm_ref, *, groups):
    @pl.when(pl.program_id(0) == 0)
    def _():
        t = lax.broadcasted_iota(jnp.int32, (CHUNK, CHUNK), 0)
        s = lax.broadcasted_iota(jnp.int32, (CHUNK, CHUNK), 1)
        for g in range(groups):
            wm_ref[g] = jnp.where(t >= s, w_ref[g], 0.0).astype(BF16)

    v = v_ref[...].astype(F32)
    vc = v - jnp.mean(v, axis=-1, keepdims=True)
    vn = vc * lax.rsqrt(jnp.mean(vc * vc, axis=-1, keepdims=True) + EPS)
    vn = (vn * lg_ref[...] + lb_ref[...]).astype(BF16)
    tm, width = vn.shape
    gd = width // groups
    for c in range(tm // CHUNK):
        rows = slice(c * CHUNK, (c + 1) * CHUNK)
        for g in range(groups):
            cols = slice(g * gd, (g + 1) * gd)
            mixed = jnp.dot(wm_ref[g], vn[rows, cols], preferred_element_type=F32)
            mixed = mixed + bt_ref[:, g:g + 1]
            o_ref[rows, cols] = (u_ref[rows, cols].astype(F32) * mixed).astype(o_ref.dtype)


def sgu_mix(z, ln_g, ln_b, sgu_w, sgu_b, tm=256):
    M, W2 = z.shape
    W = W2 // 2
    G = sgu_w.shape[0]
    tm = _tile(tm, M)
    return pl.pallas_call(
        functools.partial(_sgu_kernel, groups=G),
        grid=(M // tm,),
        in_specs=[pl.BlockSpec((tm, W), lambda i: (i, 0)),
                  pl.BlockSpec((tm, W), lambda i: (i, 1)),
                  pl.BlockSpec((1, W), lambda i: (0, 0)),
                  pl.BlockSpec((1, W), lambda i: (0, 0)),
                  pl.BlockSpec((G, CHUNK, CHUNK), lambda i: (0, 0, 0)),
                  pl.BlockSpec((CHUNK, G), lambda i: (0, 0))],
        out_specs=pl.BlockSpec((tm, W), lambda i: (i, 0)),
        out_shape=jax.ShapeDtypeStruct((M, W), BF16),
        scratch_shapes=[pltpu.VMEM((G, CHUNK, CHUNK), BF16)],
        compiler_params=_cparams(1),
        name="sgu_mix",
    )(z, z, ln_g.reshape(1, W), ln_b.reshape(1, W), sgu_w, sgu_b.T)


def _xattn_kernel(q_ref, k_ref, v_ref, o_ref, *, heads, scale):
    hd = q_ref.shape[-1] // heads
    for h in range(heads):
        cols = slice(h * hd, (h + 1) * hd)
        s = lax.dot_general(q_ref[:, cols], k_ref[0, :, cols], (((1,), (1,)), ((), ())),
                            preferred_element_type=F32) * scale
        p = jnp.exp(s - jnp.max(s, axis=-1, keepdims=True))
        p = p / jnp.sum(p, axis=-1, keepdims=True)
        o = jnp.dot(p.astype(BF16), v_ref[0, :, cols], preferred_element_type=F32)
        o_ref[:, cols] = o.astype(o_ref.dtype)


def cross_attention(q, kv, batch, tm=512):
    M, D = q.shape
    S = M // batch
    n_mem = kv.shape[1]
    tm = _tile(tm, S)
    spt = S // tm
    return pl.pallas_call(
        functools.partial(_xattn_kernel, heads=X_HEADS, scale=(D // X_HEADS) ** -0.5),
        grid=(batch, spt),
        in_specs=[pl.BlockSpec((tm, D), lambda b, m: (b * spt + m, 0)),
                  pl.BlockSpec((1, n_mem, D), lambda b, m: (b, 0, 0)),
                  pl.BlockSpec((1, n_mem, D), lambda b, m: (b, 0, 1))],
        out_specs=pl.BlockSpec((tm, D), lambda b, m: (b * spt + m, 0)),
        out_shape=jax.ShapeDtypeStruct((M, D), BF16),
        compiler_params=_cparams(2),
        name="cross_attention",
    )(q, kv, kv)


def _gates_kernel(hn_ref, w_ref, b_ref, o_ref):
    acc = lax.dot_general(w_ref[...].astype(BF16), hn_ref[...], (((1,), (1,)), ((), ())),
                          preferred_element_type=F32)
    o_ref[...] = GATE_CAP * jnp.tanh((acc + b_ref[...]) / GATE_CAP)


def mlstm_gates(hn, w_gates_t, gate_b, tm=512):
    M, D = hn.shape
    n_gates = w_gates_t.shape[0]
    tm = _tile(tm, M)
    return pl.pallas_call(
        _gates_kernel,
        grid=(M // tm,),
        in_specs=[pl.BlockSpec((tm, D), lambda i: (i, 0)),
                  pl.BlockSpec((n_gates, D), lambda i: (0, 0)),
                  pl.BlockSpec((n_gates, 1), lambda i: (0, 0))],
        out_specs=pl.BlockSpec((n_gates, tm), lambda i: (0, i)),
        out_shape=jax.ShapeDtypeStruct((n_gates, M), F32),
        compiler_params=_cparams(1),
        name="mlstm_gates",
    )(hn, w_gates_t, gate_b.reshape(n_gates, 1))


def _conv_silu(x, tail_ref, w):
    prev = tail_ref[...]
    rid = lax.broadcasted_iota(jnp.int32, prev.shape, 0)
    acc = x * w[CONV_WIDTH - 1:CONV_WIDTH, :]
    for r in range(1, CONV_WIDTH):
        rolled = pltpu.roll(x, r, 0)
        head = jnp.where(rid < r, pltpu.roll(prev, r, 0), rolled[:CONV_TAIL])
        shifted = jnp.concatenate([head, rolled[CONV_TAIL:]], axis=0)
        acc = acc + shifted * w[CONV_WIDTH - 1 - r:CONV_WIDTH - r, :]
    tail_ref[...] = x[x.shape[0] - CONV_TAIL:, :]
    return acc * _sigmoid(acc)


def _mlstm_kernel(q_ref, k_ref, v_ref, o_ref, ig_ref, fg_ref, cq_ref, ck_ref, hg_ref,
                  y_ref, s_ref, n_ref, m_ref, qt_ref, kt_ref, *, heads):
    @pl.when(pl.program_id(2) == 0)
    def _():
        s_ref[...] = jnp.zeros_like(s_ref)
        n_ref[...] = jnp.zeros_like(n_ref)
        m_ref[...] = jnp.zeros_like(m_ref)
        qt_ref[...] = jnp.zeros_like(qt_ref)
        kt_ref[...] = jnp.zeros_like(kt_ref)

    dk = q_ref.shape[-1] // heads
    dv = v_ref.shape[-1] // heads
    for h in range(heads):
        qc = slice(h * dk, (h + 1) * dk)
        vc = slice(h * dv, (h + 1) * dv)
        q = _conv_silu(q_ref[:, qc].astype(F32), qt_ref.at[h], cq_ref[:, qc]) * dk ** -0.5
        k = _conv_silu(k_ref[:, qc].astype(F32), kt_ref.at[h], ck_ref[:, qc])
        _mlstm_head(q, k, v_ref[:, vc], o_ref[:, vc], ig_ref[h], fg_ref[h], hg_ref[:, vc],
                    y_ref.at[:, vc], s_ref.at[h], n_ref.at[h], m_ref.at[h])


def _mlstm_head(q, k, vb, ob, i_row, f_row, hg, y_ref, s_ref, n_ref, m_ref):
    L = CHUNK
    qb = q.astype(BF16)
    kb = k.astype(BF16)
    v = vb.astype(F32)

    t_id = lax.broadcasted_iota(jnp.int32, (L, L), 0)
    s_id = lax.broadcasted_iota(jnp.int32, (L, L), 1)
    eye = t_id == s_id
    causal = t_id >= s_id

    logf_row = -(jnp.maximum(-f_row, 0.0) + jnp.log1p(jnp.exp(-jnp.abs(f_row))))
    logf_b = jnp.broadcast_to(logf_row, (L, L))
    i_b = jnp.broadcast_to(i_row, (L, L))
    logf_col = jnp.sum(jnp.where(eye, logf_b, 0.0), axis=1, keepdims=True)
    i_col = jnp.sum(jnp.where(eye, i_b, 0.0), axis=1, keepdims=True)
    b_col = jnp.sum(jnp.where(causal, logf_b, 0.0), axis=1, keepdims=True)
    b_row = jnp.sum(jnp.where(t_id <= s_id, jnp.broadcast_to(logf_col, (L, L)), 0.0),
                    axis=0, keepdims=True)
    g_tot = jnp.sum(logf_row, axis=1, keepdims=True)
    m0 = m_ref[:, 0:1]

    d_log = jnp.where(causal, b_col - b_row + i_row, -jnp.inf)
    m_inter = b_col + m0
    m_t = jnp.maximum(m_inter, jnp.max(d_log, axis=1, keepdims=True))
    qk = lax.dot_general(qb, kb, (((1,), (1,)), ((), ())), preferred_element_type=F32)
    scores = qk * jnp.exp(d_log - m_t)
    inter = jnp.exp(m_inter - m_t)
    num = (jnp.dot(scores.astype(BF16), vb, preferred_element_type=F32)
           + inter * jnp.dot(qb, s_ref[...].astype(BF16), preferred_element_type=F32))
    den = (jnp.sum(scores, axis=1, keepdims=True)
           + inter * jnp.sum(q * n_ref[...], axis=1, keepdims=True))
    h = num / jnp.maximum(jnp.abs(den), jnp.exp(-m_t))
    y_ref[...] = (_sigmoid(ob.astype(F32)) * _rms(h, hg)).astype(y_ref.dtype)

    w_log = g_tot - b_col + i_col
    a = jnp.max(w_log, axis=0, keepdims=True)
    w = jnp.exp(w_log - a)
    s_loc = lax.dot_general(kb, (w * v).astype(BF16), (((0,), (0,)), ((), ())),
                            preferred_element_type=F32)
    n_loc = jnp.sum(w * k, axis=0, keepdims=True)
    m_new = jnp.maximum(g_tot + m0, a)
    sc_prev = jnp.exp(g_tot + m0 - m_new)
    sc_loc = jnp.exp(a - m_new)
    s_ref[...] = sc_prev * s_ref[...] + sc_loc * s_loc
    n_ref[...] = sc_prev * n_ref[...] + sc_loc * n_loc
    m_ref[...] = jnp.broadcast_to(m_new, m_ref.shape)


def mlstm_core(proj, gates_t, conv_w, head_g, batch):
    M = proj.shape[0]
    H = MLSTM_HEADS
    G = MLSTM_HEADS_PER_STEP
    assert H % G == 0
    ng = H // G
    qkw = conv_w.shape[1]
    dk = qkw // (2 * H)
    dv = head_g.shape[0] // H
    nc = M // batch // CHUNK
    assert qkw % (G * dv) == 0
    v0 = qkw // (G * dv)
    gates4 = gates_t.reshape(2 * ng, G, 1, M)
    row = lambda b, c: b * nc + c
    return pl.pallas_call(
        functools.partial(_mlstm_kernel, heads=G),
        grid=(batch, ng, nc),
        in_specs=[pl.BlockSpec((CHUNK, G * dk), lambda b, g, c: (row(b, c), g)),
                  pl.BlockSpec((CHUNK, G * dk), lambda b, g, c: (row(b, c), ng + g)),
                  pl.BlockSpec((CHUNK, G * dv), lambda b, g, c: (row(b, c), v0 + g)),
                  pl.BlockSpec((CHUNK, G * dv), lambda b, g, c: (row(b, c), v0 + ng + g)),
                  pl.BlockSpec((None, G, 1, CHUNK), lambda b, g, c: (g, 0, 0, row(b, c))),
                  pl.BlockSpec((None, G, 1, CHUNK), lambda b, g, c: (ng + g, 0, 0, row(b, c))),
                  pl.BlockSpec((CONV_WIDTH, G * dk), lambda b, g, c: (0, g)),
                  pl.BlockSpec((CONV_WIDTH, G * dk), lambda b, g, c: (0, ng + g)),
                  pl.BlockSpec((1, G * dv), lambda b, g, c: (0, g))],
        out_specs=pl.BlockSpec((CHUNK, G * dv), lambda b, g, c: (row(b, c), g)),
        out_shape=jax.ShapeDtypeStruct((M, H * dv), BF16),
        scratch_shapes=[pltpu.VMEM((G, dk, dv), F32), pltpu.VMEM((G, 1, dk), F32),
                        pltpu.VMEM((G, 1, LANES), F32),
                        pltpu.VMEM((G, CONV_TAIL, dk), F32), pltpu.VMEM((G, CONV_TAIL, dk), F32)],
        compiler_params=_cparams(3),
        name="mlstm_core",
    )(proj, proj, proj, proj, gates4, gates4, conv_w, conv_w, head_g.reshape(1, H * dv))


def _res_router_kernel(x_ref, y_ref, gp_ref, gn_ref, w2_ref, xo_ref, idx_ref, wt_ref):
    xn = x_ref[...] + _rms(y_ref[...].astype(F32), gp_ref[...])
    xo_ref[...] = xn
    hn = _rms(xn, gn_ref[...])
    hi = hn.astype(BF16)
    lo = (hn - hi.astype(F32)).astype(BF16)
    t = jnp.dot(hi, w2_ref[...], preferred_element_type=F32)
    logits = (t[:, :LANES] + t[:, LANES:]
              + jnp.dot(lo, w2_ref[:, :LANES], preferred_element_type=F32))
    lane = lax.broadcasted_iota(jnp.int32, logits.shape, 1)
    neg = -jnp.inf
    l1 = jnp.where(lane < N_EXPERTS, logits, neg)
    m1 = jnp.max(l1, axis=1, keepdims=True)
    i1 = jnp.min(jnp.where(l1 == m1, lane, LANES), axis=1, keepdims=True)
    l2 = jnp.where(lane == i1, neg, l1)
    m2 = jnp.max(l2, axis=1, keepdims=True)
    i2 = jnp.min(jnp.where(l2 == m2, lane, LANES), axis=1, keepdims=True)
    r = jnp.exp(m2 - m1)
    w1 = 1.0 / (1.0 + r)
    w2 = r / (1.0 + r)
    idx_ref[...] = jnp.where(lane == 0, i1, jnp.where(lane == 1, i2, 0))
    wt_ref[...] = jnp.where(lane == 0, w1, jnp.where(lane == 1, w2, 0.0))


def residual_router(x, y, g_post, g_next, w_router, tm=256):
    M, D = x.shape
    tm = _tile(tm, M)
    w_pad = jnp.pad(w_router, ((0, 0), (0, LANES - w_router.shape[1])))
    w_hi = w_pad.astype(BF16)
    w_lo = (w_pad - w_hi.astype(F32)).astype(BF16)
    row = pl.BlockSpec((tm, D), lambda i: (i, 0))
    vec = pl.BlockSpec((1, D), lambda i: (0, 0))
    out = pl.BlockSpec((tm, LANES), lambda i: (i, 0))
    return pl.pallas_call(
        _res_router_kernel, grid=(M // tm,),
        in_specs=[row, row, vec, vec, pl.BlockSpec((D, 2 * LANES), lambda i: (0, 0))],
        out_specs=[row, out, out],
        out_shape=[jax.ShapeDtypeStruct((M, D), F32),
                   jax.ShapeDtypeStruct((M, LANES), jnp.int32),
                   jax.ShapeDtypeStruct((M, LANES), F32)],
        compiler_params=_cparams(1), name="residual_router",
    )(x, y, g_post.reshape(1, D), g_next.reshape(1, D),
      jnp.concatenate([w_hi, w_lo], axis=1))


def _row_copy(src_hbm, src_row, dst_ref, dst_row, sem):
    return pltpu.make_async_copy(src_hbm.at[pl.ds(src_row, 1)],
                                 dst_ref.at[pl.ds(dst_row, 1)], sem)


def _dispatch_kernel(tok_ref, nt_ref, x_hbm, g_ref, o_ref, buf_ref, sem_ref, *, tm):
    i = pl.program_id(0)
    nt = nt_ref[0]
    n_blocks = tm // GATHER_ROWS

    def start_rows(tile, slot, blk):
        for q in range(GATHER_ROWS):
            r = blk * GATHER_ROWS + q
            _row_copy(x_hbm, tok_ref[tile * tm + r], buf_ref.at[slot], r,
                      sem_ref.at[slot]).start()

    def wait_rows(slot, blk):
        for q in range(GATHER_ROWS):
            _row_copy(x_hbm, 0, buf_ref.at[slot], blk * GATHER_ROWS + q,
                      sem_ref.at[slot]).wait()

    def norm_rows(slot, blk):
        rows = pl.ds(pl.multiple_of(blk * GATHER_ROWS, GATHER_ROWS), GATHER_ROWS)
        o_ref[rows, :] = _rms(buf_ref[slot, rows, :], g_ref[...]).astype(o_ref.dtype)

    def loop(body):
        lax.fori_loop(0, n_blocks, lambda blk, c: (body(blk), c)[1], 0, unroll=GATHER_UNROLL)

    ahead = GATHER_SLOTS - 1

    @pl.when(i == 0)
    def _():
        for t in range(ahead):
            @pl.when(t < nt)
            def _(t=t):
                loop(lambda blk: start_rows(t, t, blk))

    slot = i % GATHER_SLOTS

    @pl.when(i < nt)
    def _():
        loop(lambda blk: wait_rows(slot, blk))

    @pl.when(i + ahead < nt)
    def _():
        def body(blk):
            start_rows(i + ahead, (i + ahead) % GATHER_SLOTS, blk)
            norm_rows(slot, blk)
        loop(body)

    @pl.when((i < nt) & (i + ahead >= nt))
    def _():
        loop(lambda blk: norm_rows(slot, blk))

    @pl.when(i >= nt)
    def _():
        o_ref[...] = jnp.zeros_like(o_ref)


def moe_dispatch(x, g, token_of_slot, n_tiles, tm):
    T, D = x.shape
    P = token_of_slot.shape[0]
    return pl.pallas_call(
        functools.partial(_dispatch_kernel, tm=tm),
        grid_spec=pltpu.PrefetchScalarGridSpec(
            num_scalar_prefetch=2,
            grid=(P // tm,),
            in_specs=[pl.BlockSpec(memory_space=pl.ANY),
                      pl.BlockSpec((1, D), lambda i, tok, nt: (0, 0))],
            out_specs=pl.BlockSpec((tm, D), lambda i, tok, nt: (i, 0)),
            scratch_shapes=[pltpu.VMEM((GATHER_SLOTS, tm, D), F32),
                            pltpu.SemaphoreType.DMA((GATHER_SLOTS,))]),
        out_shape=jax.ShapeDtypeStruct((P, D), BF16),
        compiler_params=_cparams(1),
        name="moe_dispatch",
    )(token_of_slot, n_tiles, x, g.reshape(1, D))


def _weights_changed(te_ref, i):
    return (i == 0) | (te_ref[i] != te_ref[jnp.maximum(i - 1, 0)])


def _stage_copies(w_hbms, e, j, tn, stage_refs, sem_ref):
    col = pl.multiple_of(j * tn, tn)
    return [pltpu.make_async_copy(w.at[e, :, pl.ds(col, tn)], s, sem_ref.at[n])
            for n, (w, s) in enumerate(zip(w_hbms, stage_refs))]


def _restage_weights(te_ref, nx_ref, w_hbms, stage_refs, wb_refs, sem_ref, *, tn, nj):
    j = pl.program_id(0)
    i = pl.program_id(1)

    @pl.when((j == 0) & (i == 0))
    def _():
        for c in _stage_copies(w_hbms, te_ref[0], 0, tn, stage_refs, sem_ref):
            c.start()

    @pl.when(_weights_changed(te_ref, i))
    def _():
        for c in _stage_copies(w_hbms, te_ref[i], j, tn, stage_refs, sem_ref):
            c.wait()
        for s, wb in zip(stage_refs, wb_refs):
            wb[...] = s[...].astype(BF16)
        nxt = nx_ref[i]
        last = nxt < 0
        e_next = jnp.where(last, te_ref[0], nxt)
        j_next = jnp.where(last, j + 1, j)

        @pl.when(j_next < nj)
        def _():
            for c in _stage_copies(w_hbms, e_next, j_next, tn, stage_refs, sem_ref):
                c.start()


def _for_used_rows(n_used, o_ref, compute):
    tm = o_ref.shape[0]
    classes = sorted({min(c, tm) for c in MOE_ROW_CLASSES} | {tm})
    lo = 0
    for r in classes:
        @pl.when((n_used > lo) & (n_used <= r))
        def _(r=r):
            o_ref[:r, :] = compute(r).astype(o_ref.dtype)
            if r < tm:
                o_ref[r:, :] = jnp.zeros((tm - r, o_ref.shape[1]), o_ref.dtype)
        lo = r

    @pl.when(n_used == 0)
    def _():
        o_ref[...] = jnp.zeros_like(o_ref)


def _moe_up_kernel(te_ref, nt_ref, nx_ref, nu_ref, a_ref, wg_hbm, wu_hbm, o_ref,
                   sg_ref, su_ref, wgb_ref, wub_ref, sem_ref, *, tn, nj):
    _restage_weights(te_ref, nx_ref, (wg_hbm, wu_hbm), (sg_ref, su_ref),
                     (wgb_ref, wub_ref), sem_ref, tn=tn, nj=nj)

    def compute(r):
        a = a_ref[:r, :]
        g = jnp.dot(a, wgb_ref[...], preferred_element_type=F32)
        u = jnp.dot(a, wub_ref[...], preferred_element_type=F32)
        return g * _sigmoid(g) * u

    _for_used_rows(nu_ref[pl.program_id(1)], o_ref, compute)


def _moe_down_kernel(te_ref, nt_ref, nx_ref, nu_ref, a_ref, w_hbm, o_ref,
                     s_ref, wb_ref, sem_ref, *, tn, nj):
    _restage_weights(te_ref, nx_ref, (w_hbm,), (s_ref,), (wb_ref,), sem_ref, tn=tn, nj=nj)
    _for_used_rows(nu_ref[pl.program_id(1)], o_ref,
                   lambda r: jnp.dot(a_ref[:r, :], wb_ref[...], preferred_element_type=F32))


def _moe_grouped(kernel_fn, name, a, weights, plan, tm, tn, out_dtype):
    te, n_tiles, nxt, n_used = plan
    P, K = a.shape
    N = weights[0].shape[2]
    tn = _tile(tn, N)
    nj = N // tn
    used = lambda i, nt: jnp.minimum(i, nt[0] - 1)
    n_w = len(weights)
    return pl.pallas_call(
        functools.partial(kernel_fn, tn=tn, nj=nj),
        grid_spec=pltpu.PrefetchScalarGridSpec(
            num_scalar_prefetch=4,
            grid=(nj, P // tm),
            in_specs=[pl.BlockSpec((tm, K), lambda j, i, te, nt, nx, nu: (used(i, nt), 0))]
                     + [pl.BlockSpec(memory_space=pl.ANY)] * n_w,
            out_specs=pl.BlockSpec((tm, tn), lambda j, i, te, nt, nx, nu: (i, j)),
            scratch_shapes=[pltpu.VMEM((K, tn), F32)] * n_w + [pltpu.VMEM((K, tn), BF16)] * n_w
                           + [pltpu.SemaphoreType.DMA((n_w,))]),
        out_shape=jax.ShapeDtypeStruct((P, N), out_dtype),
        compiler_params=_cparams(2),
        name=name,
    )(te, n_tiles, nxt, n_used, a, *weights)


def moe_up(xs, wg, wu, plan, tm, tn=512):
    return _moe_grouped(_moe_up_kernel, "moe_up", xs, (wg, wu), plan, tm, tn, BF16)


def moe_down(hs, wd, plan, tm, tn=1024):
    return _moe_grouped(_moe_down_kernel, "moe_down", hs, (wd,), plan, tm, tn, F32)


def _combine_kernel(slot_ref, x_ref, wt_ref, gp_ref, ys_hbm, xo_ref, buf_ref, sem_ref, *, tm):
    i = pl.program_id(0)
    n_blocks = tm // COMBINE_ROWS

    def start_rows(tile, s, blk):
        for q in range(COMBINE_ROWS):
            r = blk * COMBINE_ROWS + q
            for k in range(TOP_K):
                _row_copy(ys_hbm, slot_ref[(tile * tm + r) * TOP_K + k], buf_ref.at[s, k], r,
                          sem_ref.at[s]).start()

    def wait_rows(s, blk):
        for q in range(COMBINE_ROWS):
            for k in range(TOP_K):
                _row_copy(ys_hbm, 0, buf_ref.at[s, k], blk * COMBINE_ROWS + q,
                          sem_ref.at[s]).wait()

    def combine_rows(s, blk):
        rows = pl.ds(pl.multiple_of(blk * COMBINE_ROWS, COMBINE_ROWS), COMBINE_ROWS)
        y = (wt_ref[rows, 0:1] * buf_ref[s, 0, rows, :]
             + wt_ref[rows, 1:2] * buf_ref[s, 1, rows, :])
        xo_ref[rows, :] = x_ref[rows, :] + _rms(y, gp_ref[...])

    def loop(body):
        lax.fori_loop(0, n_blocks, lambda blk, c: (body(blk), c)[1], 0, unroll=COMBINE_UNROLL)

    n = pl.num_programs(0)
    ahead = GATHER_SLOTS - 1

    @pl.when(i == 0)
    def _():
        for t in range(ahead):
            @pl.when(t < n)
            def _(t=t):
                loop(lambda blk: start_rows(t, t, blk))

    s = i % GATHER_SLOTS
    loop(lambda blk: wait_rows(s, blk))

    @pl.when(i + ahead < n)
    def _():
        def body(blk):
            start_rows(i + ahead, (i + ahead) % GATHER_SLOTS, blk)
            combine_rows(s, blk)
        loop(body)

    @pl.when(i + ahead >= n)
    def _():
        loop(lambda blk: combine_rows(s, blk))


def moe_combine_residual(x, ys, slot, wts, g_post, tm=256):
    M, D = x.shape
    tm = _tile(tm, M)
    row = pl.BlockSpec((tm, D), lambda i, sl: (i, 0))
    return pl.pallas_call(
        functools.partial(_combine_kernel, tm=tm),
        grid_spec=pltpu.PrefetchScalarGridSpec(
            num_scalar_prefetch=1,
            grid=(M // tm,),
            in_specs=[row, pl.BlockSpec((tm, LANES), lambda i, sl: (i, 0)),
                      pl.BlockSpec((1, D), lambda i, sl: (0, 0)),
                      pl.BlockSpec(memory_space=pl.ANY)],
            out_specs=row,
            scratch_shapes=[pltpu.VMEM((GATHER_SLOTS, TOP_K, tm, D), F32),
                            pltpu.SemaphoreType.DMA((GATHER_SLOTS,))]),
        out_shape=jax.ShapeDtypeStruct((M, D), F32),
        compiler_params=_cparams(1), name="moe_combine_residual",
    )(slot, x, wts, g_post.reshape(1, D), ys)


def _moe_plan(idx, n_tokens, tm):
    e_flat = idx.reshape(-1)
    onehot = (e_flat[:, None] == jnp.arange(N_EXPERTS)[None, :]).astype(jnp.int32)
    counts = jnp.sum(onehot, axis=0)
    rank = jnp.sum((jnp.cumsum(onehot, axis=0) - onehot) * onehot, axis=1)
    tiles_per = (counts + tm - 1) // tm
    tile_end = jnp.cumsum(tiles_per)
    tile_start = tile_end - tiles_per
    slot = (tile_start[e_flat] * tm + rank).astype(jnp.int32)
    n_tiles_max = (n_tokens * TOP_K) // tm + N_EXPERTS
    n_tiles = tile_end[-1]
    tile_ids = jnp.minimum(jnp.arange(n_tiles_max), n_tiles - 1)
    expert_of = lambda t: jnp.minimum(
        jnp.sum((tile_end[None, :] <= t[:, None]).astype(jnp.int32), axis=1), N_EXPERTS - 1)
    te = expert_of(tile_ids)
    group_end = tile_end[te]
    nxt = jnp.where(group_end < n_tiles, expert_of(jnp.minimum(group_end, n_tiles - 1)), -1)
    all_ids = jnp.arange(n_tiles_max)
    n_used = jnp.where(all_ids < n_tiles,
                       jnp.clip(counts[te] - (all_ids - tile_start[te]) * tm, 0, tm), 0)
    token_of_pair = jnp.arange(n_tokens * TOP_K, dtype=jnp.int32) // TOP_K
    token_of_slot = jnp.zeros((n_tiles_max * tm,), jnp.int32).at[slot].set(token_of_pair)
    plan = (te.astype(jnp.int32), n_tiles.reshape(1).astype(jnp.int32), nxt.astype(jnp.int32),
            n_used.astype(jnp.int32))
    return slot, token_of_slot, plan


def kernel(x, mem, mem_norm, mem_kv, l0_norm_mix_pre, l0_mix_in, l0_sgu_ln_g, l0_sgu_ln_b, l0_sgu_w, l0_sgu_b, l0_mix_out, l0_norm_mix_post, l0_norm_x_pre, l0_xq, l0_xo, l0_norm_x_post, l0_norm_ffn_pre, l0_ffn_gate, l0_ffn_up, l0_ffn_down, l0_norm_ffn_post, l1_norm_mix_pre, l1_mix_in, l1_gate_b, l1_conv, l1_head_norm, l1_mix_out, l1_norm_mix_post, l1_norm_x_pre, l1_xq, l1_xo, l1_norm_x_post, l1_norm_ffn_pre, l1_router, l1_moe_gate, l1_moe_up, l1_moe_down, l1_norm_ffn_post):
    B, S, D = x.shape
    T = B * S
    n_mem = mem.shape[1]
    xf = x.reshape(T, D)

    memn = rmsnorm_rows(mem.reshape(B * n_mem, D), mem_norm)
    kv = matmul_tiledk(memn, mem_kv, BF16, tn=512, tk=D).reshape(B, n_mem, 2 * D)

    hn = rmsnorm_rows(xf, l0_norm_mix_pre)
    z = matmul_fullk(hn, l0_mix_in, BF16, act="gelu")
    y = sgu_mix(z, l0_sgu_ln_g, l0_sgu_ln_b, l0_sgu_w, l0_sgu_b)
    y = matmul_fullk(y, l0_mix_out, BF16)
    xf, hn = residual_norm(xf, y, l0_norm_mix_post, l0_norm_x_pre)
    o = cross_attention(matmul_fullk(hn, l0_xq, BF16), kv, B)
    y = matmul_fullk(o, l0_xo, BF16)
    xf, hn = residual_norm(xf, y, l0_norm_x_post, l0_norm_ffn_pre)
    hmid = swiglu_up(hn, l0_ffn_gate, l0_ffn_up)
    y = matmul_tiledk(hmid, l0_ffn_down, BF16)
    xf, hn = residual_norm(xf, y, l0_norm_ffn_post, l1_norm_mix_pre)

    n_gates = 2 * MLSTM_HEADS
    n_main = l1_mix_in.shape[1] - n_gates
    w_in_t = l1_mix_in.T
    proj = matmul_fullk(hn, w_in_t, BF16, w_rows=(0, n_main))
    gates_t = mlstm_gates(hn, w_in_t[n_main:], l1_gate_b)
    y = mlstm_core(proj, gates_t, l1_conv, l1_head_norm, B)
    y = matmul_fullk(y, l1_mix_out, BF16)
    xf, hn = residual_norm(xf, y, l1_norm_mix_post, l1_norm_x_pre)
    o = cross_attention(matmul_fullk(hn, l1_xq, BF16), kv, B)
    y = matmul_fullk(o, l1_xo, BF16)
    xf, idx, wts = residual_router(xf, y, l1_norm_x_post, l1_norm_ffn_pre, l1_router)
    tm = min(MOE_TM, T)
    slot, token_of_slot, plan = _moe_plan(idx[:, :TOP_K], T, tm)
    xs = moe_dispatch(xf, l1_norm_ffn_pre, token_of_slot, plan[1], tm)
    hs = moe_up(xs, l1_moe_gate, l1_moe_up, plan, tm)
    ys = moe_down(hs, l1_moe_down, plan, tm)
    xf = moe_combine_residual(xf, ys, slot, wts, l1_norm_ffn_post)
    return xf.reshape(B, S, D)
```

```python
import functools

import jax
import jax.numpy as jnp
from jax import lax
from jax.experimental import pallas as pl
from jax.experimental.pallas import tpu as pltpu

F32 = jnp.float32
BF16 = jnp.bfloat16

EPS = 1e-6
CHUNK = 128
SGU_GROUPS = 8
MLSTM_HEADS = 8
CONV_WIDTH = 4
GATE_CAP = 15.0
X_HEADS = 4
N_EXPERTS = 8
TOP_K = 2
LANES = 128
MOE_TM = 512
MOE_ROW_CLASSES = (128, 256)
GATHER_ROWS = 16
GATHER_UNROLL = 2
GATHER_SLOTS = 3
COMBINE_ROWS = 8
COMBINE_UNROLL = 4
MLSTM_HEADS_PER_STEP = 2
CONV_TAIL = 8
VMEM_LIMIT = 56 * 1024 * 1024


def _cparams(n_axes, vmem=VMEM_LIMIT):
    return pltpu.CompilerParams(
        dimension_semantics=("arbitrary",) * n_axes, vmem_limit_bytes=vmem)


def _tile(pref, dim):
    t = min(pref, dim)
    assert dim % t == 0, (pref, dim)
    return t


def _rms(x, g):
    return x * lax.rsqrt(jnp.mean(x * x, axis=-1, keepdims=True) + EPS) * g


def _sigmoid(x):
    return 1.0 / (1.0 + jnp.exp(-x))


def _rms_kernel(x_ref, g_ref, o_ref):
    o_ref[...] = _rms(x_ref[...], g_ref[...]).astype(o_ref.dtype)


def rmsnorm_rows(x, g, tm=256):
    M, D = x.shape
    tm = _tile(tm, M)
    return pl.pallas_call(
        _rms_kernel,
        grid=(M // tm,),
        in_specs=[pl.BlockSpec((tm, D), lambda i: (i, 0)),
                  pl.BlockSpec((1, D), lambda i: (0, 0))],
        out_specs=pl.BlockSpec((tm, D), lambda i: (i, 0)),
        out_shape=jax.ShapeDtypeStruct((M, D), BF16),
        compiler_params=_cparams(1),
        name="rmsnorm_rows",
    )(x, g.reshape(1, D))


def _gelu(x):
    return 0.5 * x * (1.0 + lax.erf(x * 0.7071067811865476))


def _stream_weights(w_hbms, stage_refs, wb_refs, sem_ref, *, tn, nj, transposed, first=0):
    j = pl.program_id(0)
    i = pl.program_id(1)

    def copies(jj):
        off = pl.multiple_of(first + jj * tn, tn)
        return [pltpu.make_async_copy(
                    w.at[pl.ds(off, tn), :] if transposed else w.at[:, pl.ds(off, tn)],
                    s, sem_ref.at[n])
                for n, (w, s) in enumerate(zip(w_hbms, stage_refs))]

    @pl.when((j == 0) & (i == 0))
    def _():
        for c in copies(0):
            c.start()

    @pl.when(i == 0)
    def _():
        for c in copies(j):
            c.wait()
        for s, wb in zip(stage_refs, wb_refs):
            wb[...] = (s[...].T if transposed else s[...]).astype(BF16)

        @pl.when(j + 1 < nj)
        def _():
            for c in copies(j + 1):
                c.start()


def _mm_kernel(a_ref, w_hbm, o_ref, stage_ref, wb_ref, sem_ref, *, act, tn, nj, transposed,
               first):
    _stream_weights((w_hbm,), (stage_ref,), (wb_ref,), sem_ref,
                    tn=tn, nj=nj, transposed=transposed, first=first)
    acc = jnp.dot(a_ref[...], wb_ref[...], preferred_element_type=F32)
    if act == "gelu":
        acc = _gelu(acc)
    o_ref[...] = acc.astype(o_ref.dtype)


def matmul_fullk(a, w, out_dtype, act=None, w_rows=None, tm=1024, tn=1024):
    M, K = a.shape
    transposed = w_rows is not None
    first, N = w_rows if transposed else (0, w.shape[1])
    tm = _tile(tm, M)
    tn = _tile(tn, N)
    assert first % tn == 0
    nj = N // tn
    stage_block = (tn, K) if transposed else (K, tn)
    return pl.pallas_call(
        functools.partial(_mm_kernel, act=act, tn=tn, nj=nj, transposed=transposed,
                          first=first),
        grid=(nj, M // tm),
        in_specs=[pl.BlockSpec((tm, K), lambda j, i: (i, 0)),
                  pl.BlockSpec(memory_space=pl.ANY)],
        out_specs=pl.BlockSpec((tm, tn), lambda j, i: (i, j)),
        out_shape=jax.ShapeDtypeStruct((M, N), out_dtype),
        scratch_shapes=[pltpu.VMEM(stage_block, F32), pltpu.VMEM((K, tn), BF16),
                        pltpu.SemaphoreType.DMA((1,))],
        compiler_params=_cparams(2),
        name="matmul_fullk" + ("_" + act if act else "") + ("_t" if transposed else ""),
    )(a, w)


def _swiglu_kernel(a_ref, wg_hbm, wu_hbm, o_ref, sg_ref, su_ref, wb_ref, sem_ref, *, tn, nj):
    _stream_weights((wg_hbm, wu_hbm), (sg_ref, su_ref),
                    (wb_ref.at[:, :tn], wb_ref.at[:, tn:]), sem_ref,
                    tn=tn, nj=nj, transposed=False)
    gu = jnp.dot(a_ref[...], wb_ref[...], preferred_element_type=F32)
    g = gu[:, :tn]
    o_ref[...] = (g * _sigmoid(g) * gu[:, tn:]).astype(o_ref.dtype)


def swiglu_up(a, wg, wu, tm=1024, tn=512):
    M, K = a.shape
    F = wg.shape[1]
    tm = _tile(tm, M)
    tn = _tile(tn, F)
    nj = F // tn
    hbm = pl.BlockSpec(memory_space=pl.ANY)
    return pl.pallas_call(
        functools.partial(_swiglu_kernel, tn=tn, nj=nj),
        grid=(nj, M // tm),
        in_specs=[pl.BlockSpec((tm, K), lambda j, i: (i, 0)), hbm, hbm],
        out_specs=pl.BlockSpec((tm, tn), lambda j, i: (i, j)),
        out_shape=jax.ShapeDtypeStruct((M, F), BF16),
        scratch_shapes=[pltpu.VMEM((K, tn), F32), pltpu.VMEM((K, tn), F32),
                        pltpu.VMEM((K, 2 * tn), BF16), pltpu.SemaphoreType.DMA((2,))],
        compiler_params=_cparams(2),
        name="swiglu_up",
    )(a, wg, wu)


def _mmk_kernel(a_ref, w_ref, o_ref, acc_ref, *, nk):
    k = pl.program_id(2)

    @pl.when(k == 0)
    def _():
        acc_ref[...] = jnp.zeros_like(acc_ref)

    acc_ref[...] += jnp.dot(a_ref[...], w_ref[...].astype(BF16),
                            preferred_element_type=F32)

    @pl.when(k == nk - 1)
    def _():
        o_ref[...] = acc_ref[...].astype(o_ref.dtype)


def matmul_tiledk(a, w, out_dtype, tm=2048, tn=1024, tk=1024):
    M, K = a.shape
    N = w.shape[1]
    tm, tn, tk = _tile(tm, M), _tile(tn, N), _tile(tk, K)
    nk = K // tk
    return pl.pallas_call(
        functools.partial(_mmk_kernel, nk=nk),
        grid=(N // tn, M // tm, nk),
        in_specs=[pl.BlockSpec((tm, tk), lambda j, i, k: (i, k)),
                  pl.BlockSpec((tk, tn), lambda j, i, k: (k, j))],
        out_specs=pl.BlockSpec((tm, tn), lambda j, i, k: (i, j)),
        out_shape=jax.ShapeDtypeStruct((M, N), out_dtype),
        scratch_shapes=[pltpu.VMEM((tm, tn), F32)],
        compiler_params=_cparams(3),
        name="matmul_tiledk",
    )(a, w)


def _res_kernel(x_ref, y_ref, gp_ref, gn_ref, xo_ref, hn_ref):
    xn = x_ref[...] + _rms(y_ref[...].astype(F32), gp_ref[...])
    xo_ref[...] = xn
    hn_ref[...] = _rms(xn, gn_ref[...]).astype(hn_ref.dtype)


def residual_norm(x, y, g_post, g_next, tm=256):
    M, D = x.shape
    tm = _tile(tm, M)
    row = pl.BlockSpec((tm, D), lambda i: (i, 0))
    vec = pl.BlockSpec((1, D), lambda i: (0, 0))
    return pl.pallas_call(
        _res_kernel, grid=(M // tm,),
        in_specs=[row, row, vec, vec], out_specs=[row, row],
        out_shape=[jax.ShapeDtypeStruct((M, D), F32), jax.ShapeDtypeStruct((M, D), BF16)],
        compiler_params=_cparams(1), name="residual_norm",
    )(x, y, g_post.reshape(1, D), g_next.reshape(1, D))


def _sgu_kernel(u_ref, v_ref, lg_ref, lb_ref, w_ref, bt_ref, o_ref, wm_ref, *, groups):
    @pl.when(pl.program_id(0) == 0)
    def _():
        t = lax.broadcasted_iota(jnp.int32, (CHUNK, CHUNK), 0)
        s = lax.broadcasted_iota(jnp.int32, (CHUNK, CHUNK), 1)
        for g in range(groups):
            wm_ref[g] = jnp.where(t >= s, w_ref[g], 0.0).astype(BF16)

    v = v_ref[...].astype(F32)
    vc = v - jnp.mean(v, axis=-1, keepdims=True)
    vn = vc * lax.rsqrt(jnp.mean(vc * vc, axis=-1, keepdims=True) + EPS)
    vn = (vn * lg_ref[...] + lb_ref[...]).astype(BF16)
    tm, width = vn.shape
    gd = width // groups
    for c in range(tm // CHUNK):
        rows = slice(c * CHUNK, (c + 1) * CHUNK)
        for g in range(groups):
            cols = slice(g * gd, (g + 1) * gd)
            mixed = jnp.dot(wm_ref[g], vn[rows, cols], preferred_element_type=F32)
            mixed = mixed + bt_ref[:, g:g + 1]
            o_ref[rows, cols] = (u_ref[rows, cols].astype(F32) * mixed).astype(o_ref.dtype)


def sgu_mix(z, ln_g, ln_b, sgu_w, sgu_b, tm=256):
    M, W2 = z.shape
    W = W2 // 2
    G = sgu_w.shape[0]
    tm = _tile(tm, M)
    return pl.pallas_call(
        functools.partial(_sgu_kernel, groups=G),
        grid=(M // tm,),
        in_specs=[pl.BlockSpec((tm, W), lambda i: (i, 0)),
                  pl.BlockSpec((tm, W), lambda i: (i, 1)),
                  pl.BlockSpec((1, W), lambda i: (0, 0)),
                  pl.BlockSpec((1, W), lambda i: (0, 0)),
                  pl.BlockSpec((G, CHUNK, CHUNK), lambda i: (0, 0, 0)),
                  pl.BlockSpec((CHUNK, G), lambda i: (0, 0))],
        out_specs=pl.BlockSpec((tm, W), lambda i: (i, 0)),
        out_shape=jax.ShapeDtypeStruct((M, W), BF16),
        scratch_shapes=[pltpu.VMEM((G, CHUNK, CHUNK), BF16)],
        compiler_params=_cparams(1),
        name="sgu_mix",
    )(z, z, ln_g.reshape(1, W), ln_b.reshape(1, W), sgu_w, sgu_b.T)


def _xattn_kernel(q_ref, k_ref, v_ref, o_ref, *, heads, scale):
    hd = q_ref.shape[-1] // heads
    for h in range(heads):
        cols = slice(h * hd, (h + 1) * hd)
        s = lax.dot_general(q_ref[:, cols], k_ref[0, :, cols], (((1,), (1,)), ((), ())),
                            preferred_element_type=F32) * scale
        p = jnp.exp(s - jnp.max(s, axis=-1, keepdims=True))
        p = p / jnp.sum(p, axis=-1, keepdims=True)
        o = jnp.dot(p.astype(BF16), v_ref[0, :, cols], preferred_element_type=F32)
        o_ref[:, cols] = o.astype(o_ref.dtype)


def cross_attention(q, kv, batch, tm=512):
    M, D = q.shape
    S = M // batch
    n_mem = kv.shape[1]
    tm = _tile(tm, S)
    spt = S // tm
    return pl.pallas_call(
        functools.partial(_xattn_kernel, heads=X_HEADS, scale=(D // X_HEADS) ** -0.5),
        grid=(batch, spt),
        in_specs=[pl.BlockSpec((tm, D), lambda b, m: (b * spt + m, 0)),
                  pl.BlockSpec((1, n_mem, D), lambda b, m: (b, 0, 0)),
                  pl.BlockSpec((1, n_mem, D), lambda b, m: (b, 0, 1))],
        out_specs=pl.BlockSpec((tm, D), lambda b, m: (b * spt + m, 0)),
        out_shape=jax.ShapeDtypeStruct((M, D), BF16),
        compiler_params=_cparams(2),
        name="cross_attention",
    )(q, kv, kv)


def _gates_kernel(hn_ref, w_ref, b_ref, o_ref):
    acc = lax.dot_general(w_ref[...].astype(BF16), hn_ref[...], (((1,), (1,)), ((), ())),
                          preferred_element_type=F32)
    o_ref[...] = GATE_CAP * jnp.tanh((acc + b_ref[...]) / GATE_CAP)


def mlstm_gates(hn, w_gates_t, gate_b, tm=512):
    M, D = hn.shape
    n_gates = w_gates_t.shape[0]
    tm = _tile(tm, M)
    return pl.pallas_call(
        _gates_kernel,
        grid=(M // tm,),
        in_specs=[pl.BlockSpec((tm, D), lambda i: (i, 0)),
                  pl.BlockSpec((n_gates, D), lambda i: (0, 0)),
                  pl.BlockSpec((n_gates, 1), lambda i: (0, 0))],
        out_specs=pl.BlockSpec((n_gates, tm), lambda i: (0, i)),
        out_shape=jax.ShapeDtypeStruct((n_gates, M), F32),
        compiler_params=_cparams(1),
        name="mlstm_gates",
    )(hn, w_gates_t, gate_b.reshape(n_gates, 1))


def _conv_silu(x, tail_ref, w):
    prev = tail_ref[...]
    rid = lax.broadcasted_iota(jnp.int32, prev.shape, 0)
    acc = x * w[CONV_WIDTH - 1:CONV_WIDTH, :]
    for r in range(1, CONV_WIDTH):
        rolled = pltpu.roll(x, r, 0)
        head = jnp.where(rid < r, pltpu.roll(prev, r, 0), rolled[:CONV_TAIL])
        shifted = jnp.concatenate([head, rolled[CONV_TAIL:]], axis=0)
        acc = acc + shifted * w[CONV_WIDTH - 1 - r:CONV_WIDTH - r, :]
    tail_ref[...] = x[x.shape[0] - CONV_TAIL:, :]
    return acc * _sigmoid(acc)


def _mlstm_kernel(q_ref, k_ref, v_ref, o_ref, ig_ref, fg_ref, cq_ref, ck_ref, hg_ref,
                  y_ref, s_ref, n_ref, m_ref, qt_ref, kt_ref, *, heads):
    @pl.when(pl.program_id(2) == 0)
    def _():
        s_ref[...] = jnp.zeros_like(s_ref)
        n_ref[...] = jnp.zeros_like(n_ref)
        m_ref[...] = jnp.zeros_like(m_ref)
        qt_ref[...] = jnp.zeros_like(qt_ref)
        kt_ref[...] = jnp.zeros_like(kt_ref)

    dk = q_ref.shape[-1] // heads
    dv = v_ref.shape[-1] // heads
    for h in range(heads):
        qc = slice(h * dk, (h + 1) * dk)
        vc = slice(h * dv, (h + 1) * dv)
        q = _conv_silu(q_ref[:, qc].astype(F32), qt_ref.at[h], cq_ref[:, qc]) * dk ** -0.5
        k = _conv_silu(k_ref[:, qc].astype(F32), kt_ref.at[h], ck_ref[:, qc])
        _mlstm_head(q, k, v_ref[:, vc], o_ref[:, vc], ig_ref[h], fg_ref[h], hg_ref[:, vc],
                    y_ref.at[:, vc], s_ref.at[h], n_ref.at[h], m_ref.at[h])


def _mlstm_head(q, k, vb, ob, i_row, f_row, hg, y_ref, s_ref, n_ref, m_ref):
    L = CHUNK
    qb = q.astype(BF16)
    kb = k.astype(BF16)
    v = vb.astype(F32)

    t_id = lax.broadcasted_iota(jnp.int32, (L, L), 0)
    s_id = lax.broadcasted_iota(jnp.int32, (L, L), 1)
    eye = t_id == s_id
    causal = t_id >= s_id

    logf_row = -(jnp.maximum(-f_row, 0.0) + jnp.log1p(jnp.exp(-jnp.abs(f_row))))
    logf_b = jnp.broadcast_to(logf_row, (L, L))
    i_b = jnp.broadcast_to(i_row, (L, L))
    logf_col = jnp.sum(jnp.where(eye, logf_b, 0.0), axis=1, keepdims=True)
    i_col = jnp.sum(jnp.where(eye, i_b, 0.0), axis=1, keepdims=True)
    b_col = jnp.sum(jnp.where(causal, logf_b, 0.0), axis=1, keepdims=True)
    b_row = jnp.sum(jnp.where(t_id <= s_id, jnp.broadcast_to(logf_col, (L, L)), 0.0),
                    axis=0, keepdims=True)
    g_tot = jnp.sum(logf_row, axis=1, keepdims=True)
    m0 = m_ref[:, 0:1]

    d_log = jnp.where(causal, b_col - b_row + i_row, -jnp.inf)
    m_inter = b_col + m0
    m_t = jnp.maximum(m_inter, jnp.max(d_log, axis=1, keepdims=True))
    qk = lax.dot_general(qb, kb, (((1,), (1,)), ((), ())), preferred_element_type=F32)
    scores = qk * jnp.exp(d_log - m_t)
    inter = jnp.exp(m_inter - m_t)
    num = (jnp.dot(scores.astype(BF16), vb, preferred_element_type=F32)
           + inter * jnp.dot(qb, s_ref[...].astype(BF16), preferred_element_type=F32))
    den = (jnp.sum(scores, axis=1, keepdims=True)
           + inter * jnp.sum(q * n_ref[...], axis=1, keepdims=True))
    h = num / jnp.maximum(jnp.abs(den), jnp.exp(-m_t))
    y_ref[...] = (_sigmoid(ob.astype(F32)) * _rms(h, hg)).astype(y_ref.dtype)

    w_log = g_tot - b_col + i_col
    a = jnp.max(w_log, axis=0, keepdims=True)
    w = jnp.exp(w_log - a)
    s_loc = lax.dot_general(kb, (w * v).astype(BF16), (((0,), (0,)), ((), ())),
                            preferred_element_type=F32)
    n_loc = jnp.sum(w * k, axis=0, keepdims=True)
    m_new = jnp.maximum(g_tot + m0, a)
    sc_prev = jnp.exp(g_tot + m0 - m_new)
    sc_loc = jnp.exp(a - m_new)
    s_ref[...] = sc_prev * s_ref[...] + sc_loc * s_loc
    n_ref[...] = sc_prev * n_ref[...] + sc_loc * n_loc
    m_ref[...] = jnp.broadcast_to(m_new, m_ref.shape)


def mlstm_core(proj, gates_t, conv_w, head_g, batch):
    M = proj.shape[0]
    H = MLSTM_HEADS
    G = MLSTM_HEADS_PER_STEP
    assert H % G == 0
    ng = H // G
    qkw = conv_w.shape[1]
    dk = qkw // (2 * H)
    dv = head_g.shape[0] // H
    nc = M // batch // CHUNK
    assert qkw % (G * dv) == 0
    v0 = qkw // (G * dv)
    gates4 = gates_t.reshape(2 * ng, G, 1, M)
    row = lambda b, c: b * nc + c
    return pl.pallas_call(
        functools.partial(_mlstm_kernel, heads=G),
        grid=(batch, ng, nc),
        in_specs=[pl.BlockSpec((CHUNK, G * dk), lambda b, g, c: (row(b, c), g)),
                  pl.BlockSpec((CHUNK, G * dk), lambda b, g, c: (row(b, c), ng + g)),
                  pl.BlockSpec((CHUNK, G * dv), lambda b, g, c: (row(b, c), v0 + g)),
                  pl.BlockSpec((CHUNK, G * dv), lambda b, g, c: (row(b, c), v0 + ng + g)),
                  pl.BlockSpec((None, G, 1, CHUNK), lambda b, g, c: (g, 0, 0, row(b, c))),
                  pl.BlockSpec((None, G, 1, CHUNK), lambda b, g, c: (ng + g, 0, 0, row(b, c))),
                  pl.BlockSpec((CONV_WIDTH, G * dk), lambda b, g, c: (0, g)),
                  pl.BlockSpec((CONV_WIDTH, G * dk), lambda b, g, c: (0, ng + g)),
                  pl.BlockSpec((1, G * dv), lambda b, g, c: (0, g))],
        out_specs=pl.BlockSpec((CHUNK, G * dv), lambda b, g, c: (row(b, c), g)),
        out_shape=jax.ShapeDtypeStruct((M, H * dv), BF16),
        scratch_shapes=[pltpu.VMEM((G, dk, dv), F32), pltpu.VMEM((G, 1, dk), F32),
                        pltpu.VMEM((G, 1, LANES), F32),
                        pltpu.VMEM((G, CONV_TAIL, dk), F32), pltpu.VMEM((G, CONV_TAIL, dk), F32)],
        compiler_params=_cparams(3),
        name="mlstm_core",
    )(proj, proj, proj, proj, gates4, gates4, conv_w, conv_w, head_g.reshape(1, H * dv))


def _res_router_kernel(x_ref, y_ref, gp_ref, gn_ref, w2_ref, xo_ref, idx_ref, wt_ref):
    xn = x_ref[...] + _rms(y_ref[...].astype(F32), gp_ref[...])
    xo_ref[...] = xn
    hn = _rms(xn, gn_ref[...])
    hi = hn.astype(BF16)
    lo = (hn - hi.astype(F32)).astype(BF16)
    t = jnp.dot(hi, w2_ref[...], preferred_element_type=F32)
    logits = (t[:, :LANES] + t[:, LANES:]
              + jnp.dot(lo, w2_ref[:, :LANES], preferred_element_type=F32))
    lane = lax.broadcasted_iota(jnp.int32, logits.shape, 1)
    neg = -jnp.inf
    l1 = jnp.where(lane < N_EXPERTS, logits, neg)
    m1 = jnp.max(l1, axis=1, keepdims=True)
    i1 = jnp.min(jnp.where(l1 == m1, lane, LANES), axis=1, keepdims=True)
    l2 = jnp.where(lane == i1, neg, l1)
    m2 = jnp.max(l2, axis=1, keepdims=True)
    i2 = jnp.min(jnp.where(l2 == m2, lane, LANES), axis=1, keepdims=True)
    r = jnp.exp(m2 - m1)
    w1 = 1.0 / (1.0 + r)
    w2 = r / (1.0 + r)
    idx_ref[...] = jnp.where(lane == 0, i1, jnp.where(lane == 1, i2, 0))
    wt_ref[...] = jnp.where(lane == 0, w1, jnp.where(lane == 1, w2, 0.0))


def residual_router(x, y, g_post, g_next, w_router, tm=256):
    M, D = x.shape
    tm = _tile(tm, M)
    w_pad = jnp.pad(w_router, ((0, 0), (0, LANES - w_router.shape[1])))
    w_hi = w_pad.astype(BF16)
    w_lo = (w_pad - w_hi.astype(F32)).astype(BF16)
    row = pl.BlockSpec((tm, D), lambda i: (i, 0))
    vec = pl.BlockSpec((1, D), lambda i: (0, 0))
    out = pl.BlockSpec((tm, LANES), lambda i: (i, 0))
    return pl.pallas_call(
        _res_router_kernel, grid=(M // tm,),
        in_specs=[row, row, vec, vec, pl.BlockSpec((D, 2 * LANES), lambda i: (0, 0))],
        out_specs=[row, out, out],
        out_shape=[jax.ShapeDtypeStruct((M, D), F32),
                   jax.ShapeDtypeStruct((M, LANES), jnp.int32),
                   jax.ShapeDtypeStruct((M, LANES), F32)],
        compiler_params=_cparams(1), name="residual_router",
    )(x, y, g_post.reshape(1, D), g_next.reshape(1, D),
      jnp.concatenate([w_hi, w_lo], axis=1))


def _row_copy(src_hbm, src_row, dst_ref, dst_row, sem):
    return pltpu.make_async_copy(src_hbm.at[pl.ds(src_row, 1)],
                                 dst_ref.at[pl.ds(dst_row, 1)], sem)


def _dispatch_kernel(tok_ref, nt_ref, x_hbm, g_ref, o_ref, buf_ref, sem_ref, *, tm):
    i = pl.program_id(0)
    nt = nt_ref[0]
    n_blocks = tm // GATHER_ROWS

    def start_rows(tile, slot, blk):
        for q in range(GATHER_ROWS):
            r = blk * GATHER_ROWS + q
            _row_copy(x_hbm, tok_ref[tile * tm + r], buf_ref.at[slot], r,
                      sem_ref.at[slot]).start()

    def wait_rows(slot, blk):
        for q in range(GATHER_ROWS):
            _row_copy(x_hbm, 0, buf_ref.at[slot], blk * GATHER_ROWS + q,
                      sem_ref.at[slot]).wait()

    def norm_rows(slot, blk):
        rows = pl.ds(pl.multiple_of(blk * GATHER_ROWS, GATHER_ROWS), GATHER_ROWS)
        o_ref[rows, :] = _rms(buf_ref[slot, rows, :], g_ref[...]).astype(o_ref.dtype)

    def loop(body):
        lax.fori_loop(0, n_blocks, lambda blk, c: (body(blk), c)[1], 0, unroll=GATHER_UNROLL)

    ahead = GATHER_SLOTS - 1

    @pl.when(i == 0)
    def _():
        for t in range(ahead):
            @pl.when(t < nt)
            def _(t=t):
                loop(lambda blk: start_rows(t, t, blk))

    slot = i % GATHER_SLOTS

    @pl.when(i < nt)
    def _():
        loop(lambda blk: wait_rows(slot, blk))

    @pl.when(i + ahead < nt)
    def _():
        def body(blk):
            start_rows(i + ahead, (i + ahead) % GATHER_SLOTS, blk)
            norm_rows(slot, blk)
        loop(body)

    @pl.when((i < nt) & (i + ahead >= nt))
    def _():
        loop(lambda blk: norm_rows(slot, blk))

    @pl.when(i >= nt)
    def _():
        o_ref[...] = jnp.zeros_like(o_ref)


def moe_dispatch(x, g, token_of_slot, n_tiles, tm):
    T, D = x.shape
    P = token_of_slot.shape[0]
    return pl.pallas_call(
        functools.partial(_dispatch_kernel, tm=tm),
        grid_spec=pltpu.PrefetchScalarGridSpec(
            num_scalar_prefetch=2,
            grid=(P // tm,),
            in_specs=[pl.BlockSpec(memory_space=pl.ANY),
                      pl.BlockSpec((1, D), lambda i, tok, nt: (0, 0))],
            out_specs=pl.BlockSpec((tm, D), lambda i, tok, nt: (i, 0)),
            scratch_shapes=[pltpu.VMEM((GATHER_SLOTS, tm, D), F32),
                            pltpu.SemaphoreType.DMA((GATHER_SLOTS,))]),
        out_shape=jax.ShapeDtypeStruct((P, D), BF16),
        compiler_params=_cparams(1),
        name="moe_dispatch",
    )(token_of_slot, n_tiles, x, g.reshape(1, D))


def _weights_changed(te_ref, i):
    return (i == 0) | (te_ref[i] != te_ref[jnp.maximum(i - 1, 0)])


def _stage_copies(w_hbms, e, j, tn, stage_refs, sem_ref):
    col = pl.multiple_of(j * tn, tn)
    return [pltpu.make_async_copy(w.at[e, :, pl.ds(col, tn)], s, sem_ref.at[n])
            for n, (w, s) in enumerate(zip(w_hbms, stage_refs))]


def _for_used_rows(n_used, o_ref, compute):
    tm = o_ref.shape[0]
    classes = sorted({min(c, tm) for c in MOE_ROW_CLASSES} | {tm})
    lo = 0
    for r in classes:
        @pl.when((n_used > lo) & (n_used <= r))
        def _(r=r):
            o_ref[:r, :] = compute(r).astype(o_ref.dtype)
            if r < tm:
                o_ref[r:, :] = jnp.zeros((tm - r, o_ref.shape[1]), o_ref.dtype)
        lo = r

    @pl.when(n_used == 0)
    def _():
        o_ref[...] = jnp.zeros_like(o_ref)


def _grouped_step(te_ref, nx_ref, nu_ref, w_hbms, stage_refs, wb_refs, sem_ref, o_ref, compute,
                  *, tn, nj):
    j = pl.program_id(0)
    i = pl.program_id(1)
    n_used = nu_ref[i]
    changed = _weights_changed(te_ref, i)

    @pl.when((j == 0) & (i == 0))
    def _():
        for c in _stage_copies(w_hbms, te_ref[0], 0, tn, stage_refs, sem_ref):
            c.start()

    @pl.when(changed)
    def _():
        for c in _stage_copies(w_hbms, te_ref[i], j, tn, stage_refs, sem_ref):
            c.wait()

        def cast_and_compute(r):
            weights = [s[...].astype(BF16) for s in stage_refs]
            for wb, w in zip(wb_refs, weights):
                wb[...] = w
            return compute(r, weights)

        _for_used_rows(n_used, o_ref, cast_and_compute)
        nxt = nx_ref[i]
        last = nxt < 0
        e_next = jnp.where(last, te_ref[0], nxt)
        j_next = jnp.where(last, j + 1, j)

        @pl.when(j_next < nj)
        def _():
            for c in _stage_copies(w_hbms, e_next, j_next, tn, stage_refs, sem_ref):
                c.start()

    @pl.when(jnp.logical_not(changed))
    def _():
        _for_used_rows(n_used, o_ref, lambda r: compute(r, [wb[...] for wb in wb_refs]))


def _moe_up_kernel(te_ref, nt_ref, nx_ref, nu_ref, a_ref, wg_hbm, wu_hbm, o_ref,
                   sg_ref, su_ref, wgb_ref, wub_ref, sem_ref, *, tn, nj):
    def compute(r, weights):
        a = a_ref[:r, :]
        g = jnp.dot(a, weights[0], preferred_element_type=F32)
        u = jnp.dot(a, weights[1], preferred_element_type=F32)
        return g * _sigmoid(g) * u

    _grouped_step(te_ref, nx_ref, nu_ref, (wg_hbm, wu_hbm), (sg_ref, su_ref),
                  (wgb_ref, wub_ref), sem_ref, o_ref, compute, tn=tn, nj=nj)


def _moe_down_kernel(te_ref, nt_ref, nx_ref, nu_ref, a_ref, w_hbm, o_ref,
                     s_ref, wb_ref, sem_ref, *, tn, nj):
    _grouped_step(te_ref, nx_ref, nu_ref, (w_hbm,), (s_ref,), (wb_ref,), sem_ref, o_ref,
                  lambda r, weights: jnp.dot(a_ref[:r, :], weights[0],
                                             preferred_element_type=F32),
                  tn=tn, nj=nj)


def _moe_grouped(kernel_fn, name, a, weights, plan, tm, tn, out_dtype):
    te, n_tiles, nxt, n_used = plan
    P, K = a.shape
    N = weights[0].shape[2]
    tn = _tile(tn, N)
    nj = N // tn
    used = lambda i, nt: jnp.minimum(i, nt[0] - 1)
    n_w = len(weights)
    return pl.pallas_call(
        functools.partial(kernel_fn, tn=tn, nj=nj),
        grid_spec=pltpu.PrefetchScalarGridSpec(
            num_scalar_prefetch=4,
            grid=(nj, P // tm),
            in_specs=[pl.BlockSpec((tm, K), lambda j, i, te, nt, nx, nu: (used(i, nt), 0))]
                     + [pl.BlockSpec(memory_space=pl.ANY)] * n_w,
            out_specs=pl.BlockSpec((tm, tn), lambda j, i, te, nt, nx, nu: (i, j)),
            scratch_shapes=[pltpu.VMEM((K, tn), F32)] * n_w + [pltpu.VMEM((K, tn), BF16)] * n_w
                           + [pltpu.SemaphoreType.DMA((n_w,))]),
        out_shape=jax.ShapeDtypeStruct((P, N), out_dtype),
        compiler_params=_cparams(2),
        name=name,
    )(te, n_tiles, nxt, n_used, a, *weights)


def moe_up(xs, wg, wu, plan, tm, tn=512):
    return _moe_grouped(_moe_up_kernel, "moe_up", xs, (wg, wu), plan, tm, tn, BF16)


def moe_down(hs, wd, plan, tm, tn=1024):
    return _moe_grouped(_moe_down_kernel, "moe_down", hs, (wd,), plan, tm, tn, F32)


def _combine_kernel(slot_ref, x_ref, wt_ref, gp_ref, ys_hbm, xo_ref, buf_ref, sem_ref, *, tm):
    i = pl.program_id(0)
    n_blocks = tm // COMBINE_ROWS

    def start_rows(tile, s, blk):
        for q in range(COMBINE_ROWS):
            r = blk * COMBINE_ROWS + q
            for k in range(TOP_K):
                _row_copy(ys_hbm, slot_ref[(tile * tm + r) * TOP_K + k], buf_ref.at[s, k], r,
                          sem_ref.at[s]).start()

    def wait_rows(s, blk):
        for q in range(COMBINE_ROWS):
            for k in range(TOP_K):
                _row_copy(ys_hbm, 0, buf_ref.at[s, k], blk * COMBINE_ROWS + q,
                          sem_ref.at[s]).wait()

    def combine_rows(s, blk):
        rows = pl.ds(pl.multiple_of(blk * COMBINE_ROWS, COMBINE_ROWS), COMBINE_ROWS)
        y = (wt_ref[rows, 0:1] * buf_ref[s, 0, rows, :]
             + wt_ref[rows, 1:2] * buf_ref[s, 1, rows, :])
        xo_ref[rows, :] = x_ref[rows, :] + _rms(y, gp_ref[...])

    def loop(body):
        lax.fori_loop(0, n_blocks, lambda blk, c: (body(blk), c)[1], 0, unroll=COMBINE_UNROLL)

    n = pl.num_programs(0)
    ahead = GATHER_SLOTS - 1

    @pl.when(i == 0)
    def _():
        for t in range(ahead):
            @pl.when(t < n)
            def _(t=t):
                loop(lambda blk: start_rows(t, t, blk))

    s = i % GATHER_SLOTS
    loop(lambda blk: wait_rows(s, blk))

    @pl.when(i + ahead < n)
    def _():
        def body(blk):
            start_rows(i + ahead, (i + ahead) % GATHER_SLOTS, blk)
            combine_rows(s, blk)
        loop(body)

    @pl.when(i + ahead >= n)
    def _():
        loop(lambda blk: combine_rows(s, blk))


def moe_combine_residual(x, ys, slot, wts, g_post, tm=256):
    M, D = x.shape
    tm = _tile(tm, M)
    row = pl.BlockSpec((tm, D), lambda i, sl: (i, 0))
    return pl.pallas_call(
        functools.partial(_combine_kernel, tm=tm),
        grid_spec=pltpu.PrefetchScalarGridSpec(
            num_scalar_prefetch=1,
            grid=(M // tm,),
            in_specs=[row, pl.BlockSpec((tm, LANES), lambda i, sl: (i, 0)),
                      pl.BlockSpec((1, D), lambda i, sl: (0, 0)),
                      pl.BlockSpec(memory_space=pl.ANY)],
            out_specs=row,
            scratch_shapes=[pltpu.VMEM((GATHER_SLOTS, TOP_K, tm, D), F32),
                            pltpu.SemaphoreType.DMA((GATHER_SLOTS,))]),
        out_shape=jax.ShapeDtypeStruct((M, D), F32),
        compiler_params=_cparams(1), name="moe_combine_residual",
    )(slot, x, wts, g_post.reshape(1, D), ys)


def _moe_plan(idx, n_tokens, tm):
    e_flat = idx.reshape(-1)
    onehot = (e_flat[:, None] == jnp.arange(N_EXPERTS)[None, :]).astype(jnp.int32)
    counts = jnp.sum(onehot, axis=0)
    rank = jnp.sum((jnp.cumsum(onehot, axis=0) - onehot) * onehot, axis=1)
    tiles_per = (counts + tm - 1) // tm
    tile_end = jnp.cumsum(tiles_per)
    tile_start = tile_end - tiles_per
    slot = (tile_start[e_flat] * tm + rank).astype(jnp.int32)
    n_tiles_max = (n_tokens * TOP_K) // tm + N_EXPERTS
    n_tiles = tile_end[-1]
    tile_ids = jnp.minimum(jnp.arange(n_tiles_max), n_tiles - 1)
    expert_of = lambda t: jnp.minimum(
        jnp.sum((tile_end[None, :] <= t[:, None]).astype(jnp.int32), axis=1), N_EXPERTS - 1)
    te = expert_of(tile_ids)
    group_end = tile_end[te]
    nxt = jnp.where(group_end < n_tiles, expert_of(jnp.minimum(group_end, n_tiles - 1)), -1)
    all_ids = jnp.arange(n_tiles_max)
    n_used = jnp.where(all_ids < n_tiles,
                       jnp.clip(counts[te] - (all_ids - tile_start[te]) * tm, 0, tm), 0)
    token_of_pair = jnp.arange(n_tokens * TOP_K, dtype=jnp.int32) // TOP_K
    token_of_slot = jnp.zeros((n_tiles_max * tm,), jnp.int32).at[slot].set(token_of_pair)
    plan = (te.astype(jnp.int32), n_tiles.reshape(1).astype(jnp.int32), nxt.astype(jnp.int32),
            n_used.astype(jnp.int32))
    return slot, token_of_slot, plan


def kernel(x, mem, mem_norm, mem_kv, l0_norm_mix_pre, l0_mix_in, l0_sgu_ln_g, l0_sgu_ln_b, l0_sgu_w, l0_sgu_b, l0_mix_out, l0_norm_mix_post, l0_norm_x_pre, l0_xq, l0_xo, l0_norm_x_post, l0_norm_ffn_pre, l0_ffn_gate, l0_ffn_up, l0_ffn_down, l0_norm_ffn_post, l1_norm_mix_pre, l1_mix_in, l1_gate_b, l1_conv, l1_head_norm, l1_mix_out, l1_norm_mix_post, l1_norm_x_pre, l1_xq, l1_xo, l1_norm_x_post, l1_norm_ffn_pre, l1_router, l1_moe_gate, l1_moe_up, l1_moe_down, l1_norm_ffn_post):
    B, S, D = x.shape
    T = B * S
    n_mem = mem.shape[1]
    xf = x.reshape(T, D)

    memn = rmsnorm_rows(mem.reshape(B * n_mem, D), mem_norm)
    kv = matmul_tiledk(memn, mem_kv, BF16, tn=512, tk=D).reshape(B, n_mem, 2 * D)

    hn = rmsnorm_rows(xf, l0_norm_mix_pre)
    z = matmul_fullk(hn, l0_mix_in, BF16, act="gelu")
    y = sgu_mix(z, l0_sgu_ln_g, l0_sgu_ln_b, l0_sgu_w, l0_sgu_b)
    y = matmul_fullk(y, l0_mix_out, BF16)
    xf, hn = residual_norm(xf, y, l0_norm_mix_post, l0_norm_x_pre)
    o = cross_attention(matmul_fullk(hn, l0_xq, BF16), kv, B)
    y = matmul_fullk(o, l0_xo, BF16)
    xf, hn = residual_norm(xf, y, l0_norm_x_post, l0_norm_ffn_pre)
    hmid = swiglu_up(hn, l0_ffn_gate, l0_ffn_up)
    y = matmul_tiledk(hmid, l0_ffn_down, BF16)
    xf, hn = residual_norm(xf, y, l0_norm_ffn_post, l1_norm_mix_pre)

    n_gates = 2 * MLSTM_HEADS
    n_main = l1_mix_in.shape[1] - n_gates
    w_in_t = l1_mix_in.T
    proj = matmul_fullk(hn, w_in_t, BF16, w_rows=(0, n_main))
    gates_t = mlstm_gates(hn, w_in_t[n_main:], l1_gate_b)
    y = mlstm_core(proj, gates_t, l1_conv, l1_head_norm, B)
    y = matmul_fullk(y, l1_mix_out, BF16)
    xf, hn = residual_norm(xf, y, l1_norm_mix_post, l1_norm_x_pre)
    o = cross_attention(matmul_fullk(hn, l1_xq, BF16), kv, B)
    y = matmul_fullk(o, l1_xo, BF16)
    xf, idx, wts = residual_router(xf, y, l1_norm_x_post, l1_norm_ffn_pre, l1_router)
    tm = min(MOE_TM, T)
    slot, token_of_slot, plan = _moe_plan(idx[:, :TOP_K], T, tm)
    xs = moe_dispatch(xf, l1_norm_ffn_pre, token_of_slot, plan[1], tm)
    hs = moe_up(xs, l1_moe_gate, l1_moe_up, plan, tm)
    ys = moe_down(hs, l1_moe_down, plan, tm)
    xf = moe_combine_residual(xf, ys, slot, wts, l1_norm_ffn_post)
    return xf.reshape(B, S, D)
```

```python
import functools

import jax
import jax.numpy as jnp
from jax import lax
from jax.experimental import pallas as pl
from jax.experimental.pallas import tpu as pltpu

F32 = jnp.float32
BF16 = jnp.bfloat16

EPS = 1e-6
CHUNK = 128
SGU_GROUPS = 8
MLSTM_HEADS = 8
CONV_WIDTH = 4
GATE_CAP = 15.0
X_HEADS = 4
N_EXPERTS = 8
TOP_K = 2
LANES = 128
MOE_TM = 512
MOE_ROW_CLASSES = (128, 256)
GATHER_ROWS = 16
GATHER_UNROLL = 2
GATHER_SLOTS = 3
COMBINE_ROWS = 8
COMBINE_UNROLL = 4
MLSTM_HEADS_PER_STEP = 2
CONV_TAIL = 8
VMEM_LIMIT = 56 * 1024 * 1024


def _cparams(n_axes, vmem=VMEM_LIMIT):
    return pltpu.CompilerParams(
        dimension_semantics=("arbitrary",) * n_axes, vmem_limit_bytes=vmem)


def _tile(pref, dim):
    t = min(pref, dim)
    assert dim % t == 0, (pref, dim)
    return t


def _rms(x, g):
    return x * lax.rsqrt(jnp.mean(x * x, axis=-1, keepdims=True) + EPS) * g


def _sigmoid(x):
    return 1.0 / (1.0 + jnp.exp(-x))


def _rms_kernel(x_ref, g_ref, o_ref):
    o_ref[...] = _rms(x_ref[...], g_ref[...]).astype(o_ref.dtype)


def rmsnorm_rows(x, g, tm=256):
    M, D = x.shape
    tm = _tile(tm, M)
    return pl.pallas_call(
        _rms_kernel,
        grid=(M // tm,),
        in_specs=[pl.BlockSpec((tm, D), lambda i: (i, 0)),
                  pl.BlockSpec((1, D), lambda i: (0, 0))],
        out_specs=pl.BlockSpec((tm, D), lambda i: (i, 0)),
        out_shape=jax.ShapeDtypeStruct((M, D), BF16),
        compiler_params=_cparams(1),
        name="rmsnorm_rows",
    )(x, g.reshape(1, D))


def _gelu(x):
    return 0.5 * x * (1.0 + lax.erf(x * 0.7071067811865476))


def _stream_weights(w_hbms, stage_refs, wb_refs, sem_ref, *, tn, nj, transposed, first=0,
                    first_tile=None):
    j = pl.program_id(0)
    i = pl.program_id(1)

    def copies(jj):
        off = pl.multiple_of(first + jj * tn, tn)
        return [pltpu.make_async_copy(
                    w.at[pl.ds(off, tn), :] if transposed else w.at[:, pl.ds(off, tn)],
                    s, sem_ref.at[n])
                for n, (w, s) in enumerate(zip(w_hbms, stage_refs))]

    @pl.when((j == 0) & (i == 0))
    def _():
        for c in copies(0):
            c.start()

    @pl.when(i == 0)
    def _():
        for c in copies(j):
            c.wait()
        weights = [(s[...].T if transposed else s[...]).astype(BF16) for s in stage_refs]
        for wb, w in zip(wb_refs, weights):
            wb[...] = w
        if first_tile is not None:
            first_tile(weights)

        @pl.when(j + 1 < nj)
        def _():
            for c in copies(j + 1):
                c.start()


def _mm_kernel(a_ref, w_hbm, o_ref, stage_ref, wb_ref, sem_ref, *, act, tn, nj, transposed,
               first):
    def tile(w):
        acc = jnp.dot(a_ref[...], w, preferred_element_type=F32)
        if act == "gelu":
            acc = _gelu(acc)
        o_ref[...] = acc.astype(o_ref.dtype)

    if transposed:
        _stream_weights((w_hbm,), (stage_ref,), (wb_ref,), sem_ref,
                        tn=tn, nj=nj, transposed=True, first=first)
        tile(wb_ref[...])
    else:
        _stream_weights((w_hbm,), (stage_ref,), (wb_ref,), sem_ref,
                        tn=tn, nj=nj, transposed=False, first=first,
                        first_tile=lambda ws: tile(ws[0]))

        @pl.when(pl.program_id(1) != 0)
        def _():
            tile(wb_ref[...])


def matmul_fullk(a, w, out_dtype, act=None, w_rows=None, tm=1024, tn=1024):
    M, K = a.shape
    transposed = w_rows is not None
    first, N = w_rows if transposed else (0, w.shape[1])
    tm = _tile(tm, M)
    tn = _tile(tn, N)
    assert first % tn == 0
    nj = N // tn
    stage_block = (tn, K) if transposed else (K, tn)
    return pl.pallas_call(
        functools.partial(_mm_kernel, act=act, tn=tn, nj=nj, transposed=transposed,
                          first=first),
        grid=(nj, M // tm),
        in_specs=[pl.BlockSpec((tm, K), lambda j, i: (i, 0)),
                  pl.BlockSpec(memory_space=pl.ANY)],
        out_specs=pl.BlockSpec((tm, tn), lambda j, i: (i, j)),
        out_shape=jax.ShapeDtypeStruct((M, N), out_dtype),
        scratch_shapes=[pltpu.VMEM(stage_block, F32), pltpu.VMEM((K, tn), BF16),
                        pltpu.SemaphoreType.DMA((1,))],
        compiler_params=_cparams(2),
        name="matmul_fullk" + ("_" + act if act else "") + ("_t" if transposed else ""),
    )(a, w)


def _swiglu_kernel(a_ref, wg_hbm, wu_hbm, o_ref, sg_ref, su_ref, wb_ref, sem_ref, *, tn, nj):
    def gate(g, u):
        o_ref[...] = (g * _sigmoid(g) * u).astype(o_ref.dtype)

    def first_tile(ws):
        a = a_ref[...]
        gate(jnp.dot(a, ws[0], preferred_element_type=F32),
             jnp.dot(a, ws[1], preferred_element_type=F32))

    _stream_weights((wg_hbm, wu_hbm), (sg_ref, su_ref),
                    (wb_ref.at[:, :tn], wb_ref.at[:, tn:]), sem_ref,
                    tn=tn, nj=nj, transposed=False, first_tile=first_tile)

    @pl.when(pl.program_id(1) != 0)
    def _():
        gu = jnp.dot(a_ref[...], wb_ref[...], preferred_element_type=F32)
        gate(gu[:, :tn], gu[:, tn:])


def swiglu_up(a, wg, wu, tm=1024, tn=512):
    M, K = a.shape
    F = wg.shape[1]
    tm = _tile(tm, M)
    tn = _tile(tn, F)
    nj = F // tn
    hbm = pl.BlockSpec(memory_space=pl.ANY)
    return pl.pallas_call(
        functools.partial(_swiglu_kernel, tn=tn, nj=nj),
        grid=(nj, M // tm),
        in_specs=[pl.BlockSpec((tm, K), lambda j, i: (i, 0)), hbm, hbm],
        out_specs=pl.BlockSpec((tm, tn), lambda j, i: (i, j)),
        out_shape=jax.ShapeDtypeStruct((M, F), BF16),
        scratch_shapes=[pltpu.VMEM((K, tn), F32), pltpu.VMEM((K, tn), F32),
                        pltpu.VMEM((K, 2 * tn), BF16), pltpu.SemaphoreType.DMA((2,))],
        compiler_params=_cparams(2),
        name="swiglu_up",
    )(a, wg, wu)


def _mmk_kernel(a_ref, w_ref, o_ref, acc_ref, *, nk):
    k = pl.program_id(2)

    @pl.when(k == 0)
    def _():
        acc_ref[...] = jnp.zeros_like(acc_ref)

    acc_ref[...] += jnp.dot(a_ref[...], w_ref[...].astype(BF16),
                            preferred_element_type=F32)

    @pl.when(k == nk - 1)
    def _():
        o_ref[...] = acc_ref[...].astype(o_ref.dtype)


def matmul_tiledk(a, w, out_dtype, tm=2048, tn=1024, tk=1024):
    M, K = a.shape
    N = w.shape[1]
    tm, tn, tk = _tile(tm, M), _tile(tn, N), _tile(tk, K)
    nk = K // tk
    return pl.pallas_call(
        functools.partial(_mmk_kernel, nk=nk),
        grid=(N // tn, M // tm, nk),
        in_specs=[pl.BlockSpec((tm, tk), lambda j, i, k: (i, k)),
                  pl.BlockSpec((tk, tn), lambda j, i, k: (k, j))],
        out_specs=pl.BlockSpec((tm, tn), lambda j, i, k: (i, j)),
        out_shape=jax.ShapeDtypeStruct((M, N), out_dtype),
        scratch_shapes=[pltpu.VMEM((tm, tn), F32)],
        compiler_params=_cparams(3),
        name="matmul_tiledk",
    )(a, w)


def _res_kernel(x_ref, y_ref, gp_ref, gn_ref, xo_ref, hn_ref):
    xn = x_ref[...] + _rms(y_ref[...].astype(F32), gp_ref[...])
    xo_ref[...] = xn
    hn_ref[...] = _rms(xn, gn_ref[...]).astype(hn_ref.dtype)


def residual_norm(x, y, g_post, g_next, tm=256):
    M, D = x.shape
    tm = _tile(tm, M)
    row = pl.BlockSpec((tm, D), lambda i: (i, 0))
    vec = pl.BlockSpec((1, D), lambda i: (0, 0))
    return pl.pallas_call(
        _res_kernel, grid=(M // tm,),
        in_specs=[row, row, vec, vec], out_specs=[row, row],
        out_shape=[jax.ShapeDtypeStruct((M, D), F32), jax.ShapeDtypeStruct((M, D), BF16)],
        compiler_params=_cparams(1), name="residual_norm",
    )(x, y, g_post.reshape(1, D), g_next.reshape(1, D))


def _sgu_kernel(u_ref, v_ref, lg_ref, lb_ref, w_ref, bt_ref, o_ref, wm_ref, *, groups):
    @pl.when(pl.program_id(0) == 0)
    def _():
        t = lax.broadcasted_iota(jnp.int32, (CHUNK, CHUNK), 0)
        s = lax.broadcasted_iota(jnp.int32, (CHUNK, CHUNK), 1)
        for g in range(groups):
            wm_ref[g] = jnp.where(t >= s, w_ref[g], 0.0).astype(BF16)

    v = v_ref[...].astype(F32)
    vc = v - jnp.mean(v, axis=-1, keepdims=True)
    vn = vc * lax.rsqrt(jnp.mean(vc * vc, axis=-1, keepdims=True) + EPS)
    vn = (vn * lg_ref[...] + lb_ref[...]).astype(BF16)
    tm, width = vn.shape
    gd = width // groups
    for c in range(tm // CHUNK):
        rows = slice(c * CHUNK, (c + 1) * CHUNK)
        for g in range(groups):
            cols = slice(g * gd, (g + 1) * gd)
            mixed = jnp.dot(wm_ref[g], vn[rows, cols], preferred_element_type=F32)
            mixed = mixed + bt_ref[:, g:g + 1]
            o_ref[rows, cols] = (u_ref[rows, cols].astype(F32) * mixed).astype(o_ref.dtype)


def sgu_mix(z, ln_g, ln_b, sgu_w, sgu_b, tm=256):
    M, W2 = z.shape
    W = W2 // 2
    G = sgu_w.shape[0]
    tm = _tile(tm, M)
    return pl.pallas_call(
        functools.partial(_sgu_kernel, groups=G),
        grid=(M // tm,),
        in_specs=[pl.BlockSpec((tm, W), lambda i: (i, 0)),
                  pl.BlockSpec((tm, W), lambda i: (i, 1)),
                  pl.BlockSpec((1, W), lambda i: (0, 0)),
                  pl.BlockSpec((1, W), lambda i: (0, 0)),
                  pl.BlockSpec((G, CHUNK, CHUNK), lambda i: (0, 0, 0)),
                  pl.BlockSpec((CHUNK, G), lambda i: (0, 0))],
        out_specs=pl.BlockSpec((tm, W), lambda i: (i, 0)),
        out_shape=jax.ShapeDtypeStruct((M, W), BF16),
        scratch_shapes=[pltpu.VMEM((G, CHUNK, CHUNK), BF16)],
        compiler_params=_cparams(1),
        name="sgu_mix",
    )(z, z, ln_g.reshape(1, W), ln_b.reshape(1, W), sgu_w, sgu_b.T)


def _xattn_kernel(q_ref, k_ref, v_ref, o_ref, *, heads, scale):
    hd = q_ref.shape[-1] // heads
    for h in range(heads):
        cols = slice(h * hd, (h + 1) * hd)
        s = lax.dot_general(q_ref[:, cols], k_ref[0, :, cols], (((1,), (1,)), ((), ())),
                            preferred_element_type=F32) * scale
        p = jnp.exp(s - jnp.max(s, axis=-1, keepdims=True))
        p = p / jnp.sum(p, axis=-1, keepdims=True)
        o = jnp.dot(p.astype(BF16), v_ref[0, :, cols], preferred_element_type=F32)
        o_ref[:, cols] = o.astype(o_ref.dtype)


def cross_attention(q, kv, batch, tm=512):
    M, D = q.shape
    S = M // batch
    n_mem = kv.shape[1]
    tm = _tile(tm, S)
    spt = S // tm
    return pl.pallas_call(
        functools.partial(_xattn_kernel, heads=X_HEADS, scale=(D // X_HEADS) ** -0.5),
        grid=(batch, spt),
        in_specs=[pl.BlockSpec((tm, D), lambda b, m: (b * spt + m, 0)),
                  pl.BlockSpec((1, n_mem, D), lambda b, m: (b, 0, 0)),
                  pl.BlockSpec((1, n_mem, D), lambda b, m: (b, 0, 1))],
        out_specs=pl.BlockSpec((tm, D), lambda b, m: (b * spt + m, 0)),
        out_shape=jax.ShapeDtypeStruct((M, D), BF16),
        compiler_params=_cparams(2),
        name="cross_attention",
    )(q, kv, kv)


def _gates_kernel(hn_ref, w_ref, b_ref, o_ref):
    acc = lax.dot_general(w_ref[...].astype(BF16), hn_ref[...], (((1,), (1,)), ((), ())),
                          preferred_element_type=F32)
    o_ref[...] = GATE_CAP * jnp.tanh((acc + b_ref[...]) / GATE_CAP)


def mlstm_gates(hn, w_gates_t, gate_b, tm=512):
    M, D = hn.shape
    n_gates = w_gates_t.shape[0]
    tm = _tile(tm, M)
    return pl.pallas_call(
        _gates_kernel,
        grid=(M // tm,),
        in_specs=[pl.BlockSpec((tm, D), lambda i: (i, 0)),
                  pl.BlockSpec((n_gates, D), lambda i: (0, 0)),
                  pl.BlockSpec((n_gates, 1), lambda i: (0, 0))],
        out_specs=pl.BlockSpec((n_gates, tm), lambda i: (0, i)),
        out_shape=jax.ShapeDtypeStruct((n_gates, M), F32),
        compiler_params=_cparams(1),
        name="mlstm_gates",
    )(hn, w_gates_t, gate_b.reshape(n_gates, 1))


def _conv_silu(x, tail_ref, w):
    prev = tail_ref[...]
    rid = lax.broadcasted_iota(jnp.int32, prev.shape, 0)
    acc = x * w[CONV_WIDTH - 1:CONV_WIDTH, :]
    for r in range(1, CONV_WIDTH):
        rolled = pltpu.roll(x, r, 0)
        head = jnp.where(rid < r, pltpu.roll(prev, r, 0), rolled[:CONV_TAIL])
        shifted = jnp.concatenate([head, rolled[CONV_TAIL:]], axis=0)
        acc = acc + shifted * w[CONV_WIDTH - 1 - r:CONV_WIDTH - r, :]
    tail_ref[...] = x[x.shape[0] - CONV_TAIL:, :]
    return acc * _sigmoid(acc)


def _mlstm_kernel(q_ref, k_ref, v_ref, o_ref, ig_ref, fg_ref, cq_ref, ck_ref, hg_ref,
                  y_ref, s_ref, n_ref, m_ref, qt_ref, kt_ref, *, heads):
    @pl.when(pl.program_id(2) == 0)
    def _():
        s_ref[...] = jnp.zeros_like(s_ref)
        n_ref[...] = jnp.zeros_like(n_ref)
        m_ref[...] = jnp.zeros_like(m_ref)
        qt_ref[...] = jnp.zeros_like(qt_ref)
        kt_ref[...] = jnp.zeros_like(kt_ref)

    dk = q_ref.shape[-1] // heads
    dv = v_ref.shape[-1] // heads
    for h in range(heads):
        qc = slice(h * dk, (h + 1) * dk)
        vc = slice(h * dv, (h + 1) * dv)
        q = _conv_silu(q_ref[:, qc].astype(F32), qt_ref.at[h], cq_ref[:, qc]) * dk ** -0.5
        k = _conv_silu(k_ref[:, qc].astype(F32), kt_ref.at[h], ck_ref[:, qc])
        _mlstm_head(q, k, v_ref[:, vc], o_ref[:, vc], ig_ref[h], fg_ref[h], hg_ref[:, vc],
                    y_ref.at[:, vc], s_ref.at[h], n_ref.at[h], m_ref.at[h])


def _mlstm_head(q, k, vb, ob, i_row, f_row, hg, y_ref, s_ref, n_ref, m_ref):
    L = CHUNK
    qb = q.astype(BF16)
    kb = k.astype(BF16)
    v = vb.astype(F32)

    t_id = lax.broadcasted_iota(jnp.int32, (L, L), 0)
    s_id = lax.broadcasted_iota(jnp.int32, (L, L), 1)
    eye = t_id == s_id
    causal = t_id >= s_id

    logf_row = -(jnp.maximum(-f_row, 0.0) + jnp.log1p(jnp.exp(-jnp.abs(f_row))))
    logf_b = jnp.broadcast_to(logf_row, (L, L))
    i_b = jnp.broadcast_to(i_row, (L, L))
    logf_col = jnp.sum(jnp.where(eye, logf_b, 0.0), axis=1, keepdims=True)
    i_col = jnp.sum(jnp.where(eye, i_b, 0.0), axis=1, keepdims=True)
    b_col = jnp.sum(jnp.where(causal, logf_b, 0.0), axis=1, keepdims=True)
    b_row = jnp.sum(jnp.where(t_id <= s_id, jnp.broadcast_to(logf_col, (L, L)), 0.0),
                    axis=0, keepdims=True)
    g_tot = jnp.sum(logf_row, axis=1, keepdims=True)
    m0 = m_ref[:, 0:1]

    d_log = jnp.where(causal, b_col - b_row + i_row, -jnp.inf)
    m_inter = b_col + m0
    m_t = jnp.maximum(m_inter, jnp.max(d_log, axis=1, keepdims=True))
    qk = lax.dot_general(qb, kb, (((1,), (1,)), ((), ())), preferred_element_type=F32)
    scores = qk * jnp.exp(d_log - m_t)
    inter = jnp.exp(m_inter - m_t)
    num = (jnp.dot(scores.astype(BF16), vb, preferred_element_type=F32)
           + inter * jnp.dot(qb, s_ref[...].astype(BF16), preferred_element_type=F32))
    den = (jnp.sum(scores, axis=1, keepdims=True)
           + inter * jnp.sum(q * n_ref[...], axis=1, keepdims=True))
    h = num / jnp.maximum(jnp.abs(den), jnp.exp(-m_t))
    y_ref[...] = (_sigmoid(ob.astype(F32)) * _rms(h, hg)).astype(y_ref.dtype)

    w_log = g_tot - b_col + i_col
    a = jnp.max(w_log, axis=0, keepdims=True)
    w = jnp.exp(w_log - a)
    s_loc = lax.dot_general(kb, (w * v).astype(BF16), (((0,), (0,)), ((), ())),
                            preferred_element_type=F32)
    n_loc = jnp.sum(w * k, axis=0, keepdims=True)
    m_new = jnp.maximum(g_tot + m0, a)
    sc_prev = jnp.exp(g_tot + m0 - m_new)
    sc_loc = jnp.exp(a - m_new)
    s_ref[...] = sc_prev * s_ref[...] + sc_loc * s_loc
    n_ref[...] = sc_prev * n_ref[...] + sc_loc * n_loc
    m_ref[...] = jnp.broadcast_to(m_new, m_ref.shape)


def mlstm_core(proj, gates_t, conv_w, head_g, batch):
    M = proj.shape[0]
    H = MLSTM_HEADS
    G = MLSTM_HEADS_PER_STEP
    assert H % G == 0
    ng = H // G
    qkw = conv_w.shape[1]
    dk = qkw // (2 * H)
    dv = head_g.shape[0] // H
    nc = M // batch // CHUNK
    assert qkw % (G * dv) == 0
    v0 = qkw // (G * dv)
    gates4 = gates_t.reshape(2 * ng, G, 1, M)
    row = lambda b, c: b * nc + c
    return pl.pallas_call(
        functools.partial(_mlstm_kernel, heads=G),
        grid=(batch, ng, nc),
        in_specs=[pl.BlockSpec((CHUNK, G * dk), lambda b, g, c: (row(b, c), g)),
                  pl.BlockSpec((CHUNK, G * dk), lambda b, g, c: (row(b, c), ng + g)),
                  pl.BlockSpec((CHUNK, G * dv), lambda b, g, c: (row(b, c), v0 + g)),
                  pl.BlockSpec((CHUNK, G * dv), lambda b, g, c: (row(b, c), v0 + ng + g)),
                  pl.BlockSpec((None, G, 1, CHUNK), lambda b, g, c: (g, 0, 0, row(b, c))),
                  pl.BlockSpec((None, G, 1, CHUNK), lambda b, g, c: (ng + g, 0, 0, row(b, c))),
                  pl.BlockSpec((CONV_WIDTH, G * dk), lambda b, g, c: (0, g)),
                  pl.BlockSpec((CONV_WIDTH, G * dk), lambda b, g, c: (0, ng + g)),
                  pl.BlockSpec((1, G * dv), lambda b, g, c: (0, g))],
        out_specs=pl.BlockSpec((CHUNK, G * dv), lambda b, g, c: (row(b, c), g)),
        out_shape=jax.ShapeDtypeStruct((M, H * dv), BF16),
        scratch_shapes=[pltpu.VMEM((G, dk, dv), F32), pltpu.VMEM((G, 1, dk), F32),
                        pltpu.VMEM((G, 1, LANES), F32),
                        pltpu.VMEM((G, CONV_TAIL, dk), F32), pltpu.VMEM((G, CONV_TAIL, dk), F32)],
        compiler_params=_cparams(3),
        name="mlstm_core",
    )(proj, proj, proj, proj, gates4, gates4, conv_w, conv_w, head_g.reshape(1, H * dv))


def _res_router_kernel(x_ref, y_ref, gp_ref, gn_ref, w2_ref, xo_ref, idx_ref, wt_ref):
    xn = x_ref[...] + _rms(y_ref[...].astype(F32), gp_ref[...])
    xo_ref[...] = xn
    hn = _rms(xn, gn_ref[...])
    hi = hn.astype(BF16)
    lo = (hn - hi.astype(F32)).astype(BF16)
    t = jnp.dot(hi, w2_ref[...], preferred_element_type=F32)
    logits = (t[:, :LANES] + t[:, LANES:]
              + jnp.dot(lo, w2_ref[:, :LANES], preferred_element_type=F32))
    lane = lax.broadcasted_iota(jnp.int32, logits.shape, 1)
    neg = -jnp.inf
    l1 = jnp.where(lane < N_EXPERTS, logits, neg)
    m1 = jnp.max(l1, axis=1, keepdims=True)
    i1 = jnp.min(jnp.where(l1 == m1, lane, LANES), axis=1, keepdims=True)
    l2 = jnp.where(lane == i1, neg, l1)
    m2 = jnp.max(l2, axis=1, keepdims=True)
    i2 = jnp.min(jnp.where(l2 == m2, lane, LANES), axis=1, keepdims=True)
    r = jnp.exp(m2 - m1)
    w1 = 1.0 / (1.0 + r)
    w2 = r / (1.0 + r)
    idx_ref[...] = jnp.where(lane == 0, i1, jnp.where(lane == 1, i2, 0))
    wt_ref[...] = jnp.where(lane == 0, w1, jnp.where(lane == 1, w2, 0.0))


def residual_router(x, y, g_post, g_next, w_router, tm=256):
    M, D = x.shape
    tm = _tile(tm, M)
    w_pad = jnp.pad(w_router, ((0, 0), (0, LANES - w_router.shape[1])))
    w_hi = w_pad.astype(BF16)
    w_lo = (w_pad - w_hi.astype(F32)).astype(BF16)
    row = pl.BlockSpec((tm, D), lambda i: (i, 0))
    vec = pl.BlockSpec((1, D), lambda i: (0, 0))
    out = pl.BlockSpec((tm, LANES), lambda i: (i, 0))
    return pl.pallas_call(
        _res_router_kernel, grid=(M // tm,),
        in_specs=[row, row, vec, vec, pl.BlockSpec((D, 2 * LANES), lambda i: (0, 0))],
        out_specs=[row, out, out],
        out_shape=[jax.ShapeDtypeStruct((M, D), F32),
                   jax.ShapeDtypeStruct((M, LANES), jnp.int32),
                   jax.ShapeDtypeStruct((M, LANES), F32)],
        compiler_params=_cparams(1), name="residual_router",
    )(x, y, g_post.reshape(1, D), g_next.reshape(1, D),
      jnp.concatenate([w_hi, w_lo], axis=1))


def _row_copy(src_hbm, src_row, dst_ref, dst_row, sem):
    return pltpu.make_async_copy(src_hbm.at[pl.ds(src_row, 1)],
                                 dst_ref.at[pl.ds(dst_row, 1)], sem)


def _dispatch_kernel(tok_ref, nt_ref, x_hbm, g_ref, o_ref, buf_ref, sem_ref, *, tm):
    i = pl.program_id(0)
    nt = nt_ref[0]
    n_blocks = tm // GATHER_ROWS

    def start_rows(tile, slot, blk):
        for q in range(GATHER_ROWS):
            r = blk * GATHER_ROWS + q
            _row_copy(x_hbm, tok_ref[tile * tm + r], buf_ref.at[slot], r,
                      sem_ref.at[slot]).start()

    def wait_rows(slot, blk):
        for q in range(GATHER_ROWS):
            _row_copy(x_hbm, 0, buf_ref.at[slot], blk * GATHER_ROWS + q,
                      sem_ref.at[slot]).wait()

    def norm_rows(slot, blk):
        rows = pl.ds(pl.multiple_of(blk * GATHER_ROWS, GATHER_ROWS), GATHER_ROWS)
        o_ref[rows, :] = _rms(buf_ref[slot, rows, :], g_ref[...]).astype(o_ref.dtype)

    def loop(body):
        lax.fori_loop(0, n_blocks, lambda blk, c: (body(blk), c)[1], 0, unroll=GATHER_UNROLL)

    ahead = GATHER_SLOTS - 1

    @pl.when(i == 0)
    def _():
        for t in range(ahead):
            @pl.when(t < nt)
            def _(t=t):
                loop(lambda blk: start_rows(t, t, blk))

    slot = i % GATHER_SLOTS

    @pl.when(i < nt)
    def _():
        loop(lambda blk: wait_rows(slot, blk))

    @pl.when(i + ahead < nt)
    def _():
        def body(blk):
            start_rows(i + ahead, (i + ahead) % GATHER_SLOTS, blk)
            norm_rows(slot, blk)
        loop(body)

    @pl.when((i < nt) & (i + ahead >= nt))
    def _():
        loop(lambda blk: norm_rows(slot, blk))

    @pl.when(i >= nt)
    def _():
        o_ref[...] = jnp.zeros_like(o_ref)


def moe_dispatch(x, g, token_of_slot, n_tiles, tm):
    T, D = x.shape
    P = token_of_slot.shape[0]
    return pl.pallas_call(
        functools.partial(_dispatch_kernel, tm=tm),
        grid_spec=pltpu.PrefetchScalarGridSpec(
            num_scalar_prefetch=2,
            grid=(P // tm,),
            in_specs=[pl.BlockSpec(memory_space=pl.ANY),
                      pl.BlockSpec((1, D), lambda i, tok, nt: (0, 0))],
            out_specs=pl.BlockSpec((tm, D), lambda i, tok, nt: (i, 0)),
            scratch_shapes=[pltpu.VMEM((GATHER_SLOTS, tm, D), F32),
                            pltpu.SemaphoreType.DMA((GATHER_SLOTS,))]),
        out_shape=jax.ShapeDtypeStruct((P, D), BF16),
        compiler_params=_cparams(1),
        name="moe_dispatch",
    )(token_of_slot, n_tiles, x, g.reshape(1, D))


def _weights_changed(te_ref, i):
    return (i == 0) | (te_ref[i] != te_ref[jnp.maximum(i - 1, 0)])


def _stage_copies(w_hbms, e, j, tn, stage_refs, sem_ref):
    col = pl.multiple_of(j * tn, tn)
    return [pltpu.make_async_copy(w.at[e, :, pl.ds(col, tn)], s, sem_ref.at[n])
            for n, (w, s) in enumerate(zip(w_hbms, stage_refs))]


def _restage_weights(te_ref, nx_ref, w_hbms, stage_refs, wb_refs, sem_ref, *, tn, nj):
    j = pl.program_id(0)
    i = pl.program_id(1)

    @pl.when((j == 0) & (i == 0))
    def _():
        for c in _stage_copies(w_hbms, te_ref[0], 0, tn, stage_refs, sem_ref):
            c.start()

    @pl.when(_weights_changed(te_ref, i))
    def _():
        for c in _stage_copies(w_hbms, te_ref[i], j, tn, stage_refs, sem_ref):
            c.wait()
        for s, wb in zip(stage_refs, wb_refs):
            wb[...] = s[...].astype(BF16)
        nxt = nx_ref[i]
        last = nxt < 0
        e_next = jnp.where(last, te_ref[0], nxt)
        j_next = jnp.where(last, j + 1, j)

        @pl.when(j_next < nj)
        def _():
            for c in _stage_copies(w_hbms, e_next, j_next, tn, stage_refs, sem_ref):
                c.start()


def _for_used_rows(n_used, o_ref, compute):
    tm = o_ref.shape[0]
    classes = sorted({min(c, tm) for c in MOE_ROW_CLASSES} | {tm})
    lo = 0
    for r in classes:
        @pl.when((n_used > lo) & (n_used <= r))
        def _(r=r):
            o_ref[:r, :] = compute(r).astype(o_ref.dtype)
            if r < tm:
                o_ref[r:, :] = jnp.zeros((tm - r, o_ref.shape[1]), o_ref.dtype)
        lo = r

    @pl.when(n_used == 0)
    def _():
        o_ref[...] = jnp.zeros_like(o_ref)


def _moe_up_kernel(te_ref, nt_ref, nx_ref, nu_ref, a_ref, wg_hbm, wu_hbm, o_ref,
                   sg_ref, su_ref, wgb_ref, wub_ref, sem_ref, *, tn, nj):
    _restage_weights(te_ref, nx_ref, (wg_hbm, wu_hbm), (sg_ref, su_ref),
                     (wgb_ref, wub_ref), sem_ref, tn=tn, nj=nj)

    def compute(r):
        a = a_ref[:r, :]
        g = jnp.dot(a, wgb_ref[...], preferred_element_type=F32)
        u = jnp.dot(a, wub_ref[...], preferred_element_type=F32)
        return g * _sigmoid(g) * u

    _for_used_rows(nu_ref[pl.program_id(1)], o_ref, compute)


def _moe_down_kernel(te_ref, nt_ref, nx_ref, nu_ref, a_ref, w_hbm, o_ref,
                     s_ref, wb_ref, sem_ref, *, tn, nj):
    _restage_weights(te_ref, nx_ref, (w_hbm,), (s_ref,), (wb_ref,), sem_ref, tn=tn, nj=nj)
    _for_used_rows(nu_ref[pl.program_id(1)], o_ref,
                   lambda r: jnp.dot(a_ref[:r, :], wb_ref[...], preferred_element_type=F32))


def _moe_grouped(kernel_fn, name, a, weights, plan, tm, tn, out_dtype):
    te, n_tiles, nxt, n_used = plan
    P, K = a.shape
    N = weights[0].shape[2]
    tn = _tile(tn, N)
    nj = N // tn
    used = lambda i, nt: jnp.minimum(i, nt[0] - 1)
    n_w = len(weights)
    return pl.pallas_call(
        functools.partial(kernel_fn, tn=tn, nj=nj),
        grid_spec=pltpu.PrefetchScalarGridSpec(
            num_scalar_prefetch=4,
            grid=(nj, P // tm),
            in_specs=[pl.BlockSpec((tm, K), lambda j, i, te, nt, nx, nu: (used(i, nt), 0))]
                     + [pl.BlockSpec(memory_space=pl.ANY)] * n_w,
            out_specs=pl.BlockSpec((tm, tn), lambda j, i, te, nt, nx, nu: (i, j)),
            scratch_shapes=[pltpu.VMEM((K, tn), F32)] * n_w + [pltpu.VMEM((K, tn), BF16)] * n_w
                           + [pltpu.SemaphoreType.DMA((n_w,))]),
        out_shape=jax.ShapeDtypeStruct((P, N), out_dtype),
        compiler_params=_cparams(2),
        name=name,
    )(te, n_tiles, nxt, n_used, a, *weights)


def moe_up(xs, wg, wu, plan, tm, tn=512):
    return _moe_grouped(_moe_up_kernel, "moe_up", xs, (wg, wu), plan, tm, tn, BF16)


def moe_down(hs, wd, plan, tm, tn=1024):
    return _moe_grouped(_moe_down_kernel, "moe_down", hs, (wd,), plan, tm, tn, F32)


def _combine_kernel(slot_ref, x_ref, wt_ref, gp_ref, ys_hbm, xo_ref, buf_ref, sem_ref, *, tm):
    i = pl.program_id(0)
    n_blocks = tm // COMBINE_ROWS

    def start_rows(tile, s, blk):
        for q in range(COMBINE_ROWS):
            r = blk * COMBINE_ROWS + q
            for k in range(TOP_K):
                _row_copy(ys_hbm, slot_ref[(tile * tm + r) * TOP_K + k], buf_ref.at[s, k], r,
                          sem_ref.at[s]).start()

    def wait_rows(s, blk):
        for q in range(COMBINE_ROWS):
            for k in range(TOP_K):
                _row_copy(ys_hbm, 0, buf_ref.at[s, k], blk * COMBINE_ROWS + q,
                          sem_ref.at[s]).wait()

    def combine_rows(s, blk):
        rows = pl.ds(pl.multiple_of(blk * COMBINE_ROWS, COMBINE_ROWS), COMBINE_ROWS)
        y = (wt_ref[rows, 0:1] * buf_ref[s, 0, rows, :]
             + wt_ref[rows, 1:2] * buf_ref[s, 1, rows, :])
        xo_ref[rows, :] = x_ref[rows, :] + _rms(y, gp_ref[...])

    def loop(body):
        lax.fori_loop(0, n_blocks, lambda blk, c: (body(blk), c)[1], 0, unroll=COMBINE_UNROLL)

    n = pl.num_programs(0)
    ahead = GATHER_SLOTS - 1

    @pl.when(i == 0)
    def _():
        for t in range(ahead):
            @pl.when(t < n)
            def _(t=t):
                loop(lambda blk: start_rows(t, t, blk))

    s = i % GATHER_SLOTS
    loop(lambda blk: wait_rows(s, blk))

    @pl.when(i + ahead < n)
    def _():
        def body(blk):
            start_rows(i + ahead, (i + ahead) % GATHER_SLOTS, blk)
            combine_rows(s, blk)
        loop(body)

    @pl.when(i + ahead >= n)
    def _():
        loop(lambda blk: combine_rows(s, blk))


def moe_combine_residual(x, ys, slot, wts, g_post, tm=256):
    M, D = x.shape
    tm = _tile(tm, M)
    row = pl.BlockSpec((tm, D), lambda i, sl: (i, 0))
    return pl.pallas_call(
        functools.partial(_combine_kernel, tm=tm),
        grid_spec=pltpu.PrefetchScalarGridSpec(
            num_scalar_prefetch=1,
            grid=(M // tm,),
            in_specs=[row, pl.BlockSpec((tm, LANES), lambda i, sl: (i, 0)),
                      pl.BlockSpec((1, D), lambda i, sl: (0, 0)),
                      pl.BlockSpec(memory_space=pl.ANY)],
            out_specs=row,
            scratch_shapes=[pltpu.VMEM((GATHER_SLOTS, TOP_K, tm, D), F32),
                            pltpu.SemaphoreType.DMA((GATHER_SLOTS,))]),
        out_shape=jax.ShapeDtypeStruct((M, D), F32),
        compiler_params=_cparams(1), name="moe_combine_residual",
    )(slot, x, wts, g_post.reshape(1, D), ys)


def _moe_plan(idx, n_tokens, tm):
    e_flat = idx.reshape(-1)
    onehot = (e_flat[:, None] == jnp.arange(N_EXPERTS)[None, :]).astype(jnp.int32)
    counts = jnp.sum(onehot, axis=0)
    rank = jnp.sum((jnp.cumsum(onehot, axis=0) - onehot) * onehot, axis=1)
    tiles_per = (counts + tm - 1) // tm
    tile_end = jnp.cumsum(tiles_per)
    tile_start = tile_end - tiles_per
    slot = (tile_start[e_flat] * tm + rank).astype(jnp.int32)
    n_tiles_max = (n_tokens * TOP_K) // tm + N_EXPERTS
    n_tiles = tile_end[-1]
    tile_ids = jnp.minimum(jnp.arange(n_tiles_max), n_tiles - 1)
    expert_of = lambda t: jnp.minimum(
        jnp.sum((tile_end[None, :] <= t[:, None]).astype(jnp.int32), axis=1), N_EXPERTS - 1)
    te = expert_of(tile_ids)
    group_end = tile_end[te]
    nxt = jnp.where(group_end < n_tiles, expert_of(jnp.minimum(group_end, n_tiles - 1)), -1)
    all_ids = jnp.arange(n_tiles_max)
    n_used = jnp.where(all_ids < n_tiles,
                       jnp.clip(counts[te] - (all_ids - tile_start[te]) * tm, 0, tm), 0)
    token_of_pair = jnp.arange(n_tokens * TOP_K, dtype=jnp.int32) // TOP_K
    token_of_slot = jnp.zeros((n_tiles_max * tm,), jnp.int32).at[slot].set(token_of_pair)
    plan = (te.astype(jnp.int32), n_tiles.reshape(1).astype(jnp.int32), nxt.astype(jnp.int32),
            n_used.astype(jnp.int32))
    return slot, token_of_slot, plan


def kernel(x, mem, mem_norm, mem_kv, l0_norm_mix_pre, l0_mix_in, l0_sgu_ln_g, l0_sgu_ln_b, l0_sgu_w, l0_sgu_b, l0_mix_out, l0_norm_mix_post, l0_norm_x_pre, l0_xq, l0_xo, l0_norm_x_post, l0_norm_ffn_pre, l0_ffn_gate, l0_ffn_up, l0_ffn_down, l0_norm_ffn_post, l1_norm_mix_pre, l1_mix_in, l1_gate_b, l1_conv, l1_head_norm, l1_mix_out, l1_norm_mix_post, l1_norm_x_pre, l1_xq, l1_xo, l1_norm_x_post, l1_norm_ffn_pre, l1_router, l1_moe_gate, l1_moe_up, l1_moe_down, l1_norm_ffn_post):
    B, S, D = x.shape
    T = B * S
    n_mem = mem.shape[1]
    xf = x.reshape(T, D)

    memn = rmsnorm_rows(mem.reshape(B * n_mem, D), mem_norm)
    kv = matmul_tiledk(memn, mem_kv, BF16, tn=512, tk=D).reshape(B, n_mem, 2 * D)

    hn = rmsnorm_rows(xf, l0_norm_mix_pre)
    z = matmul_fullk(hn, l0_mix_in, BF16, act="gelu")
    y = sgu_mix(z, l0_sgu_ln_g, l0_sgu_ln_b, l0_sgu_w, l0_sgu_b)
    y = matmul_fullk(y, l0_mix_out, BF16)
    xf, hn = residual_norm(xf, y, l0_norm_mix_post, l0_norm_x_pre)
    o = cross_attention(matmul_fullk(hn, l0_xq, BF16), kv, B)
    y = matmul_fullk(o, l0_xo, BF16)
    xf, hn = residual_norm(xf, y, l0_norm_x_post, l0_norm_ffn_pre)
    hmid = swiglu_up(hn, l0_ffn_gate, l0_ffn_up)
    y = matmul_tiledk(hmid, l0_ffn_down, BF16)
    xf, hn = residual_norm(xf, y, l0_norm_ffn_post, l1_norm_mix_pre)

    n_gates = 2 * MLSTM_HEADS
    n_main = l1_mix_in.shape[1] - n_gates
    w_in_t = l1_mix_in.T
    proj = matmul_fullk(hn, w_in_t, BF16, w_rows=(0, n_main))
    gates_t = mlstm_gates(hn, w_in_t[n_main:], l1_gate_b)
    y = mlstm_core(proj, gates_t, l1_conv, l1_head_norm, B)
    y = matmul_fullk(y, l1_mix_out, BF16)
    xf, hn = residual_norm(xf, y, l1_norm_mix_post, l1_norm_x_pre)
    o = cross_attention(matmul_fullk(hn, l1_xq, BF16), kv, B)
    y = matmul_fullk(o, l1_xo, BF16)
    xf, idx, wts = residual_router(xf, y, l1_norm_x_post, l1_norm_ffn_pre, l1_router)
    tm = min(MOE_TM, T)
    slot, token_of_slot, plan = _moe_plan(idx[:, :TOP_K], T, tm)
    xs = moe_dispatch(xf, l1_norm_ffn_pre, token_of_slot, plan[1], tm)
    hs = moe_up(xs, l1_moe_gate, l1_moe_up, plan, tm)
    ys = moe_down(hs, l1_moe_down, plan, tm)
    xf = moe_combine_residual(xf, ys, slot, wts, l1_norm_ffn_post)
    return xf.reshape(B, S, D)
```

```python
import functools

import jax
import jax.numpy as jnp
from jax import lax
from jax.experimental import pallas as pl
from jax.experimental.pallas import tpu as pltpu

F32 = jnp.float32
BF16 = jnp.bfloat16

EPS = 1e-6
CHUNK = 128
SGU_GROUPS = 8
MLSTM_HEADS = 8
CONV_WIDTH = 4
GATE_CAP = 15.0
X_HEADS = 4
N_EXPERTS = 8
TOP_K = 2
LANES = 128
MOE_TM = 512
MOE_ROW_CLASSES = (128, 256)
GATHER_ROWS = 16
GATHER_UNROLL = 2
GATHER_SLOTS = 3
COMBINE_ROWS = 8
COMBINE_UNROLL = 4
MLSTM_HEADS_PER_STEP = 2
CONV_TAIL = 8
VMEM_LIMIT = 56 * 1024 * 1024


def _cparams(n_axes, vmem=VMEM_LIMIT):
    return pltpu.CompilerParams(
        dimension_semantics=("arbitrary",) * n_axes, vmem_limit_bytes=vmem)


def _tile(pref, dim):
    t = min(pref, dim)
    assert dim % t == 0, (pref, dim)
    return t


def _rms(x, g):
    return x * lax.rsqrt(jnp.mean(x * x, axis=-1, keepdims=True) + EPS) * g


def _sigmoid(x):
    return 1.0 / (1.0 + jnp.exp(-x))


def _rms_kernel(x_ref, g_ref, o_ref):
    o_ref[...] = _rms(x_ref[...], g_ref[...]).astype(o_ref.dtype)


def rmsnorm_rows(x, g, tm=256):
    M, D = x.shape
    tm = _tile(tm, M)
    return pl.pallas_call(
        _rms_kernel,
        grid=(M // tm,),
        in_specs=[pl.BlockSpec((tm, D), lambda i: (i, 0)),
                  pl.BlockSpec((1, D), lambda i: (0, 0))],
        out_specs=pl.BlockSpec((tm, D), lambda i: (i, 0)),
        out_shape=jax.ShapeDtypeStruct((M, D), BF16),
        compiler_params=_cparams(1),
        name="rmsnorm_rows",
    )(x, g.reshape(1, D))


def _gelu(x):
    return 0.5 * x * (1.0 + lax.erf(x * 0.7071067811865476))


def _stream_weights(w_hbms, stage_refs, wb_refs, sem_ref, *, tn, nj, transposed, first=0,
                    first_tile=None):
    j = pl.program_id(0)
    i = pl.program_id(1)

    def copies(jj):
        off = pl.multiple_of(first + jj * tn, tn)
        return [pltpu.make_async_copy(
                    w.at[pl.ds(off, tn), :] if transposed else w.at[:, pl.ds(off, tn)],
                    s, sem_ref.at[n])
                for n, (w, s) in enumerate(zip(w_hbms, stage_refs))]

    @pl.when((j == 0) & (i == 0))
    def _():
        for c in copies(0):
            c.start()

    @pl.when(i == 0)
    def _():
        for c in copies(j):
            c.wait()
        weights = [(s[...].T if transposed else s[...]).astype(BF16) for s in stage_refs]
        for wb, w in zip(wb_refs, weights):
            wb[...] = w
        if first_tile is not None:
            first_tile(weights)

        @pl.when(j + 1 < nj)
        def _():
            for c in copies(j + 1):
                c.start()


def _mm_kernel(a_ref, w_hbm, o_ref, stage_ref, wb_ref, sem_ref, *, act, tn, nj, transposed,
               first):
    def tile(w):
        acc = jnp.dot(a_ref[...], w, preferred_element_type=F32)
        if act == "gelu":
            acc = _gelu(acc)
        o_ref[...] = acc.astype(o_ref.dtype)

    _stream_weights((w_hbm,), (stage_ref,), (wb_ref,), sem_ref,
                    tn=tn, nj=nj, transposed=transposed, first=first,
                    first_tile=lambda ws: tile(ws[0]))

    @pl.when(pl.program_id(1) != 0)
    def _():
        tile(wb_ref[...])


def matmul_fullk(a, w, out_dtype, act=None, w_rows=None, tm=1024, tn=1024):
    M, K = a.shape
    transposed = w_rows is not None
    first, N = w_rows if transposed else (0, w.shape[1])
    tm = _tile(tm, M)
    tn = _tile(tn, N)
    assert first % tn == 0
    nj = N // tn
    stage_block = (tn, K) if transposed else (K, tn)
    return pl.pallas_call(
        functools.partial(_mm_kernel, act=act, tn=tn, nj=nj, transposed=transposed,
                          first=first),
        grid=(nj, M // tm),
        in_specs=[pl.BlockSpec((tm, K), lambda j, i: (i, 0)),
                  pl.BlockSpec(memory_space=pl.ANY)],
        out_specs=pl.BlockSpec((tm, tn), lambda j, i: (i, j)),
        out_shape=jax.ShapeDtypeStruct((M, N), out_dtype),
        scratch_shapes=[pltpu.VMEM(stage_block, F32), pltpu.VMEM((K, tn), BF16),
                        pltpu.SemaphoreType.DMA((1,))],
        compiler_params=_cparams(2),
        name="matmul_fullk" + ("_" + act if act else "") + ("_t" if transposed else ""),
    )(a, w)


def _swiglu_kernel(a_ref, wg_hbm, wu_hbm, o_ref, sg_ref, su_ref, wb_ref, sem_ref, *, tn, nj):
    def gate(g, u):
        o_ref[...] = (g * _sigmoid(g) * u).astype(o_ref.dtype)

    def first_tile(ws):
        a = a_ref[...]
        gate(jnp.dot(a, ws[0], preferred_element_type=F32),
             jnp.dot(a, ws[1], preferred_element_type=F32))

    _stream_weights((wg_hbm, wu_hbm), (sg_ref, su_ref),
                    (wb_ref.at[:, :tn], wb_ref.at[:, tn:]), sem_ref,
                    tn=tn, nj=nj, transposed=False, first_tile=first_tile)

    @pl.when(pl.program_id(1) != 0)
    def _():
        gu = jnp.dot(a_ref[...], wb_ref[...], preferred_element_type=F32)
        gate(gu[:, :tn], gu[:, tn:])


def swiglu_up(a, wg, wu, tm=1024, tn=512):
    M, K = a.shape
    F = wg.shape[1]
    tm = _tile(tm, M)
    tn = _tile(tn, F)
    nj = F // tn
    hbm = pl.BlockSpec(memory_space=pl.ANY)
    return pl.pallas_call(
        functools.partial(_swiglu_kernel, tn=tn, nj=nj),
        grid=(nj, M // tm),
        in_specs=[pl.BlockSpec((tm, K), lambda j, i: (i, 0)), hbm, hbm],
        out_specs=pl.BlockSpec((tm, tn), lambda j, i: (i, j)),
        out_shape=jax.ShapeDtypeStruct((M, F), BF16),
        scratch_shapes=[pltpu.VMEM((K, tn), F32), pltpu.VMEM((K, tn), F32),
                        pltpu.VMEM((K, 2 * tn), BF16), pltpu.SemaphoreType.DMA((2,))],
        compiler_params=_cparams(2),
        name="swiglu_up",
    )(a, wg, wu)


def _mmk_kernel(a_ref, w_ref, o_ref, acc_ref, *, nk):
    k = pl.program_id(2)

    @pl.when(k == 0)
    def _():
        acc_ref[...] = jnp.zeros_like(acc_ref)

    acc_ref[...] += jnp.dot(a_ref[...], w_ref[...].astype(BF16),
                            preferred_element_type=F32)

    @pl.when(k == nk - 1)
    def _():
        o_ref[...] = acc_ref[...].astype(o_ref.dtype)


def matmul_tiledk(a, w, out_dtype, tm=2048, tn=1024, tk=1024):
    M, K = a.shape
    N = w.shape[1]
    tm, tn, tk = _tile(tm, M), _tile(tn, N), _tile(tk, K)
    nk = K // tk
    return pl.pallas_call(
        functools.partial(_mmk_kernel, nk=nk),
        grid=(N // tn, M // tm, nk),
        in_specs=[pl.BlockSpec((tm, tk), lambda j, i, k: (i, k)),
                  pl.BlockSpec((tk, tn), lambda j, i, k: (k, j))],
        out_specs=pl.BlockSpec((tm, tn), lambda j, i, k: (i, j)),
        out_shape=jax.ShapeDtypeStruct((M, N), out_dtype),
        scratch_shapes=[pltpu.VMEM((tm, tn), F32)],
        compiler_params=_cparams(3),
        name="matmul_tiledk",
    )(a, w)


def _res_kernel(x_ref, y_ref, gp_ref, gn_ref, xo_ref, hn_ref):
    xn = x_ref[...] + _rms(y_ref[...].astype(F32), gp_ref[...])
    xo_ref[...] = xn
    hn_ref[...] = _rms(xn, gn_ref[...]).astype(hn_ref.dtype)


def residual_norm(x, y, g_post, g_next, tm=256):
    M, D = x.shape
    tm = _tile(tm, M)
    row = pl.BlockSpec((tm, D), lambda i: (i, 0))
    vec = pl.BlockSpec((1, D), lambda i: (0, 0))
    return pl.pallas_call(
        _res_kernel, grid=(M // tm,),
        in_specs=[row, row, vec, vec], out_specs=[row, row],
        out_shape=[jax.ShapeDtypeStruct((M, D), F32), jax.ShapeDtypeStruct((M, D), BF16)],
        compiler_params=_cparams(1), name="residual_norm",
    )(x, y, g_post.reshape(1, D), g_next.reshape(1, D))


def _sgu_kernel(u_ref, v_ref, lg_ref, lb_ref, w_ref, bt_ref, o_ref, wm_ref, *, groups):
    @pl.when(pl.program_id(0) == 0)
    def _():
        t = lax.broadcasted_iota(jnp.int32, (CHUNK, CHUNK), 0)
        s = lax.broadcasted_iota(jnp.int32, (CHUNK, CHUNK), 1)
        for g in range(groups):
            wm_ref[g] = jnp.where(t >= s, w_ref[g], 0.0).astype(BF16)

    v = v_ref[...].astype(F32)
    vc = v - jnp.mean(v, axis=-1, keepdims=True)
    vn = vc * lax.rsqrt(jnp.mean(vc * vc, axis=-1, keepdims=True) + EPS)
    vn = (vn * lg_ref[...] + lb_ref[...]).astype(BF16)
    tm, width = vn.shape
    gd = width // groups
    for c in range(tm // CHUNK):
        rows = slice(c * CHUNK, (c + 1) * CHUNK)
        for g in range(groups):
            cols = slice(g * gd, (g + 1) * gd)
            mixed = jnp.dot(wm_ref[g], vn[rows, cols], preferred_element_type=F32)
            mixed = mixed + bt_ref[:, g:g + 1]
            o_ref[rows, cols] = (u_ref[rows, cols].astype(F32) * mixed).astype(o_ref.dtype)


def sgu_mix(z, ln_g, ln_b, sgu_w, sgu_b, tm=256):
    M, W2 = z.shape
    W = W2 // 2
    G = sgu_w.shape[0]
    tm = _tile(tm, M)
    return pl.pallas_call(
        functools.partial(_sgu_kernel, groups=G),
        grid=(M // tm,),
        in_specs=[pl.BlockSpec((tm, W), lambda i: (i, 0)),
                  pl.BlockSpec((tm, W), lambda i: (i, 1)),
                  pl.BlockSpec((1, W), lambda i: (0, 0)),
                  pl.BlockSpec((1, W), lambda i: (0, 0)),
                  pl.BlockSpec((G, CHUNK, CHUNK), lambda i: (0, 0, 0)),
                  pl.BlockSpec((CHUNK, G), lambda i: (0, 0))],
        out_specs=pl.BlockSpec((tm, W), lambda i: (i, 0)),
        out_shape=jax.ShapeDtypeStruct((M, W), BF16),
        scratch_shapes=[pltpu.VMEM((G, CHUNK, CHUNK), BF16)],
        compiler_params=_cparams(1),
        name="sgu_mix",
    )(z, z, ln_g.reshape(1, W), ln_b.reshape(1, W), sgu_w, sgu_b.T)


def _xattn_kernel(q_ref, k_ref, v_ref, o_ref, *, heads, scale):
    hd = q_ref.shape[-1] // heads
    for h in range(heads):
        cols = slice(h * hd, (h + 1) * hd)
        s = lax.dot_general(q_ref[:, cols], k_ref[0, :, cols], (((1,), (1,)), ((), ())),
                            preferred_element_type=F32) * scale
        p = jnp.exp(s - jnp.max(s, axis=-1, keepdims=True))
        p = p / jnp.sum(p, axis=-1, keepdims=True)
        o = jnp.dot(p.astype(BF16), v_ref[0, :, cols], preferred_element_type=F32)
        o_ref[:, cols] = o.astype(o_ref.dtype)


def cross_attention(q, kv, batch, tm=512):
    M, D = q.shape
    S = M // batch
    n_mem = kv.shape[1]
    tm = _tile(tm, S)
    spt = S // tm
    return pl.pallas_call(
        functools.partial(_xattn_kernel, heads=X_HEADS, scale=(D // X_HEADS) ** -0.5),
        grid=(batch, spt),
        in_specs=[pl.BlockSpec((tm, D), lambda b, m: (b * spt + m, 0)),
                  pl.BlockSpec((1, n_mem, D), lambda b, m: (b, 0, 0)),
                  pl.BlockSpec((1, n_mem, D), lambda b, m: (b, 0, 1))],
        out_specs=pl.BlockSpec((tm, D), lambda b, m: (b * spt + m, 0)),
        out_shape=jax.ShapeDtypeStruct((M, D), BF16),
        compiler_params=_cparams(2),
        name="cross_attention",
    )(q, kv, kv)


def _gates_kernel(hn_ref, w_ref, b_ref, o_ref):
    acc = lax.dot_general(w_ref[...].astype(BF16), hn_ref[...], (((1,), (1,)), ((), ())),
                          preferred_element_type=F32)
    o_ref[...] = GATE_CAP * jnp.tanh((acc + b_ref[...]) / GATE_CAP)


def mlstm_gates(hn, w_gates_t, gate_b, tm=512):
    M, D = hn.shape
    n_gates = w_gates_t.shape[0]
    tm = _tile(tm, M)
    return pl.pallas_call(
        _gates_kernel,
        grid=(M // tm,),
        in_specs=[pl.BlockSpec((tm, D), lambda i: (i, 0)),
                  pl.BlockSpec((n_gates, D), lambda i: (0, 0)),
                  pl.BlockSpec((n_gates, 1), lambda i: (0, 0))],
        out_specs=pl.BlockSpec((n_gates, tm), lambda i: (0, i)),
        out_shape=jax.ShapeDtypeStruct((n_gates, M), F32),
        compiler_params=_cparams(1),
        name="mlstm_gates",
    )(hn, w_gates_t, gate_b.reshape(n_gates, 1))


def _conv_silu(x, tail_ref, w):
    prev = tail_ref[...]
    rid = lax.broadcasted_iota(jnp.int32, prev.shape, 0)
    acc = x * w[CONV_WIDTH - 1:CONV_WIDTH, :]
    for r in range(1, CONV_WIDTH):
        rolled = pltpu.roll(x, r, 0)
        head = jnp.where(rid < r, pltpu.roll(prev, r, 0), rolled[:CONV_TAIL])
        shifted = jnp.concatenate([head, rolled[CONV_TAIL:]], axis=0)
        acc = acc + shifted * w[CONV_WIDTH - 1 - r:CONV_WIDTH - r, :]
    tail_ref[...] = x[x.shape[0] - CONV_TAIL:, :]
    return acc * _sigmoid(acc)


def _mlstm_kernel(q_ref, k_ref, v_ref, o_ref, ig_ref, fg_ref, cq_ref, ck_ref, hg_ref,
                  y_ref, s_ref, n_ref, m_ref, qt_ref, kt_ref, *, heads):
    @pl.when(pl.program_id(2) == 0)
    def _():
        s_ref[...] = jnp.zeros_like(s_ref)
        n_ref[...] = jnp.zeros_like(n_ref)
        m_ref[...] = jnp.zeros_like(m_ref)
        qt_ref[...] = jnp.zeros_like(qt_ref)
        kt_ref[...] = jnp.zeros_like(kt_ref)

    dk = q_ref.shape[-1] // heads
    dv = v_ref.shape[-1] // heads
    for h in range(heads):
        qc = slice(h * dk, (h + 1) * dk)
        vc = slice(h * dv, (h + 1) * dv)
        q = _conv_silu(q_ref[:, qc].astype(F32), qt_ref.at[h], cq_ref[:, qc]) * dk ** -0.5
        k = _conv_silu(k_ref[:, qc].astype(F32), kt_ref.at[h], ck_ref[:, qc])
        _mlstm_head(q, k, v_ref[:, vc], o_ref[:, vc], ig_ref[h], fg_ref[h], hg_ref[:, vc],
                    y_ref.at[:, vc], s_ref.at[h], n_ref.at[h], m_ref.at[h])


def _mlstm_head(q, k, vb, ob, i_row, f_row, hg, y_ref, s_ref, n_ref, m_ref):
    L = CHUNK
    qb = q.astype(BF16)
    kb = k.astype(BF16)
    v = vb.astype(F32)

    t_id = lax.broadcasted_iota(jnp.int32, (L, L), 0)
    s_id = lax.broadcasted_iota(jnp.int32, (L, L), 1)
    eye = t_id == s_id
    causal = t_id >= s_id

    logf_row = -(jnp.maximum(-f_row, 0.0) + jnp.log1p(jnp.exp(-jnp.abs(f_row))))
    logf_b = jnp.broadcast_to(logf_row, (L, L))
    i_b = jnp.broadcast_to(i_row, (L, L))
    logf_col = jnp.sum(jnp.where(eye, logf_b, 0.0), axis=1, keepdims=True)
    i_col = jnp.sum(jnp.where(eye, i_b, 0.0), axis=1, keepdims=True)
    b_col = jnp.sum(jnp.where(causal, logf_b, 0.0), axis=1, keepdims=True)
    b_row = jnp.sum(jnp.where(t_id <= s_id, jnp.broadcast_to(logf_col, (L, L)), 0.0),
                    axis=0, keepdims=True)
    g_tot = jnp.sum(logf_row, axis=1, keepdims=True)
    m0 = m_ref[:, 0:1]

    d_log = jnp.where(causal, b_col - b_row + i_row, -jnp.inf)
    m_inter = b_col + m0
    m_t = jnp.maximum(m_inter, jnp.max(d_log, axis=1, keepdims=True))
    qk = lax.dot_general(qb, kb, (((1,), (1,)), ((), ())), preferred_element_type=F32)
    scores = qk * jnp.exp(d_log - m_t)
    inter = jnp.exp(m_inter - m_t)
    num = (jnp.dot(scores.astype(BF16), vb, preferred_element_type=F32)
           + inter * jnp.dot(qb, s_ref[...].astype(BF16), preferred_element_type=F32))
    den = (jnp.sum(scores, axis=1, keepdims=True)
           + inter * jnp.sum(q * n_ref[...], axis=1, keepdims=True))
    h = num / jnp.maximum(jnp.abs(den), jnp.exp(-m_t))
    y_ref[...] = (_sigmoid(ob.astype(F32)) * _rms(h, hg)).astype(y_ref.dtype)

    w_log = g_tot - b_col + i_col
    a = jnp.max(w_log, axis=0, keepdims=True)
    w = jnp.exp(w_log - a)
    s_loc = lax.dot_general(kb, (w * v).astype(BF16), (((0,), (0,)), ((), ())),
                            preferred_element_type=F32)
    n_loc = jnp.sum(w * k, axis=0, keepdims=True)
    m_new = jnp.maximum(g_tot + m0, a)
    sc_prev = jnp.exp(g_tot + m0 - m_new)
    sc_loc = jnp.exp(a - m_new)
    s_ref[...] = sc_prev * s_ref[...] + sc_loc * s_loc
    n_ref[...] = sc_prev * n_ref[...] + sc_loc * n_loc
    m_ref[...] = jnp.broadcast_to(m_new, m_ref.shape)


def mlstm_core(proj, gates_t, conv_w, head_g, batch):
    M = proj.shape[0]
    H = MLSTM_HEADS
    G = MLSTM_HEADS_PER_STEP
    assert H % G == 0
    ng = H // G
    qkw = conv_w.shape[1]
    dk = qkw // (2 * H)
    dv = head_g.shape[0] // H
    nc = M // batch // CHUNK
    assert qkw % (G * dv) == 0
    v0 = qkw // (G * dv)
    gates4 = gates_t.reshape(2 * ng, G, 1, M)
    row = lambda b, c: b * nc + c
    return pl.pallas_call(
        functools.partial(_mlstm_kernel, heads=G),
        grid=(batch, ng, nc),
        in_specs=[pl.BlockSpec((CHUNK, G * dk), lambda b, g, c: (row(b, c), g)),
                  pl.BlockSpec((CHUNK, G * dk), lambda b, g, c: (row(b, c), ng + g)),
                  pl.BlockSpec((CHUNK, G * dv), lambda b, g, c: (row(b, c), v0 + g)),
                  pl.BlockSpec((CHUNK, G * dv), lambda b, g, c: (row(b, c), v0 + ng + g)),
                  pl.BlockSpec((None, G, 1, CHUNK), lambda b, g, c: (g, 0, 0, row(b, c))),
                  pl.BlockSpec((None, G, 1, CHUNK), lambda b, g, c: (ng + g, 0, 0, row(b, c))),
                  pl.BlockSpec((CONV_WIDTH, G * dk), lambda b, g, c: (0, g)),
                  pl.BlockSpec((CONV_WIDTH, G * dk), lambda b, g, c: (0, ng + g)),
                  pl.BlockSpec((1, G * dv), lambda b, g, c: (0, g))],
        out_specs=pl.BlockSpec((CHUNK, G * dv), lambda b, g, c: (row(b, c), g)),
        out_shape=jax.ShapeDtypeStruct((M, H * dv), BF16),
        scratch_shapes=[pltpu.VMEM((G, dk, dv), F32), pltpu.VMEM((G, 1, dk), F32),
                        pltpu.VMEM((G, 1, LANES), F32),
                        pltpu.VMEM((G, CONV_TAIL, dk), F32), pltpu.VMEM((G, CONV_TAIL, dk), F32)],
        compiler_params=_cparams(3),
        name="mlstm_core",
    )(proj, proj, proj, proj, gates4, gates4, conv_w, conv_w, head_g.reshape(1, H * dv))


def _res_router_kernel(x_ref, y_ref, gp_ref, gn_ref, w2_ref, xo_ref, idx_ref, wt_ref):
    xn = x_ref[...] + _rms(y_ref[...].astype(F32), gp_ref[...])
    xo_ref[...] = xn
    hn = _rms(xn, gn_ref[...])
    hi = hn.astype(BF16)
    lo = (hn - hi.astype(F32)).astype(BF16)
    t = jnp.dot(hi, w2_ref[...], preferred_element_type=F32)
    logits = (t[:, :LANES] + t[:, LANES:]
              + jnp.dot(lo, w2_ref[:, :LANES], preferred_element_type=F32))
    lane = lax.broadcasted_iota(jnp.int32, logits.shape, 1)
    neg = -jnp.inf
    l1 = jnp.where(lane < N_EXPERTS, logits, neg)
    m1 = jnp.max(l1, axis=1, keepdims=True)
    i1 = jnp.min(jnp.where(l1 == m1, lane, LANES), axis=1, keepdims=True)
    l2 = jnp.where(lane == i1, neg, l1)
    m2 = jnp.max(l2, axis=1, keepdims=True)
    i2 = jnp.min(jnp.where(l2 == m2, lane, LANES), axis=1, keepdims=True)
    r = jnp.exp(m2 - m1)
    w1 = 1.0 / (1.0 + r)
    w2 = r / (1.0 + r)
    idx_ref[...] = jnp.where(lane == 0, i1, jnp.where(lane == 1, i2, 0))
    wt_ref[...] = jnp.where(lane == 0, w1, jnp.where(lane == 1, w2, 0.0))


def residual_router(x, y, g_post, g_next, w_router, tm=256):
    M, D = x.shape
    tm = _tile(tm, M)
    w_pad = jnp.pad(w_router, ((0, 0), (0, LANES - w_router.shape[1])))
    w_hi = w_pad.astype(BF16)
    w_lo = (w_pad - w_hi.astype(F32)).astype(BF16)
    row = pl.BlockSpec((tm, D), lambda i: (i, 0))
    vec = pl.BlockSpec((1, D), lambda i: (0, 0))
    out = pl.BlockSpec((tm, LANES), lambda i: (i, 0))
    return pl.pallas_call(
        _res_router_kernel, grid=(M // tm,),
        in_specs=[row, row, vec, vec, pl.BlockSpec((D, 2 * LANES), lambda i: (0, 0))],
        out_specs=[row, out, out],
        out_shape=[jax.ShapeDtypeStruct((M, D), F32),
                   jax.ShapeDtypeStruct((M, LANES), jnp.int32),
                   jax.ShapeDtypeStruct((M, LANES), F32)],
        compiler_params=_cparams(1), name="residual_router",
    )(x, y, g_post.reshape(1, D), g_next.reshape(1, D),
      jnp.concatenate([w_hi, w_lo], axis=1))


def _row_copy(src_hbm, src_row, dst_ref, dst_row, sem):
    return pltpu.make_async_copy(src_hbm.at[pl.ds(src_row, 1)],
                                 dst_ref.at[pl.ds(dst_row, 1)], sem)


def _dispatch_kernel(tok_ref, nt_ref, x_hbm, g_ref, o_ref, buf_ref, sem_ref, *, tm):
    i = pl.program_id(0)
    nt = nt_ref[0]
    n_blocks = tm // GATHER_ROWS

    def start_rows(tile, slot, blk):
        for q in range(GATHER_ROWS):
            r = blk * GATHER_ROWS + q
            _row_copy(x_hbm, tok_ref[tile * tm + r], buf_ref.at[slot], r,
                      sem_ref.at[slot]).start()

    def wait_rows(slot, blk):
        for q in range(GATHER_ROWS):
            _row_copy(x_hbm, 0, buf_ref.at[slot], blk * GATHER_ROWS + q,
                      sem_ref.at[slot]).wait()

    def norm_rows(slot, blk):
        rows = pl.ds(pl.multiple_of(blk * GATHER_ROWS, GATHER_ROWS), GATHER_ROWS)
        o_ref[rows, :] = _rms(buf_ref[slot, rows, :], g_ref[...]).astype(o_ref.dtype)

    def loop(body):
        lax.fori_loop(0, n_blocks, lambda blk, c: (body(blk), c)[1], 0, unroll=GATHER_UNROLL)

    ahead = GATHER_SLOTS - 1

    @pl.when(i == 0)
    def _():
        for t in range(ahead):
            @pl.when(t < nt)
            def _(t=t):
                loop(lambda blk: start_rows(t, t, blk))

    slot = i % GATHER_SLOTS

    @pl.when(i < nt)
    def _():
        loop(lambda blk: wait_rows(slot, blk))

    @pl.when(i + ahead < nt)
    def _():
        def body(blk):
            start_rows(i + ahead, (i + ahead) % GATHER_SLOTS, blk)
            norm_rows(slot, blk)
        loop(body)

    @pl.when((i < nt) & (i + ahead >= nt))
    def _():
        loop(lambda blk: norm_rows(slot, blk))

    @pl.when(i >= nt)
    def _():
        o_ref[...] = jnp.zeros_like(o_ref)


def moe_dispatch(x, g, token_of_slot, n_tiles, tm):
    T, D = x.shape
    P = token_of_slot.shape[0]
    return pl.pallas_call(
        functools.partial(_dispatch_kernel, tm=tm),
        grid_spec=pltpu.PrefetchScalarGridSpec(
            num_scalar_prefetch=2,
            grid=(P // tm,),
            in_specs=[pl.BlockSpec(memory_space=pl.ANY),
                      pl.BlockSpec((1, D), lambda i, tok, nt: (0, 0))],
            out_specs=pl.BlockSpec((tm, D), lambda i, tok, nt: (i, 0)),
            scratch_shapes=[pltpu.VMEM((GATHER_SLOTS, tm, D), F32),
                            pltpu.SemaphoreType.DMA((GATHER_SLOTS,))]),
        out_shape=jax.ShapeDtypeStruct((P, D), BF16),
        compiler_params=_cparams(1),
        name="moe_dispatch",
    )(token_of_slot, n_tiles, x, g.reshape(1, D))


def _weights_changed(te_ref, i):
    return (i == 0) | (te_ref[i] != te_ref[jnp.maximum(i - 1, 0)])


def _stage_copies(w_hbms, e, j, tn, stage_refs, sem_ref):
    col = pl.multiple_of(j * tn, tn)
    return [pltpu.make_async_copy(w.at[e, :, pl.ds(col, tn)], s, sem_ref.at[n])
            for n, (w, s) in enumerate(zip(w_hbms, stage_refs))]


def _restage_weights(te_ref, nx_ref, w_hbms, stage_refs, wb_refs, sem_ref, *, tn, nj):
    j = pl.program_id(0)
    i = pl.program_id(1)

    @pl.when((j == 0) & (i == 0))
    def _():
        for c in _stage_copies(w_hbms, te_ref[0], 0, tn, stage_refs, sem_ref):
            c.start()

    @pl.when(_weights_changed(te_ref, i))
    def _():
        for c in _stage_copies(w_hbms, te_ref[i], j, tn, stage_refs, sem_ref):
            c.wait()
        for s, wb in zip(stage_refs, wb_refs):
            wb[...] = s[...].astype(BF16)
        nxt = nx_ref[i]
        last = nxt < 0
        e_next = jnp.where(last, te_ref[0], nxt)
        j_next = jnp.where(last, j + 1, j)

        @pl.when(j_next < nj)
        def _():
            for c in _stage_copies(w_hbms, e_next, j_next, tn, stage_refs, sem_ref):
                c.start()


def _for_used_rows(n_used, o_ref, compute):
    tm = o_ref.shape[0]
    classes = sorted({min(c, tm) for c in MOE_ROW_CLASSES} | {tm})
    lo = 0
    for r in classes:
        @pl.when((n_used > lo) & (n_used <= r))
        def _(r=r):
            o_ref[:r, :] = compute(r).astype(o_ref.dtype)
            if r < tm:
                o_ref[r:, :] = jnp.zeros((tm - r, o_ref.shape[1]), o_ref.dtype)
        lo = r

    @pl.when(n_used == 0)
    def _():
        o_ref[...] = jnp.zeros_like(o_ref)


def _moe_up_kernel(te_ref, nt_ref, nx_ref, nu_ref, a_ref, wg_hbm, wu_hbm, o_ref,
                   sg_ref, su_ref, wgb_ref, wub_ref, sem_ref, *, tn, nj):
    _restage_weights(te_ref, nx_ref, (wg_hbm, wu_hbm), (sg_ref, su_ref),
                     (wgb_ref, wub_ref), sem_ref, tn=tn, nj=nj)

    def compute(r):
        a = a_ref[:r, :]
        g = jnp.dot(a, wgb_ref[...], preferred_element_type=F32)
        u = jnp.dot(a, wub_ref[...], preferred_element_type=F32)
        return g * _sigmoid(g) * u

    _for_used_rows(nu_ref[pl.program_id(1)], o_ref, compute)


def _moe_down_kernel(te_ref, nt_ref, nx_ref, nu_ref, a_ref, w_hbm, o_ref,
                     s_ref, wb_ref, sem_ref, *, tn, nj):
    _restage_weights(te_ref, nx_ref, (w_hbm,), (s_ref,), (wb_ref,), sem_ref, tn=tn, nj=nj)
    _for_used_rows(nu_ref[pl.program_id(1)], o_ref,
                   lambda r: jnp.dot(a_ref[:r, :], wb_ref[...], preferred_element_type=F32))


def _moe_grouped(kernel_fn, name, a, weights, plan, tm, tn, out_dtype):
    te, n_tiles, nxt, n_used = plan
    P, K = a.shape
    N = weights[0].shape[2]
    tn = _tile(tn, N)
    nj = N // tn
    used = lambda i, nt: jnp.minimum(i, nt[0] - 1)
    n_w = len(weights)
    return pl.pallas_call(
        functools.partial(kernel_fn, tn=tn, nj=nj),
        grid_spec=pltpu.PrefetchScalarGridSpec(
            num_scalar_prefetch=4,
            grid=(nj, P // tm),
            in_specs=[pl.BlockSpec((tm, K), lambda j, i, te, nt, nx, nu: (used(i, nt), 0))]
                     + [pl.BlockSpec(memory_space=pl.ANY)] * n_w,
            out_specs=pl.BlockSpec((tm, tn), lambda j, i, te, nt, nx, nu: (i, j)),
            scratch_shapes=[pltpu.VMEM((K, tn), F32)] * n_w + [pltpu.VMEM((K, tn), BF16)] * n_w
                           + [pltpu.SemaphoreType.DMA((n_w,))]),
        out_shape=jax.ShapeDtypeStruct((P, N), out_dtype),
        compiler_params=_cparams(2),
        name=name,
    )(te, n_tiles, nxt, n_used, a, *weights)


def moe_up(xs, wg, wu, plan, tm, tn=512):
    return _moe_grouped(_moe_up_kernel, "moe_up", xs, (wg, wu), plan, tm, tn, BF16)


def moe_down(hs, wd, plan, tm, tn=1024):
    return _moe_grouped(_moe_down_kernel, "moe_down", hs, (wd,), plan, tm, tn, F32)


def _combine_kernel(slot_ref, x_ref, wt_ref, gp_ref, ys_hbm, xo_ref, buf_ref, sem_ref, *, tm):
    i = pl.program_id(0)
    n_blocks = tm // COMBINE_ROWS

    def start_rows(tile, s, blk):
        for q in range(COMBINE_ROWS):
            r = blk * COMBINE_ROWS + q
            for k in range(TOP_K):
                _row_copy(ys_hbm, slot_ref[(tile * tm + r) * TOP_K + k], buf_ref.at[s, k], r,
                          sem_ref.at[s]).start()

    def wait_rows(s, blk):
        for q in range(COMBINE_ROWS):
            for k in range(TOP_K):
                _row_copy(ys_hbm, 0, buf_ref.at[s, k], blk * COMBINE_ROWS + q,
                          sem_ref.at[s]).wait()

    def combine_rows(s, blk):
        rows = pl.ds(pl.multiple_of(blk * COMBINE_ROWS, COMBINE_ROWS), COMBINE_ROWS)
        y = (wt_ref[rows, 0:1] * buf_ref[s, 0, rows, :]
             + wt_ref[rows, 1:2] * buf_ref[s, 1, rows, :])
        xo_ref[rows, :] = x_ref[rows, :] + _rms(y, gp_ref[...])

    def loop(body):
        lax.fori_loop(0, n_blocks, lambda blk, c: (body(blk), c)[1], 0, unroll=COMBINE_UNROLL)

    n = pl.num_programs(0)
    ahead = GATHER_SLOTS - 1

    @pl.when(i == 0)
    def _():
        for t in range(ahead):
            @pl.when(t < n)
            def _(t=t):
                loop(lambda blk: start_rows(t, t, blk))

    s = i % GATHER_SLOTS
    loop(lambda blk: wait_rows(s, blk))

    @pl.when(i + ahead < n)
    def _():
        def body(blk):
            start_rows(i + ahead, (i + ahead) % GATHER_SLOTS, blk)
            combine_rows(s, blk)
        loop(body)

    @pl.when(i + ahead >= n)
    def _():
        loop(lambda blk: combine_rows(s, blk))


def moe_combine_residual(x, ys, slot, wts, g_post, tm=256):
    M, D = x.shape
    tm = _tile(tm, M)
    row = pl.BlockSpec((tm, D), lambda i, sl: (i, 0))
    return pl.pallas_call(
        functools.partial(_combine_kernel, tm=tm),
        grid_spec=pltpu.PrefetchScalarGridSpec(
            num_scalar_prefetch=1,
            grid=(M // tm,),
            in_specs=[row, pl.BlockSpec((tm, LANES), lambda i, sl: (i, 0)),
                      pl.BlockSpec((1, D), lambda i, sl: (0, 0)),
                      pl.BlockSpec(memory_space=pl.ANY)],
            out_specs=row,
            scratch_shapes=[pltpu.VMEM((GATHER_SLOTS, TOP_K, tm, D), F32),
                            pltpu.SemaphoreType.DMA((GATHER_SLOTS,))]),
        out_shape=jax.ShapeDtypeStruct((M, D), F32),
        compiler_params=_cparams(1), name="moe_combine_residual",
    )(slot, x, wts, g_post.reshape(1, D), ys)


def _moe_plan(idx, n_tokens, tm):
    e_flat = idx.reshape(-1)
    onehot = (e_flat[:, None] == jnp.arange(N_EXPERTS)[None, :]).astype(jnp.int32)
    counts = jnp.sum(onehot, axis=0)
    rank = jnp.sum((jnp.cumsum(onehot, axis=0) - onehot) * onehot, axis=1)
    tiles_per = (counts + tm - 1) // tm
    tile_end = jnp.cumsum(tiles_per)
    tile_start = tile_end - tiles_per
    slot = (tile_start[e_flat] * tm + rank).astype(jnp.int32)
    n_tiles_max = (n_tokens * TOP_K) // tm + N_EXPERTS
    n_tiles = tile_end[-1]
    tile_ids = jnp.minimum(jnp.arange(n_tiles_max), n_tiles - 1)
    expert_of = lambda t: jnp.minimum(
        jnp.sum((tile_end[None, :] <= t[:, None]).astype(jnp.int32), axis=1), N_EXPERTS - 1)
    te = expert_of(tile_ids)
    group_end = tile_end[te]
    nxt = jnp.where(group_end < n_tiles, expert_of(jnp.minimum(group_end, n_tiles - 1)), -1)
    all_ids = jnp.arange(n_tiles_max)
    n_used = jnp.where(all_ids < n_tiles,
                       jnp.clip(counts[te] - (all_ids - tile_start[te]) * tm, 0, tm), 0)
    token_of_pair = jnp.arange(n_tokens * TOP_K, dtype=jnp.int32) // TOP_K
    token_of_slot = jnp.zeros((n_tiles_max * tm,), jnp.int32).at[slot].set(token_of_pair)
    plan = (te.astype(jnp.int32), n_tiles.reshape(1).astype(jnp.int32), nxt.astype(jnp.int32),
            n_used.astype(jnp.int32))
    return slot, token_of_slot, plan


def kernel(x, mem, mem_norm, mem_kv, l0_norm_mix_pre, l0_mix_in, l0_sgu_ln_g, l0_sgu_ln_b, l0_sgu_w, l0_sgu_b, l0_mix_out, l0_norm_mix_post, l0_norm_x_pre, l0_xq, l0_xo, l0_norm_x_post, l0_norm_ffn_pre, l0_ffn_gate, l0_ffn_up, l0_ffn_down, l0_norm_ffn_post, l1_norm_mix_pre, l1_mix_in, l1_gate_b, l1_conv, l1_head_norm, l1_mix_out, l1_norm_mix_post, l1_norm_x_pre, l1_xq, l1_xo, l1_norm_x_post, l1_norm_ffn_pre, l1_router, l1_moe_gate, l1_moe_up, l1_moe_down, l1_norm_ffn_post):
    B, S, D = x.shape
    T = B * S
    n_mem = mem.shape[1]
    xf = x.reshape(T, D)

    memn = rmsnorm_rows(mem.reshape(B * n_mem, D), mem_norm)
    kv = matmul_tiledk(memn, mem_kv, BF16, tn=512, tk=D).reshape(B, n_mem, 2 * D)

    hn = rmsnorm_rows(xf, l0_norm_mix_pre)
    z = matmul_fullk(hn, l0_mix_in, BF16, act="gelu")
    y = sgu_mix(z, l0_sgu_ln_g, l0_sgu_ln_b, l0_sgu_w, l0_sgu_b)
    y = matmul_fullk(y, l0_mix_out, BF16)
    xf, hn = residual_norm(xf, y, l0_norm_mix_post, l0_norm_x_pre)
    o = cross_attention(matmul_fullk(hn, l0_xq, BF16), kv, B)
    y = matmul_fullk(o, l0_xo, BF16)
    xf, hn = residual_norm(xf, y, l0_norm_x_post, l0_norm_ffn_pre)
    hmid = swiglu_up(hn, l0_ffn_gate, l0_ffn_up)
    y = matmul_tiledk(hmid, l0_ffn_down, BF16)
    xf, hn = residual_norm(xf, y, l0_norm_ffn_post, l1_norm_mix_pre)

    n_gates = 2 * MLSTM_HEADS
    n_main = l1_mix_in.shape[1] - n_gates
    w_in_t = l1_mix_in.T
    proj = matmul_fullk(hn, w_in_t, BF16, w_rows=(0, n_main))
    gates_t = mlstm_gates(hn, w_in_t[n_main:], l1_gate_b)
    y = mlstm_core(proj, gates_t, l1_conv, l1_head_norm, B)
    y = matmul_fullk(y, l1_mix_out, BF16)
    xf, hn = residual_norm(xf, y, l1_norm_mix_post, l1_norm_x_pre)
    o = cross_attention(matmul_fullk(hn, l1_xq, BF16), kv, B)
    y = matmul_fullk(o, l1_xo, BF16)
    xf, idx, wts = residual_router(xf, y, l1_norm_x_post, l1_norm_ffn_pre, l1_router)
    tm = min(MOE_TM, T)
    slot, token_of_slot, plan = _moe_plan(idx[:, :TOP_K], T, tm)
    xs = moe_dispatch(xf, l1_norm_ffn_pre, token_of_slot, plan[1], tm)
    hs = moe_up(xs, l1_moe_gate, l1_moe_up, plan, tm)
    ys = moe_down(hs, l1_moe_down, plan, tm)
    xf = moe_combine_residual(xf, ys, slot, wts, l1_norm_ffn_post)
    return xf.reshape(B, S, D)
```
